```python
import jax
import jax.numpy as jnp
from jax import lax
import numpy as np

D_MODEL = 2048
BATCH = 1
SEQ = 8192
DEPTH = 1

HEAD_DIM = 64
D_MIX = D_MODEL
RWKV_WIDTH = D_MIX // 2
RWKV_HEADS = RWKV_WIDTH // HEAD_DIM
ATTN_WIDTH = D_MIX - RWKV_WIDTH
ATTN_Q_HEADS = ATTN_WIDTH // HEAD_DIM
ATTN_KV_HEADS = 4
ATTN_GROUP = ATTN_Q_HEADS // ATTN_KV_HEADS
ATTN_KV_WIDTH = ATTN_KV_HEADS * HEAD_DIM
WINDOW = 128
ROPE_THETA = 10000.0
DECAY_LORA = 64
AAA_LORA = 64
GATE_LORA = 160
RWKV_COLS = 3 * RWKV_WIDTH + DECAY_LORA + AAA_LORA + GATE_LORA
ATTN_COLS = ATTN_WIDTH + 2 * ATTN_KV_WIDTH
N_IN_COLS = RWKV_COLS + ATTN_COLS
N_GROUPS = 8
EXPERTS_PER_GROUP = 8
N_EXPERTS = N_GROUPS * EXPERTS_PER_GROUP
TOP_K_INNER = 2
D_EXPERT = 768
MOE_BLOCK = 256
RMS_EPS = 1e-6
RWKV_GN_EPS = 64e-5

kernel_name = "hymba_rwkv7_swa_sink_hier_moe"


def rms_norm(x, gain):
    xf = x.astype(jnp.float32)
    y = xf * lax.rsqrt(jnp.mean(xf * xf, axis=-1, keepdims=True) + RMS_EPS)
    return (y * gain.astype(jnp.float32)).astype(x.dtype)


def token_shift(p):
    return jnp.concatenate([jnp.zeros_like(p[:, :1]), p[:, :-1]], axis=1)


def apply_rope(x, positions):
    inv_freq = ROPE_THETA ** (-jnp.arange(0, HEAD_DIM, 2, dtype=jnp.float32) / HEAD_DIM)
    ang = positions.astype(jnp.float32)[:, :, None, None] * inv_freq
    cos, sin = jnp.cos(ang), jnp.sin(ang)
    xf = x.astype(jnp.float32)
    x1, x2 = xf[..., : HEAD_DIM // 2], xf[..., HEAD_DIM // 2:]
    return jnp.concatenate([x1 * cos - x2 * sin, x2 * cos + x1 * sin], axis=-1).astype(x.dtype)


def _rwkv7_step(state, inp):
    r, w, k, v, a_, b_ = inp
    sa = jnp.einsum("bhvk,bhk->bhv", state, a_)
    state = state * w[..., None, :] + sa[..., :, None] * b_[..., None, :] + v[..., :, None] * k[..., None, :]
    y = jnp.einsum("bhvk,bhk->bhv", state, r)
    return state, y


def rwkv7_time_mix(p, mu, w_decay0, w_decay_up, a0, w_a_up, w_g_up, k_k, k_a, r_k, gn_w, gn_b):
    B, T, _ = p.shape
    H, N = RWKV_HEADS, HEAD_DIM
    f32 = jnp.float32
    ps = p + (token_shift(p) - p) * mu
    o1, o2, o3 = RWKV_WIDTH, 2 * RWKV_WIDTH, 3 * RWKV_WIDTH
    o4, o5 = o3 + DECAY_LORA, o3 + DECAY_LORA + AAA_LORA
    r, k, v = ps[..., :o1], ps[..., o1:o2], ps[..., o2:o3]
    wd, ad, gd = ps[..., o3:o4], ps[..., o4:o5], ps[..., o5:]
    w_log = -jax.nn.softplus(-(w_decay0 + jnp.tanh(wd) @ w_decay_up)) - 0.5
    decay = jnp.exp(-jnp.exp(w_log.astype(f32)))
    a = jax.nn.sigmoid(a0 + ad @ w_a_up)
    g = jax.nn.sigmoid(gd) @ w_g_up

    def heads(t):
        return t.astype(f32).reshape(B, T, H, N)

    r, k, v, a, decay = heads(r), heads(k), heads(v), heads(a), heads(decay)
    kk = k * k_k.astype(f32).reshape(H, N)
    kk = kk / jnp.maximum(jnp.sqrt(jnp.sum(kk * kk, axis=-1, keepdims=True)), 1e-12)
    k = k * (1.0 + (a - 1.0) * k_a.astype(f32).reshape(H, N))
    xs = tuple(jnp.moveaxis(t, 1, 0) for t in (r, decay, k, v, -kk, kk * a))
    s0 = jnp.zeros((B, H, N, N), f32)
    _, y = lax.scan(_rwkv7_step, s0, xs)
    y = jnp.moveaxis(y, 0, 1)
    mean = jnp.mean(y, axis=-1, keepdims=True)
    var = jnp.mean(jnp.square(y - mean), axis=-1, keepdims=True)
    y = (y - mean) * lax.rsqrt(var + RWKV_GN_EPS)
    y = y * gn_w.astype(f32).reshape(H, N) + gn_b.astype(f32).reshape(H, N)
    y = y + jnp.sum(r * k * r_k.astype(f32), axis=-1, keepdims=True) * v
    return (y.reshape(B, T, RWKV_WIDTH) * g.astype(f32)).astype(p.dtype)


def swa_sink_attention(q, k, v, positions, sinks):
    B, T = q.shape[0], q.shape[1]
    nb = T // WINDOW
    scale = HEAD_DIM ** -0.5
    q = apply_rope(q, positions) * scale
    k = apply_rope(k, positions)
    qb = q.reshape(B, nb, WINDOW, ATTN_KV_HEADS, ATTN_GROUP, HEAD_DIM)
    kb = k.reshape(B, nb, WINDOW, ATTN_KV_HEADS, HEAD_DIM)
    vb = v.reshape(B, nb, WINDOW, ATTN_KV_HEADS, HEAD_DIM)

    def band(t):
        prev = jnp.concatenate([jnp.zeros_like(t[:, :1]), t[:, :-1]], axis=1)
        return jnp.concatenate([prev, t], axis=2)

    k_band, v_band = band(kb), band(vb)
    s = jnp.einsum("bnqhgd,bnkhd->bnhgqk", qb, k_band).astype(jnp.float32)
    qi = jnp.arange(WINDOW) + WINDOW
    ki = jnp.arange(2 * WINDOW)
    rel = qi[:, None] - ki[None, :]
    local = (rel >= 0) & (rel < WINDOW)
    first_ok = (jnp.arange(nb)[:, None, None] > 0) | (ki[None, None, :] >= WINDOW)
    mask = local[None] & first_ok
    s = jnp.where(mask[None, :, None, None], s, -jnp.inf)
    sink = sinks.astype(jnp.float32).reshape(ATTN_KV_HEADS, ATTN_GROUP)[None, None, :, :, None, None]
    m = jnp.maximum(jnp.max(s, axis=-1, keepdims=True), sink)
    pexp = jnp.exp(s - m)
    denom = jnp.sum(pexp, axis=-1, keepdims=True) + jnp.exp(sink - m)
    probs = (pexp / denom).astype(v.dtype)
    o = jnp.einsum("bnhgqk,bnkhd->bnqhgd", probs, v_band)
    return o.reshape(B, T, ATTN_WIDTH)


def hierarchical_moe(x, w_rg, b_rg, w_re, b_re, w_gu, w_dn):
    B, T, D = x.shape
    xt = x.reshape(-1, D)
    n = xt.shape[0]
    m_assign = n * TOP_K_INNER
    p_group = jax.nn.softmax((xt @ w_rg + b_rg).astype(jnp.float32), axis=-1)
    pg_top, g_idx = lax.top_k(p_group, 1)
    el = (xt @ w_re + b_re).astype(jnp.float32).reshape(n, N_GROUPS, EXPERTS_PER_GROUP)
    el_sel = el[jnp.arange(n), g_idx[:, 0]]
    top_vals, e_loc = lax.top_k(el_sel, TOP_K_INNER)
    gate = pg_top * jax.nn.softmax(top_vals, axis=-1)
    eid = (g_idx * EXPERTS_PER_GROUP + e_loc).reshape(-1).astype(jnp.int32)
    tok = jnp.repeat(jnp.arange(n, dtype=jnp.int32), TOP_K_INNER)
    gate_flat = gate.reshape(-1)
    order = jnp.argsort(eid)
    eid_s, tok_s, gate_s = eid[order], tok[order], gate_flat[order]
    counts = jnp.bincount(eid, length=N_EXPERTS).astype(jnp.int32)
    padded = ((counts + MOE_BLOCK - 1) // MOE_BLOCK) * MOE_BLOCK
    pad_end = jnp.cumsum(padded)
    pad_start = pad_end - padded
    start = jnp.cumsum(counts) - counts
    dest = pad_start[eid_s] + (jnp.arange(m_assign, dtype=jnp.int32) - start[eid_s])
    n_blocks = -(-m_assign // MOE_BLOCK) + N_EXPERTS
    cap = n_blocks * MOE_BLOCK
    tok_pad = jnp.full((cap,), n, dtype=jnp.int32).at[dest].set(tok_s)
    gate_pad = jnp.zeros((cap,), x.dtype).at[dest].set(gate_s.astype(x.dtype))
    blk_start = jnp.arange(n_blocks, dtype=jnp.int32) * MOE_BLOCK
    blk_exp = jnp.minimum(jnp.searchsorted(pad_end, blk_start, side="right"), N_EXPERTS - 1)
    x_pad = jnp.concatenate([xt, jnp.zeros((1, D), xt.dtype)], axis=0)
    xb = x_pad[tok_pad].reshape(n_blocks, MOE_BLOCK, D)

    def expert_block(args):
        xblk, e = args
        hgu = xblk @ w_gu[e]
        hg, hu = hgu[:, :D_EXPERT], hgu[:, D_EXPERT:]
        return (jax.nn.silu(hg) * hu) @ w_dn[e]

    yb = lax.map(expert_block, (xb, blk_exp)).reshape(cap, D)
    y = jax.ops.segment_sum(yb * gate_pad[:, None], tok_pad, num_segments=n + 1)[:n]
    return y.reshape(B, T, D)


def setup_inputs(seed: int = 0) -> dict:
    key = jax.random.key(seed)
    ks = jax.random.split(key, 24)
    f32 = jnp.float32

    def nrm(k, shape, scale):
        return jax.random.normal(k, shape, f32) * scale

    return {
        "x": nrm(ks[0], (BATCH, SEQ, D_MODEL), 1.0),
        "positions": jnp.tile(jnp.arange(SEQ, dtype=jnp.int32)[None, :], (BATCH, 1)),
        "ln1": 1.0 + nrm(ks[1], (DEPTH, D_MODEL), 0.01),
        "w_in": nrm(ks[2], (DEPTH, D_MODEL, N_IN_COLS), D_MODEL ** -0.5),
        "mu_shift": jax.random.uniform(ks[3], (DEPTH, RWKV_COLS), f32),
        "w_decay0": jax.random.uniform(ks[4], (DEPTH, RWKV_WIDTH), f32, -6.0, 1.0),
        "w_decay_up": nrm(ks[5], (DEPTH, DECAY_LORA, RWKV_WIDTH), 0.1),
        "a0": nrm(ks[6], (DEPTH, RWKV_WIDTH), 0.5),
        "w_a_up": nrm(ks[7], (DEPTH, AAA_LORA, RWKV_WIDTH), 0.1),
        "w_g_up": nrm(ks[8], (DEPTH, GATE_LORA, RWKV_WIDTH), GATE_LORA ** -0.5),
        "k_k": 0.85 + nrm(ks[9], (DEPTH, RWKV_WIDTH), 0.05),
        "k_a": 1.0 + nrm(ks[10], (DEPTH, RWKV_WIDTH), 0.05),
        "r_k": nrm(ks[11], (DEPTH, RWKV_HEADS, HEAD_DIM), 0.1),
        "gn_w": 1.0 + nrm(ks[12], (DEPTH, RWKV_WIDTH), 0.05),
        "gn_b": nrm(ks[13], (DEPTH, RWKV_WIDTH), 0.02),
        "sinks": nrm(ks[14], (DEPTH, ATTN_Q_HEADS), 1.0),
        "w_out": nrm(ks[15], (DEPTH, D_MIX, D_MODEL), D_MIX ** -0.5),
        "ln2": 1.0 + nrm(ks[16], (DEPTH, D_MODEL), 0.01),
        "w_router_group": nrm(ks[17], (DEPTH, D_MODEL, N_GROUPS), D_MODEL ** -0.5),
        "b_router_group": nrm(ks[18], (DEPTH, N_GROUPS), 0.01),
        "w_router_expert": nrm(ks[19], (DEPTH, D_MODEL, N_EXPERTS), D_MODEL ** -0.5),
        "b_router_expert": nrm(ks[20], (DEPTH, N_EXPERTS), 0.01),
        "w_expert_gu": nrm(ks[21], (DEPTH, N_EXPERTS, D_MODEL, 2 * D_EXPERT), D_MODEL ** -0.5),
        "w_expert_down": nrm(ks[22], (DEPTH, N_EXPERTS, D_EXPERT, D_MODEL), D_EXPERT ** -0.5),
        "ln_f": 1.0 + nrm(ks[23], (D_MODEL,), 0.01),
    }


def reference(x, positions, ln1, w_in, mu_shift, w_decay0, w_decay_up, a0, w_a_up, w_g_up,
              k_k, k_a, r_k, gn_w, gn_b, sinks, w_out, ln2, w_router_group, b_router_group,
              w_router_expert, b_router_expert, w_expert_gu, w_expert_down, ln_f):
    B, T, _ = x.shape
    for l in range(DEPTH):
        h = rms_norm(x, ln1[l])
        p = h @ w_in[l]
        p_rwkv = p[..., :RWKV_COLS]
        p_attn = p[..., RWKV_COLS:]
        y_rwkv = rwkv7_time_mix(p_rwkv, mu_shift[l], w_decay0[l], w_decay_up[l], a0[l], w_a_up[l],
                                w_g_up[l], k_k[l], k_a[l], r_k[l], gn_w[l], gn_b[l])
        q = p_attn[..., :ATTN_WIDTH].reshape(B, T, ATTN_Q_HEADS, HEAD_DIM)
        kv = p_attn[..., ATTN_WIDTH:]
        k = kv[..., :ATTN_KV_WIDTH].reshape(B, T, ATTN_KV_HEADS, HEAD_DIM)
        v = kv[..., ATTN_KV_WIDTH:].reshape(B, T, ATTN_KV_HEADS, HEAD_DIM)
        y_attn = swa_sink_attention(q, k, v, positions, sinks[l])
        x = x + jnp.concatenate([y_rwkv, y_attn], axis=-1) @ w_out[l]
        h = rms_norm(x, ln2[l])
        x = x + hierarchical_moe(h, w_router_group[l], b_router_group[l], w_router_expert[l],
                                 b_router_expert[l], w_expert_gu[l], w_expert_down[l])
    return rms_norm(x, ln_f)
```

```python
import functools
import math

import jax
import jax.numpy as jnp
from jax import lax
from jax.experimental import pallas as pl
from jax.experimental.pallas import tpu as pltpu

D_MODEL = 2048
HEAD_DIM = 64
RWKV_WIDTH = 1024
RWKV_HEADS = 16
ATTN_WIDTH = 1024
ATTN_Q_HEADS = 16
ATTN_KV_HEADS = 4
ATTN_GROUP = 4
ATTN_KV_WIDTH = 256
WINDOW = 128
ROPE_THETA = 10000.0
DECAY_LORA = 64
AAA_LORA = 64
GATE_LORA = 160
GATE_LORA_PAD = 256
RWKV_COLS = 3 * RWKV_WIDTH + DECAY_LORA + AAA_LORA + GATE_LORA
RWKV_COLS_PAD = 3 * RWKV_WIDTH + DECAY_LORA + AAA_LORA + GATE_LORA_PAD
ATTN_COLS = ATTN_WIDTH + 2 * ATTN_KV_WIDTH
N_GROUPS = 8
EXPERTS_PER_GROUP = 8
N_EXPERTS = 64
TOP_K_INNER = 2
D_EXPERT = 768
MOE_BLOCK = 256
RMS_EPS = 1e-6
RWKV_GN_EPS = 64e-5
ROUTER_COLS_PAD = 128

LANES = 128
CHUNK = 64
VMEM_LIMIT = 56 * 1024 * 1024

HI = lax.Precision.HIGHEST
F32 = jnp.float32
BF16 = jnp.bfloat16


def _dot(a, b):
    return jnp.dot(a, b, preferred_element_type=F32)


def _dot_hi(a, b):
    return jnp.dot(a, b, preferred_element_type=F32, precision=HI)


def _dot_nt(a, b):
    return lax.dot_general(a, b, (((1,), (1,)), ((), ())), preferred_element_type=F32)


def _dot_tn(a, b):
    return lax.dot_general(a, b, (((0,), (0,)), ((), ())), preferred_element_type=F32)


def _params(*sem):
    return pltpu.CompilerParams(dimension_semantics=sem, vmem_limit_bytes=VMEM_LIMIT)


def _head_sum_matrix():
    r = lax.broadcasted_iota(jnp.int32, (LANES, LANES), 0) // HEAD_DIM
    c = lax.broadcasted_iota(jnp.int32, (LANES, LANES), 1) // HEAD_DIM
    return (r == c).astype(F32)


def _head_sums(x, bd):
    parts = [_dot_hi(x[:, j * LANES:(j + 1) * LANES], bd) for j in range(x.shape[1] // LANES)]
    return jnp.concatenate(parts, axis=1)


def _in_proj_kernel(x_ref, g_ref, w_ref, o_ref, h_ref):
    @pl.when(pl.program_id(1) == 0)
    def _():
        x = x_ref[...]
        y = x * lax.rsqrt(jnp.mean(x * x, axis=-1, keepdims=True) + RMS_EPS)
        h_ref[...] = (y * g_ref[...]).astype(BF16)

    o_ref[...] = _dot(h_ref[...], w_ref[...])


def _in_proj(x, gain, w, tm, tn):
    t, d = x.shape
    n = w.shape[1]
    return pl.pallas_call(
        _in_proj_kernel,
        grid=(t // tm, n // tn),
        in_specs=[
            pl.BlockSpec((tm, d), lambda i, j: (i, 0)),
            pl.BlockSpec((1, d), lambda i, j: (0, 0)),
            pl.BlockSpec((d, tn), lambda i, j: (0, j)),
        ],
        out_specs=pl.BlockSpec((tm, tn), lambda i, j: (i, j)),
        out_shape=jax.ShapeDtypeStruct((t, n), F32),
        scratch_shapes=[pltpu.VMEM((tm, d), BF16)],
        compiler_params=_params("parallel", "arbitrary"),
        name="in_proj",
    )(x, gain, w)


def _prep_kernel(p_ref, prev_ref, mu_ref, w0_ref, wdu_ref, a0_ref, wau_ref, wgu_ref, kk_ref, ka_ref, rk_ref,
                 at_ref, rt_ref, bt_ref, kt_ref, bh_ref, kh_ref, v_ref, gam_ref, g_ref, bv_ref):
    i = pl.program_id(0)
    tm = p_ref.shape[0]
    w = RWKV_WIDTH
    p = p_ref[...]
    prev_row = jnp.where(i == 0, 0.0, prev_ref[7:8, :])
    row = lax.broadcasted_iota(jnp.int32, p.shape, 0)
    shifted = jnp.where(row == 0, prev_row, pltpu.roll(p, 1, 0))
    ps = p + (shifted - p) * mu_ref[...]

    r = ps[:, 0:w]
    k = ps[:, w:2 * w]
    v = ps[:, 2 * w:3 * w]
    wd = ps[:, 3 * w:3 * w + DECAY_LORA]
    ad = ps[:, 3 * w + DECAY_LORA:3 * w + DECAY_LORA + AAA_LORA]
    gd = ps[:, 3 * w + DECAY_LORA + AAA_LORA:]

    z = w0_ref[...] + _dot_hi(jnp.tanh(wd), wdu_ref[...])
    logw = -math.exp(-0.5) * jax.nn.sigmoid(z)
    alr = jax.nn.sigmoid(a0_ref[...] + _dot_hi(ad, wau_ref[...]))
    g_ref[...] = _dot_hi(jax.nn.sigmoid(gd), wgu_ref[...])

    bd = _head_sum_matrix()
    kk = k * kk_ref[...]
    kk = kk / jnp.maximum(jnp.sqrt(_head_sums(kk * kk, bd)), 1e-12)
    kmod = k * (1.0 + (alr - 1.0) * ka_ref[...])
    b = kk * alr
    bv_ref[...] = _head_sums(r * kmod * rk_ref[...], bd) * v
    v_ref[...] = v.astype(BF16)

    rr = lax.broadcasted_iota(jnp.int32, (tm, tm), 0)
    cc = lax.broadcasted_iota(jnp.int32, (tm, tm), 1)
    tri = ((cc <= rr) & (cc // CHUNK == rr // CHUNK)).astype(F32)
    cum = _dot_hi(tri, logw)
    tot_rows = []
    for c in range(tm // CHUNK):
        last = cum[c * CHUNK + CHUNK - 1:c * CHUNK + CHUNK, :]
        gam_ref[c] = jnp.exp(last)
        tot_rows.append(jnp.broadcast_to(last, (CHUNK, w)))
    tot = jnp.concatenate(tot_rows, axis=0)

    e_neg = jnp.exp(-cum)
    e_rem = jnp.exp(tot - cum)
    at_ref[...] = (-kk * jnp.exp(cum - logw)).astype(BF16)
    rt_ref[...] = (r * jnp.exp(cum)).astype(BF16)
    bt_ref[...] = (b * e_neg).astype(BF16)
    kt_ref[...] = (kmod * e_neg).astype(BF16)
    bh_ref[...] = (b * e_rem).astype(BF16)
    kh_ref[...] = (kmod * e_rem).astype(BF16)


def _prep(p_rwkv, mu, w0, wdu, a0, wau, wgu, k_k, k_a, r_k, tm):
    t = p_rwkv.shape[0]
    w = RWKV_WIDTH
    cp = RWKV_COLS_PAD
    nc = tm // CHUNK
    row = lambda i: (i, 0)
    fixed = lambda i: (0, 0)
    vec = pl.BlockSpec((1, w), fixed)
    big_bf = jax.ShapeDtypeStruct((t, w), BF16)
    big_f32 = jax.ShapeDtypeStruct((t, w), F32)
    out_tile = pl.BlockSpec((tm, w), row)
    return pl.pallas_call(
        _prep_kernel,
        grid=(t // tm,),
        in_specs=[
            pl.BlockSpec((tm, cp), row),
            pl.BlockSpec((8, cp), lambda i: (jnp.maximum(i * (tm // 8) - 1, 0), 0)),
            pl.BlockSpec((1, cp), fixed),
            vec, pl.BlockSpec((DECAY_LORA, w), fixed),
            vec, pl.BlockSpec((AAA_LORA, w), fixed),
            pl.BlockSpec((GATE_LORA_PAD, w), fixed),
            vec, vec, vec,
        ],
        out_specs=[out_tile] * 7 + [pl.BlockSpec((nc, 1, w), lambda i: (i, 0, 0)), out_tile, out_tile],
        out_shape=[big_bf] * 7 + [jax.ShapeDtypeStruct((t // CHUNK, 1, w), F32), big_f32, big_f32],
        compiler_params=_params("parallel"),
        name="rwkv_prep",
    )(p_rwkv, p_rwkv, mu, w0, wdu, a0, wau, wgu, k_k, k_a, r_k)


def _scan_kernel(at_ref, rt_ref, bt_ref, kt_ref, bh_ref, kh_ref, v_ref, gam_ref, g_ref, bv_ref, gnw_ref, gnb_ref,
                 y_ref, h_ref):
    c = pl.program_id(1)
    n = 2 * CHUNK

    @pl.when(c == 0)
    def _():
        h_ref[...] = jnp.zeros_like(h_ref)

    lane_head = lax.broadcasted_iota(jnp.int32, (n, LANES), 1) // HEAD_DIM
    row_head = lax.broadcasted_iota(jnp.int32, (n, LANES), 0) // CHUNK
    own = lane_head == row_head

    def stack(ref):
        x = ref[...]
        return jnp.where(own, jnp.concatenate([x, x], axis=0), jnp.zeros((), x.dtype))

    a_s, r_s, b_s, k_s = stack(at_ref), stack(rt_ref), stack(bt_ref), stack(kt_ref)
    bh_s, kh_s, v_s = stack(bh_ref), stack(kh_ref), stack(v_ref)

    ti = lax.broadcasted_iota(jnp.int32, (n, n), 0)
    si = lax.broadcasted_iota(jnp.int32, (n, n), 1)
    same = (ti // CHUNK) == (si // CHUNK)
    strict = same & (si < ti)
    incl = same & (si <= ti)

    a_ab = jnp.where(strict, _dot_nt(a_s, b_s), 0.0)
    a_ak = jnp.where(strict, _dot_nt(a_s, k_s), 0.0)
    a_rb = jnp.where(incl, _dot_nt(r_s, b_s), 0.0)
    a_rk = jnp.where(incl, _dot_nt(r_s, k_s), 0.0)

    pq = jnp.concatenate([a_s.astype(F32), _dot(a_ak.astype(BF16), v_s)], axis=1)
    nk = a_ab
    span = 1
    while True:
        nk_b = nk.astype(BF16)
        pq = pq + _dot(nk_b, pq.astype(BF16))
        span *= 2
        if span >= CHUNK:
            break
        nk = _dot(nk_b, nk_b)
    pq_b = pq.astype(BF16)

    gam = jnp.broadcast_to(gam_ref[0], (LANES, LANES))
    eye = lax.broadcasted_iota(jnp.int32, (LANES, LANES), 0) == lax.broadcasted_iota(jnp.int32, (LANES, LANES), 1)
    mn = _dot_tn(bh_s, pq_b)
    m_mat = mn[:, :LANES] + jnp.where(eye, gam, 0.0)
    n_mat = mn[:, LANES:] + _dot_tn(kh_s, v_s)
    yy = _dot(a_rb.astype(BF16), pq_b)
    y_c = r_s.astype(F32) + yy[:, :LANES]
    y_n = yy[:, LANES:] + _dot(a_rk.astype(BF16), v_s)

    h = h_ref[...]
    y_st = _dot_hi(y_c, h) + y_n
    h_ref[...] = _dot_hi(m_mat, h) + n_mat
    y = y_st[:CHUNK] + y_st[CHUNK:]

    bd = _head_sum_matrix()
    mean = _dot_hi(y, bd) * (1.0 / HEAD_DIM)
    yc = y - mean
    var = _dot_hi(yc * yc, bd) * (1.0 / HEAD_DIM)
    yn = yc * lax.rsqrt(var + RWKV_GN_EPS) * gnw_ref[...] + gnb_ref[...]
    y_ref[...] = ((yn + bv_ref[...]) * g_ref[...]).astype(BF16)


def _scan(at, rt, bt, kt, bh, kh, v, gam, g, bv, gn_w, gn_b):
    t = at.shape[0]
    tile = pl.BlockSpec((CHUNK, LANES), lambda p, c: (c, p))
    vec = pl.BlockSpec((1, LANES), lambda p, c: (0, p))
    return pl.pallas_call(
        _scan_kernel,
        grid=(RWKV_WIDTH // LANES, t // CHUNK),
        in_specs=[tile] * 7 + [pl.BlockSpec((1, 1, LANES), lambda p, c: (c, 0, p)), tile, tile, vec, vec],
        out_specs=tile,
        out_shape=jax.ShapeDtypeStruct((t, RWKV_WIDTH), BF16),
        scratch_shapes=[pltpu.VMEM((LANES, LANES), F32)],
        compiler_params=_params("parallel", "arbitrary"),
        name="rwkv_scan",
    )(at, rt, bt, kt, bh, kh, v, gam, g, bv, gn_w, gn_b)


def _attn_kernel(sink_ref, qkv_ref, pos_ref, freq_ref, o_ref, kprev_ref, vprev_ref):
    nb = pl.program_id(0)

    @pl.when(nb == 0)
    def _():
        kprev_ref[...] = jnp.zeros_like(kprev_ref)
        vprev_ref[...] = jnp.zeros_like(vprev_ref)

    ang = pos_ref[...].astype(F32) * freq_ref[...]
    cos = jnp.cos(ang)
    lane = lax.broadcasted_iota(jnp.int32, ang.shape, 1)
    sin = jnp.where(lane % HEAD_DIM < HEAD_DIM // 2, -1.0, 1.0) * jnp.sin(ang)
    pr = lax.broadcasted_iota(jnp.int32, (LANES, LANES), 0)
    pc = lax.broadcasted_iota(jnp.int32, (LANES, LANES), 1)
    swap = ((pr // HEAD_DIM == pc // HEAD_DIM) & ((pr + HEAD_DIM // 2) % HEAD_DIM == pc % HEAD_DIM)).astype(F32)

    def rope(x):
        return x * cos + _dot_hi(x, swap) * sin

    scale = HEAD_DIM ** -0.5
    qi = lax.broadcasted_iota(jnp.int32, (WINDOW, 2 * WINDOW), 0) + WINDOW
    ki = lax.broadcasted_iota(jnp.int32, (WINDOW, 2 * WINDOW), 1)
    rel = qi - ki
    mask = (rel >= 0) & (rel < WINDOW) & ((nb > 0) | (ki >= WINDOW))

    k_cur = jnp.concatenate([rope(qkv_ref[:, ATTN_WIDTH + j * LANES:ATTN_WIDTH + (j + 1) * LANES])
                             for j in range(ATTN_KV_WIDTH // LANES)], axis=1)
    v_cur = qkv_ref[:, ATTN_WIDTH + ATTN_KV_WIDTH:]
    k_band = jnp.concatenate([kprev_ref[...], k_cur], axis=0)
    v_band = jnp.concatenate([vprev_ref[...], v_cur], axis=0)
    k_heads = [k_band[:, g * HEAD_DIM:(g + 1) * HEAD_DIM].astype(BF16) for g in range(ATTN_KV_HEADS)]
    v_heads = [v_band[:, g * HEAD_DIM:(g + 1) * HEAD_DIM].astype(BF16) for g in range(ATTN_KV_HEADS)]

    for j in range(ATTN_WIDTH // LANES):
        q2 = rope(qkv_ref[:, j * LANES:(j + 1) * LANES]) * scale
        outs = []
        for u in range(LANES // HEAD_DIM):
            h = j * (LANES // HEAD_DIM) + u
            g = h // ATTN_GROUP
            q = q2[:, u * HEAD_DIM:(u + 1) * HEAD_DIM].astype(BF16)
            s = jnp.where(mask, _dot_nt(q, k_heads[g]), -jnp.inf)
            sink = sink_ref[h]
            m = jnp.maximum(jnp.max(s, axis=-1, keepdims=True), sink)
            pexp = jnp.exp(s - m)
            denom = jnp.sum(pexp, axis=-1, keepdims=True) + jnp.exp(sink - m)
            outs.append(_dot((pexp / denom).astype(BF16), v_heads[g]))
        o_ref[:, j * LANES:(j + 1) * LANES] = jnp.concatenate(outs, axis=1).astype(BF16)

    kprev_ref[...] = k_cur
    vprev_ref[...] = v_cur


def _attn(qkv, pos_col, freq, sinks):
    t = qkv.shape[0]
    return pl.pallas_call(
        _attn_kernel,
        grid_spec=pltpu.PrefetchScalarGridSpec(
            num_scalar_prefetch=1,
            grid=(t // WINDOW,),
            in_specs=[
                pl.BlockSpec((WINDOW, ATTN_COLS), lambda i, s: (i, 0)),
                pl.BlockSpec((WINDOW, 1), lambda i, s: (i, 0)),
                pl.BlockSpec((1, LANES), lambda i, s: (0, 0)),
            ],
            out_specs=pl.BlockSpec((WINDOW, ATTN_WIDTH), lambda i, s: (i, 0)),
            scratch_shapes=[pltpu.VMEM((WINDOW, ATTN_KV_WIDTH), F32), pltpu.VMEM((WINDOW, ATTN_KV_WIDTH), F32)],
        ),
        out_shape=jax.ShapeDtypeStruct((t, ATTN_WIDTH), BF16),
        compiler_params=_params("arbitrary"),
        name="swa_attn",
    )(sinks, qkv, pos_col, freq)


def _out_proj_kernel(x_ref, ya_ref, yb_ref, wo_ref, g_ref, wr_ref, br_ref, x1_ref, h2_ref, lg_ref):
    x1 = x_ref[...] + _dot(ya_ref[...], wo_ref[:RWKV_WIDTH, :]) + _dot(yb_ref[...], wo_ref[RWKV_WIDTH:, :])
    x1_ref[...] = x1
    h2 = x1 * lax.rsqrt(jnp.mean(x1 * x1, axis=-1, keepdims=True) + RMS_EPS) * g_ref[...]
    h2_ref[...] = h2
    lg_ref[...] = _dot_hi(h2, wr_ref[...]) + br_ref[...]


def _out_proj(x, y_rwkv, y_attn, wo, ln2, wr, br, tm):
    t, d = x.shape
    row = lambda i: (i, 0)
    fixed = lambda i: (0, 0)
    return pl.pallas_call(
        _out_proj_kernel,
        grid=(t // tm,),
        in_specs=[
            pl.BlockSpec((tm, d), row),
            pl.BlockSpec((tm, RWKV_WIDTH), row),
            pl.BlockSpec((tm, ATTN_WIDTH), row),
            pl.BlockSpec((d, d), fixed),
            pl.BlockSpec((1, d), fixed),
            pl.BlockSpec((d, ROUTER_COLS_PAD), fixed),
            pl.BlockSpec((1, ROUTER_COLS_PAD), fixed),
        ],
        out_specs=[pl.BlockSpec((tm, d), row), pl.BlockSpec((tm, d), row), pl.BlockSpec((tm, ROUTER_COLS_PAD), row)],
        out_shape=[jax.ShapeDtypeStruct((t, d), F32), jax.ShapeDtypeStruct((t, d), F32),
                   jax.ShapeDtypeStruct((t, ROUTER_COLS_PAD), F32)],
        compiler_params=_params("parallel"),
        name="out_proj_router",
    )(x, y_rwkv, y_attn, wo, ln2, wr, br)


def _moe_kernel(exp_ref, nvalid_ref, tok_ref, dst_ref, h_hbm, wgu_ref, wdn_ref, gate_ref, y_hbm,
                xbuf, obuf, gsem, ssem):
    b = pl.program_id(0)
    nv = nvalid_ref[b]
    base = b * MOE_BLOCK

    def gather_copy(r):
        return pltpu.make_async_copy(h_hbm.at[pl.ds(tok_ref[base + r], 1), :], xbuf.at[pl.ds(r, 1), :], gsem)

    def scatter_copy(r):
        return pltpu.make_async_copy(obuf.at[pl.ds(r, 1), :], y_hbm.at[pl.ds(dst_ref[base + r], 1), :], ssem)

    @pl.when(b == 0)
    def _():
        xbuf[...] = jnp.zeros_like(xbuf)

    @pl.when(nv > 0)
    def _():
        def start_gather(r, carry):
            gather_copy(r).start()
            return carry

        def wait_gather(r, carry):
            gather_copy(r).wait()
            return carry

        lax.fori_loop(0, nv, start_gather, 0)
        lax.fori_loop(0, nv, wait_gather, 0)

        x = xbuf[...].astype(BF16)
        step = 256
        for j in range(D_EXPERT // step):
            wg = wgu_ref[0, :, j * step:(j + 1) * step].astype(BF16)
            wu = wgu_ref[0, :, D_EXPERT + j * step:D_EXPERT + (j + 1) * step].astype(BF16)
            hg = _dot(x, wg)
            hu = _dot(x, wu)
            act = (hg * jax.nn.sigmoid(hg) * hu).astype(BF16)
            part = _dot(act, wdn_ref[0, j * step:(j + 1) * step, :].astype(BF16))
            if j == 0:
                obuf[...] = part
            else:
                obuf[...] += part
        obuf[...] = obuf[...] * gate_ref[0]

        def start_scatter(r, carry):
            scatter_copy(r).start()
            return carry

        def wait_scatter(r, carry):
            scatter_copy(r).wait()
            return carry

        lax.fori_loop(0, nv, start_scatter, 0)
        lax.fori_loop(0, nv, wait_scatter, 0)


def _moe(h2, w_gu, w_dn, blk_exp, nvalid, tok_pad, dst_pad, gate_pad):
    t, d = h2.shape
    n_blocks = blk_exp.shape[0]
    return pl.pallas_call(
        _moe_kernel,
        grid_spec=pltpu.PrefetchScalarGridSpec(
            num_scalar_prefetch=4,
            grid=(n_blocks,),
            in_specs=[
                pl.BlockSpec(memory_space=pl.ANY),
                pl.BlockSpec((1, d, 2 * D_EXPERT), lambda b, e, nv, tk, ds: (e[b], 0, 0)),
                pl.BlockSpec((1, D_EXPERT, d), lambda b, e, nv, tk, ds: (e[b], 0, 0)),
                pl.BlockSpec((1, MOE_BLOCK, 1), lambda b, e, nv, tk, ds: (b, 0, 0)),
            ],
            out_specs=pl.BlockSpec(memory_space=pl.ANY),
            scratch_shapes=[
                pltpu.VMEM((MOE_BLOCK, d), F32),
                pltpu.VMEM((MOE_BLOCK, d), F32),
                pltpu.SemaphoreType.DMA,
                pltpu.SemaphoreType.DMA,
            ],
        ),
        out_shape=jax.ShapeDtypeStruct((TOP_K_INNER * t, d), F32),
        compiler_params=_params("arbitrary"),
        name="moe_experts",
    )(blk_exp, nvalid, tok_pad, dst_pad, h2, w_gu, w_dn, gate_pad)


def _route(logits, t):
    n = t
    m_assign = n * TOP_K_INNER
    p_group = jax.nn.softmax(logits[:, :N_GROUPS], axis=-1)
    pg_top, g_idx = lax.top_k(p_group, 1)
    el = logits[:, N_GROUPS:N_GROUPS + N_EXPERTS].reshape(n, N_GROUPS, EXPERTS_PER_GROUP)
    el_sel = el[jnp.arange(n), g_idx[:, 0]]
    top_vals, e_loc = lax.top_k(el_sel, TOP_K_INNER)
    gate = pg_top * jax.nn.softmax(top_vals, axis=-1)
    eid = (g_idx * EXPERTS_PER_GROUP + e_loc).reshape(-1).astype(jnp.int32)
    gate_flat = gate.reshape(-1)
    order = jnp.argsort(eid).astype(jnp.int32)
    eid_s = eid[order]
    counts = jnp.bincount(eid, length=N_EXPERTS).astype(jnp.int32)
    padded = ((counts + MOE_BLOCK - 1) // MOE_BLOCK) * MOE_BLOCK
    pad_end = jnp.cumsum(padded)
    pad_start = pad_end - padded
    start = jnp.cumsum(counts) - counts
    dest = pad_start[eid_s] + (jnp.arange(m_assign, dtype=jnp.int32) - start[eid_s])
    n_blocks = -(-m_assign // MOE_BLOCK) + N_EXPERTS
    cap = n_blocks * MOE_BLOCK
    tok_pad = jnp.zeros((cap,), jnp.int32).at[dest].set(order // TOP_K_INNER)
    dst_pad = jnp.zeros((cap,), jnp.int32).at[dest].set((order % TOP_K_INNER) * n + order // TOP_K_INNER)
    gate_pad = jnp.zeros((cap,), F32).at[dest].set(gate_flat[order])
    blk_start = jnp.arange(n_blocks, dtype=jnp.int32) * MOE_BLOCK
    blk_exp = jnp.minimum(jnp.searchsorted(pad_end, blk_start, side="right"), N_EXPERTS - 1).astype(jnp.int32)
    nvalid = jnp.clip(counts[blk_exp] - (blk_start - pad_start[blk_exp]), 0, MOE_BLOCK)
    nvalid = jnp.where(blk_start < pad_end[-1], nvalid, 0).astype(jnp.int32)
    return blk_exp, nvalid, tok_pad, dst_pad, gate_pad.reshape(n_blocks, MOE_BLOCK, 1)


def _final_kernel(x_ref, ya_ref, yb_ref, g_ref, o_ref):
    x = x_ref[...] + ya_ref[...] + yb_ref[...]
    o_ref[...] = x * lax.rsqrt(jnp.mean(x * x, axis=-1, keepdims=True) + RMS_EPS) * g_ref[...]


def _final(x1, y2, gain, tm):
    t, d = x1.shape
    nt = t // tm
    return pl.pallas_call(
        _final_kernel,
        grid=(nt,),
        in_specs=[
            pl.BlockSpec((tm, d), lambda i: (i, 0)),
            pl.BlockSpec((tm, d), lambda i: (i, 0)),
            pl.BlockSpec((tm, d), lambda i: (i + nt, 0)),
            pl.BlockSpec((1, d), lambda i: (0, 0)),
        ],
        out_specs=pl.BlockSpec((tm, d), lambda i: (i, 0)),
        out_shape=jax.ShapeDtypeStruct((t, d), F32),
        compiler_params=_params("parallel"),
        name="final_norm",
    )(x1, y2, y2, gain)


def _pad_cols(a, n):
    return jnp.pad(a, ((0, 0), (0, n - a.shape[1])))


def _rwkv_mix(p_rwkv, mu, w0, wdu, a0, wau, wgu, k_k, k_a, r_k, gn_w, gn_b):
    row = lambda a: a.reshape(1, -1)
    outs = _prep(p_rwkv, _pad_cols(row(mu), RWKV_COLS_PAD), row(w0), wdu, row(a0), wau,
                 jnp.pad(wgu, ((0, GATE_LORA_PAD - GATE_LORA), (0, 0))), row(k_k), row(k_a), row(r_k), tm=256)
    return _scan(*outs, row(gn_w), row(gn_b))


def _layer(x, positions, ln1, w_in, mu, w0, wdu, a0, wau, wgu, k_k, k_a, r_k, gn_w, gn_b, sinks, w_out, ln2,
           w_rg, b_rg, w_re, b_re, w_gu, w_dn):
    t = x.shape[0]
    row = lambda a: a.reshape(1, -1)
    w_rwkv = _pad_cols(w_in[:, :RWKV_COLS], RWKV_COLS_PAD).astype(BF16)
    w_attn = w_in[:, RWKV_COLS:].astype(BF16)
    p_rwkv = _in_proj(x, row(ln1), w_rwkv, tm=min(512, t), tn=RWKV_COLS_PAD // 3)
    qkv = _in_proj(x, row(ln1), w_attn, tm=min(512, t), tn=ATTN_COLS)

    y_rwkv = _rwkv_mix(p_rwkv, mu, w0, wdu, a0, wau, wgu, k_k, k_a, r_k, gn_w, gn_b)

    inv_freq = ROPE_THETA ** (-jnp.arange(0, HEAD_DIM, 2, dtype=F32) / HEAD_DIM)
    freq = jnp.tile(inv_freq, 2 * LANES // HEAD_DIM).reshape(1, LANES)
    y_attn = _attn(qkv, positions.reshape(t, 1), freq, sinks)

    wr = _pad_cols(jnp.concatenate([w_rg, w_re], axis=1), ROUTER_COLS_PAD)
    br = _pad_cols(row(jnp.concatenate([b_rg, b_re])), ROUTER_COLS_PAD)
    x1, h2, logits = _out_proj(x, y_rwkv, y_attn, w_out.astype(BF16), row(ln2), wr, br, tm=256)

    blk_exp, nvalid, tok_pad, dst_pad, gate_pad = _route(logits, t)
    y2 = _moe(h2, w_gu, w_dn, blk_exp, nvalid, tok_pad, dst_pad, gate_pad)
    return x1, y2


def kernel(x, positions, ln1, w_in, mu_shift, w_decay0, w_decay_up, a0, w_a_up, w_g_up, k_k, k_a, r_k, gn_w, gn_b,
           sinks, w_out, ln2, w_router_group, b_router_group, w_router_expert, b_router_expert, w_expert_gu,
           w_expert_down, ln_f):
    assert ln1.shape[0] == 1, "one trunk layer"
    outs = []
    for i in range(x.shape[0]):
        x1, y2 = _layer(x[i], positions[i], ln1[0], w_in[0], mu_shift[0], w_decay0[0], w_decay_up[0], a0[0],
                        w_a_up[0], w_g_up[0], k_k[0], k_a[0], r_k[0], gn_w[0], gn_b[0], sinks[0], w_out[0],
                        ln2[0], w_router_group[0], b_router_group[0], w_router_expert[0], b_router_expert[0],
                        w_expert_gu[0], w_expert_down[0])
        outs.append(_final(x1, y2, ln_f.reshape(1, -1), tm=256))
    return jnp.stack(outs, axis=0)
```

```python
import math

import jax
import jax.numpy as jnp
from jax import lax
from jax.experimental import pallas as pl
from jax.experimental.pallas import tpu as pltpu

D_MODEL = 2048
HEAD_DIM = 64
RWKV_WIDTH = 1024
ATTN_WIDTH = 1024
ATTN_KV_HEADS = 4
ATTN_GROUP = 4
ATTN_KV_WIDTH = 256
WINDOW = 128
ROPE_THETA = 10000.0
DECAY_LORA = 64
AAA_LORA = 64
GATE_LORA = 160
GATE_LORA_PAD = 256
RWKV_COLS = 3 * RWKV_WIDTH + DECAY_LORA + AAA_LORA + GATE_LORA
RWKV_COLS_PAD = 3 * RWKV_WIDTH + DECAY_LORA + AAA_LORA + GATE_LORA_PAD
ATTN_COLS = ATTN_WIDTH + 2 * ATTN_KV_WIDTH
N_GROUPS = 8
EXPERTS_PER_GROUP = 8
N_EXPERTS = 64
TOP_K_INNER = 2
D_EXPERT = 768
MOE_BLOCK = 256
RMS_EPS = 1e-6
RWKV_GN_EPS = 64e-5

LANES = 128
CHUNK = 64
SCAN_CHUNKS = 8
VMEM_LIMIT = 56 * 1024 * 1024

HI = lax.Precision.HIGHEST
F32 = jnp.float32
BF16 = jnp.bfloat16


def _dot(a, b):
    return jnp.dot(a, b, preferred_element_type=F32)


def _dot_hi(a, b):
    return jnp.dot(a, b, preferred_element_type=F32, precision=HI)


def _dot_nt(a, b):
    return lax.dot_general(a, b, (((1,), (1,)), ((), ())), preferred_element_type=F32)


def _dot_tn(a, b):
    return lax.dot_general(a, b, (((0,), (0,)), ((), ())), preferred_element_type=F32)


def _params(*sem):
    return pltpu.CompilerParams(dimension_semantics=sem, vmem_limit_bytes=VMEM_LIMIT)


def _head_sum_matrix():
    r = lax.broadcasted_iota(jnp.int32, (LANES, LANES), 0) // HEAD_DIM
    c = lax.broadcasted_iota(jnp.int32, (LANES, LANES), 1) // HEAD_DIM
    return (r == c).astype(F32)


def _head_sums(x, bd):
    parts = [_dot_hi(x[:, j * LANES:(j + 1) * LANES], bd) for j in range(x.shape[1] // LANES)]
    return jnp.concatenate(parts, axis=1)


def _rms(x, gain):
    return x * lax.rsqrt(jnp.mean(x * x, axis=-1, keepdims=True) + RMS_EPS) * gain


def _in_proj_kernel(x_ref, g_ref, w_ref, o_ref, h_ref):
    @pl.when(pl.program_id(1) == 0)
    def _():
        h_ref[...] = _rms(x_ref[...], g_ref[...]).astype(BF16)

    o_ref[...] = _dot(h_ref[...], w_ref[...])


def _in_proj(x, gain, w, tm, tn):
    t, d = x.shape
    n = w.shape[1]
    return pl.pallas_call(
        _in_proj_kernel,
        grid=(t // tm, n // tn),
        in_specs=[
            pl.BlockSpec((tm, d), lambda i, j: (i, 0)),
            pl.BlockSpec((1, d), lambda i, j: (0, 0)),
            pl.BlockSpec((d, tn), lambda i, j: (0, j)),
        ],
        out_specs=pl.BlockSpec((tm, tn), lambda i, j: (i, j)),
        out_shape=jax.ShapeDtypeStruct((t, n), F32),
        scratch_shapes=[pltpu.VMEM((tm, d), BF16)],
        compiler_params=_params("parallel", "arbitrary"),
        name="in_proj",
    )(x, gain, w)


def _prep_kernel(p_ref, prev_ref, mu_ref, w0_ref, wdu_ref, a0_ref, wau_ref, wgu_ref, kk_ref, ka_ref, rk_ref,
                 at_ref, rt_ref, bt_ref, kt_ref, bh_ref, kh_ref, v_ref, gam_ref, g_ref, bv_ref):
    i = pl.program_id(0)
    tm = p_ref.shape[0]
    w = RWKV_WIDTH
    p = p_ref[...]
    prev_row = jnp.where(i == 0, 0.0, prev_ref[7:8, :])
    row = lax.broadcasted_iota(jnp.int32, p.shape, 0)
    shifted = jnp.where(row == 0, prev_row, pltpu.roll(p, 1, 0))
    ps = p + (shifted - p) * mu_ref[...]

    r = ps[:, 0:w]
    k = ps[:, w:2 * w]
    v = ps[:, 2 * w:3 * w]
    wd = ps[:, 3 * w:3 * w + DECAY_LORA]
    ad = ps[:, 3 * w + DECAY_LORA:3 * w + DECAY_LORA + AAA_LORA]
    gd = ps[:, 3 * w + DECAY_LORA + AAA_LORA:]

    z = w0_ref[...] + _dot_hi(jnp.tanh(wd), wdu_ref[...])
    logw = -math.exp(-0.5) * jax.nn.sigmoid(z)
    alr = jax.nn.sigmoid(a0_ref[...] + _dot_hi(ad, wau_ref[...]))
    g_ref[...] = _dot_hi(jax.nn.sigmoid(gd), wgu_ref[...])

    bd = _head_sum_matrix()
    kk = k * kk_ref[...]
    kk = kk / jnp.maximum(jnp.sqrt(_head_sums(kk * kk, bd)), 1e-12)
    kmod = k * (1.0 + (alr - 1.0) * ka_ref[...])
    b = kk * alr
    bv_ref[...] = _head_sums(r * kmod * rk_ref[...], bd) * v
    v_ref[...] = v.astype(BF16)

    rr = lax.broadcasted_iota(jnp.int32, (tm, tm), 0)
    cc = lax.broadcasted_iota(jnp.int32, (tm, tm), 1)
    tri = ((cc <= rr) & (cc // CHUNK == rr // CHUNK)).astype(F32)
    cum = _dot_hi(tri, logw)
    tot_rows = []
    for c in range(tm // CHUNK):
        last = cum[c * CHUNK + CHUNK - 1:c * CHUNK + CHUNK, :]
        gam_ref[c] = jnp.exp(last)
        tot_rows.append(jnp.broadcast_to(last, (CHUNK, w)))
    tot = jnp.concatenate(tot_rows, axis=0)

    e_neg = jnp.exp(-cum)
    e_rem = jnp.exp(tot - cum)
    at_ref[...] = (-kk * jnp.exp(cum - logw)).astype(BF16)
    rt_ref[...] = (r * jnp.exp(cum)).astype(BF16)
    bt_ref[...] = (b * e_neg).astype(BF16)
    kt_ref[...] = (kmod * e_neg).astype(BF16)
    bh_ref[...] = (b * e_rem).astype(BF16)
    kh_ref[...] = (kmod * e_rem).astype(BF16)


def _prep(p_rwkv, mu, w0, wdu, a0, wau, wgu, k_k, k_a, r_k, tm):
    t = p_rwkv.shape[0]
    w = RWKV_WIDTH
    cp = RWKV_COLS_PAD
    nc = tm // CHUNK
    row = lambda i: (i, 0)
    fixed = lambda i: (0, 0)
    vec = pl.BlockSpec((1, w), fixed)
    big_bf = jax.ShapeDtypeStruct((t, w), BF16)
    big_f32 = jax.ShapeDtypeStruct((t, w), F32)
    out_tile = pl.BlockSpec((tm, w), row)
    return pl.pallas_call(
        _prep_kernel,
        grid=(t // tm,),
        in_specs=[
            pl.BlockSpec((tm, cp), row),
            pl.BlockSpec((8, cp), lambda i: (jnp.maximum(i * (tm // 8) - 1, 0), 0)),
            pl.BlockSpec((1, cp), fixed),
            vec, pl.BlockSpec((DECAY_LORA, w), fixed),
            vec, pl.BlockSpec((AAA_LORA, w), fixed),
            pl.BlockSpec((GATE_LORA_PAD, w), fixed),
            vec, vec, vec,
        ],
        out_specs=[out_tile] * 7 + [pl.BlockSpec((nc, 1, w), lambda i: (i, 0, 0)), out_tile, out_tile],
        out_shape=[big_bf] * 7 + [jax.ShapeDtypeStruct((t // CHUNK, 1, w), F32), big_f32, big_f32],
        compiler_params=_params("parallel"),
        name="rwkv_prep",
    )(p_rwkv, p_rwkv, mu, w0, wdu, a0, wau, wgu, k_k, k_a, r_k)


def _chunk_factors(nchunks, own, strict, incl, eye, at_ref, rt_ref, bt_ref, kt_ref, bh_ref, kh_ref, v_ref, gam_ref):
    js = range(nchunks)
    n = 2 * CHUNK

    def stack(ref):
        xs = [ref[pl.ds(j * CHUNK, CHUNK), :] for j in js]
        return [jnp.where(own, jnp.concatenate([x, x], axis=0), jnp.zeros((), x.dtype)) for x in xs]

    a_s, r_s, b_s, k_s = stack(at_ref), stack(rt_ref), stack(bt_ref), stack(kt_ref)
    bh_s, kh_s, v_s = stack(bh_ref), stack(kh_ref), stack(v_ref)

    prod = [_dot_nt(jnp.concatenate([a_s[j], r_s[j]], axis=0), jnp.concatenate([b_s[j], k_s[j]], axis=0)) for j in js]
    a_ak = [jnp.where(strict, prod[j][:n, n:], 0.0).astype(BF16) for j in js]
    a_rb = [jnp.where(incl, prod[j][n:, :n], 0.0).astype(BF16) for j in js]
    a_rk = [jnp.where(incl, prod[j][n:, n:], 0.0).astype(BF16) for j in js]

    pq = [jnp.concatenate([a_s[j].astype(F32), _dot(a_ak[j], v_s[j])], axis=1) for j in js]
    nk_b = [jnp.where(strict, prod[j][:n, :n], 0.0).astype(BF16) for j in js]
    span = 1
    while True:
        pq = [pq[j] + _dot(nk_b[j], pq[j].astype(BF16)) for j in js]
        span *= 2
        if span >= CHUNK:
            break
        nk_b = [_dot(nk_b[j], nk_b[j]).astype(BF16) for j in js]
    pq_b = [pq[j].astype(BF16) for j in js]

    mn = [_dot_tn(bh_s[j], pq_b[j]) for j in js]
    kv = [_dot_tn(kh_s[j], v_s[j]) for j in js]
    yy = [_dot(a_rb[j], pq_b[j]) for j in js]
    yv = [_dot(a_rk[j], v_s[j]) for j in js]
    out = []
    for j in js:
        m_mat = mn[j][:, :LANES] + jnp.where(eye, jnp.broadcast_to(gam_ref[j], (LANES, LANES)), 0.0)
        n_mat = mn[j][:, LANES:] + kv[j]
        y_c = r_s[j].astype(F32) + yy[j][:, :LANES]
        y_n = yy[j][:, LANES:] + yv[j]
        out.append((m_mat, n_mat, y_c, y_n))
    return out


def _scan_kernel(at_ref, rt_ref, bt_ref, kt_ref, bh_ref, kh_ref, v_ref, gam_ref, g_ref, bv_ref, gnw_ref, gnb_ref,
                 y_ref, h_ref):
    @pl.when(pl.program_id(1) == 0)
    def _():
        h_ref[...] = jnp.zeros_like(h_ref)

    n = 2 * CHUNK
    own = (lax.broadcasted_iota(jnp.int32, (n, LANES), 1) // HEAD_DIM
           == lax.broadcasted_iota(jnp.int32, (n, LANES), 0) // CHUNK)
    ti = lax.broadcasted_iota(jnp.int32, (n, n), 0)
    si = lax.broadcasted_iota(jnp.int32, (n, n), 1)
    same = (ti // CHUNK) == (si // CHUNK)
    strict = same & (si < ti)
    incl = same & (si <= ti)
    eye = ti == si

    factors = _chunk_factors(SCAN_CHUNKS, own, strict, incl, eye, at_ref, rt_ref, bt_ref, kt_ref, bh_ref, kh_ref,
                             v_ref, gam_ref)

    h = h_ref[...]
    ys = []
    for m_mat, n_mat, y_c, y_n in factors:
        y_st = _dot_hi(y_c, h) + y_n
        h = _dot_hi(m_mat, h) + n_mat
        ys.append(y_st[:CHUNK] + y_st[CHUNK:])
    h_ref[...] = h
    y = jnp.concatenate(ys, axis=0)

    bd = _head_sum_matrix()
    mean = _dot_hi(y, bd) * (1.0 / HEAD_DIM)
    yc = y - mean
    var = _dot_hi(yc * yc, bd) * (1.0 / HEAD_DIM)
    yn = yc * lax.rsqrt(var + RWKV_GN_EPS) * gnw_ref[...] + gnb_ref[...]
    y_ref[...] = ((yn + bv_ref[...]) * g_ref[...]).astype(BF16)


def _scan(at, rt, bt, kt, bh, kh, v, gam, g, bv, gn_w, gn_b):
    t = at.shape[0]
    rows = SCAN_CHUNKS * CHUNK
    tile = pl.BlockSpec((rows, LANES), lambda p, c: (c, p))
    vec = pl.BlockSpec((1, LANES), lambda p, c: (0, p))
    return pl.pallas_call(
        _scan_kernel,
        grid=(RWKV_WIDTH // LANES, t // rows),
        in_specs=[tile] * 7 + [pl.BlockSpec((SCAN_CHUNKS, 1, LANES), lambda p, c: (c, 0, p)), tile, tile, vec, vec],
        out_specs=tile,
        out_shape=jax.ShapeDtypeStruct((t, RWKV_WIDTH), BF16),
        scratch_shapes=[pltpu.VMEM((LANES, LANES), F32)],
        compiler_params=_params("parallel", "arbitrary"),
        name="rwkv_scan",
    )(at, rt, bt, kt, bh, kh, v, gam, g, bv, gn_w, gn_b)


def _attn_kernel(sink_ref, qkv_ref, pos_ref, freq_ref, o_ref, kprev_ref, vprev_ref):
    nb = pl.program_id(0)

    @pl.when(nb == 0)
    def _():
        kprev_ref[...] = jnp.zeros_like(kprev_ref)
        vprev_ref[...] = jnp.zeros_like(vprev_ref)

    ang = pos_ref[...].astype(F32) * freq_ref[...]
    cos = jnp.cos(ang)
    lane = lax.broadcasted_iota(jnp.int32, ang.shape, 1)
    sin = jnp.where(lane % HEAD_DIM < HEAD_DIM // 2, -1.0, 1.0) * jnp.sin(ang)
    pr = lax.broadcasted_iota(jnp.int32, (LANES, LANES), 0)
    pc = lax.broadcasted_iota(jnp.int32, (LANES, LANES), 1)
    swap = ((pr // HEAD_DIM == pc // HEAD_DIM) & ((pr + HEAD_DIM // 2) % HEAD_DIM == pc % HEAD_DIM)).astype(F32)

    def rope(x):
        return x * cos + _dot_hi(x, swap) * sin

    scale = HEAD_DIM ** -0.5
    qi = lax.broadcasted_iota(jnp.int32, (WINDOW, 2 * WINDOW), 0) + WINDOW
    ki = lax.broadcasted_iota(jnp.int32, (WINDOW, 2 * WINDOW), 1)
    rel = qi - ki
    mask = (rel >= 0) & (rel < WINDOW) & ((nb > 0) | (ki >= WINDOW))

    k_cur = jnp.concatenate([rope(qkv_ref[:, ATTN_WIDTH + j * LANES:ATTN_WIDTH + (j + 1) * LANES])
                             for j in range(ATTN_KV_WIDTH // LANES)], axis=1)
    v_cur = qkv_ref[:, ATTN_WIDTH + ATTN_KV_WIDTH:]
    k_band = jnp.concatenate([kprev_ref[...], k_cur], axis=0)
    v_band = jnp.concatenate([vprev_ref[...], v_cur], axis=0)
    k_heads = [k_band[:, g * HEAD_DIM:(g + 1) * HEAD_DIM].astype(BF16) for g in range(ATTN_KV_HEADS)]
    v_heads = [v_band[:, g * HEAD_DIM:(g + 1) * HEAD_DIM].astype(BF16) for g in range(ATTN_KV_HEADS)]

    for j in range(ATTN_WIDTH // LANES):
        q2 = rope(qkv_ref[:, j * LANES:(j + 1) * LANES]) * scale
        outs = []
        for u in range(LANES // HEAD_DIM):
            h = j * (LANES // HEAD_DIM) + u
            g = h // ATTN_GROUP
            q = q2[:, u * HEAD_DIM:(u + 1) * HEAD_DIM].astype(BF16)
            s = jnp.where(mask, _dot_nt(q, k_heads[g]), -jnp.inf)
            sink = sink_ref[h]
            m = jnp.maximum(jnp.max(s, axis=-1, keepdims=True), sink)
            pexp = jnp.exp(s - m)
            denom = jnp.sum(pexp, axis=-1, keepdims=True) + jnp.exp(sink - m)
            outs.append(_dot((pexp / denom).astype(BF16), v_heads[g]))
        o_ref[:, j * LANES:(j + 1) * LANES] = jnp.concatenate(outs, axis=1).astype(BF16)

    kprev_ref[...] = k_cur
    vprev_ref[...] = v_cur


def _attn(qkv, pos_col, freq, sinks):
    t = qkv.shape[0]
    return pl.pallas_call(
        _attn_kernel,
        grid_spec=pltpu.PrefetchScalarGridSpec(
            num_scalar_prefetch=1,
            grid=(t // WINDOW,),
            in_specs=[
                pl.BlockSpec((WINDOW, ATTN_COLS), lambda i, s: (i, 0)),
                pl.BlockSpec((WINDOW, 1), lambda i, s: (i, 0)),
                pl.BlockSpec((1, LANES), lambda i, s: (0, 0)),
            ],
            out_specs=pl.BlockSpec((WINDOW, ATTN_WIDTH), lambda i, s: (i, 0)),
            scratch_shapes=[pltpu.VMEM((WINDOW, ATTN_KV_WIDTH), F32), pltpu.VMEM((WINDOW, ATTN_KV_WIDTH), F32)],
        ),
        out_shape=jax.ShapeDtypeStruct((t, ATTN_WIDTH), BF16),
        compiler_params=_params("arbitrary"),
        name="swa_attn",
    )(sinks, qkv, pos_col, freq)


def _route_tile(lg, run):
    tm = lg.shape[0]
    lane = lax.broadcasted_iota(jnp.int32, lg.shape, 1)
    lane_f = lane.astype(F32)
    ninf = -jnp.inf

    def top(vals):
        best = jnp.max(vals, axis=-1, keepdims=True)
        idx = jnp.min(jnp.where(vals == best, lane_f, float(LANES)), axis=-1, keepdims=True)
        return best, idx

    gl = jnp.where(lane < N_GROUPS, lg, ninf)
    gmax, gidx = top(gl)
    pg_top = 1.0 / jnp.sum(jnp.exp(gl - gmax), axis=-1, keepdims=True)
    first = N_GROUPS + EXPERTS_PER_GROUP * gidx
    el = jnp.where((lane_f >= first) & (lane_f < first + EXPERTS_PER_GROUP), lg, ninf)
    v0, i0 = top(el)
    v1, i1 = top(jnp.where(lane_f == i0, ninf, el))
    ex = jnp.exp(v1 - v0)
    g0 = pg_top / (1.0 + ex)
    g1 = pg_top * ex / (1.0 + ex)
    e0 = i0 - N_GROUPS
    e1 = i1 - N_GROUPS

    hot0 = (lane_f == e0).astype(BF16)
    hot1 = (lane_f == e1).astype(BF16)
    rr = lax.broadcasted_iota(jnp.int32, (tm, tm), 0)
    cc = lax.broadcasted_iota(jnp.int32, (tm, tm), 1)
    before = (cc < rr).astype(BF16)
    cum = _dot(before, jnp.concatenate([hot0, hot1], axis=1))
    h0 = hot0.astype(F32)
    h1 = hot1.astype(F32)
    tot0 = jnp.sum(h0, axis=0, keepdims=True)
    tot1 = jnp.sum(h1, axis=0, keepdims=True)
    rank0 = jnp.sum(h0 * (run + cum[:, :LANES]), axis=-1, keepdims=True)
    rank1 = jnp.sum(h1 * (run + tot0 + cum[:, LANES:]), axis=-1, keepdims=True)
    ints = jnp.where(lane == 0, e0, jnp.where(lane == 1, e1, jnp.where(lane == 2, rank0, jnp.where(lane == 3, rank1, 0.0))))
    gates = jnp.where(lane == 0, g0, jnp.where(lane == 1, g1, 0.0))
    return ints.astype(jnp.int32), gates, run + tot0 + tot1


def _out_proj_kernel(x_ref, ya_ref, yb_ref, wo_ref, g_ref, wr_ref, br_ref,
                     x1_ref, h2_ref, ri_ref, rg_ref, cnt_ref, run_ref):
    @pl.when(pl.program_id(0) == 0)
    def _():
        run_ref[...] = jnp.zeros_like(run_ref)

    x1 = x_ref[...] + _dot(ya_ref[...], wo_ref[:RWKV_WIDTH, :]) + _dot(yb_ref[...], wo_ref[RWKV_WIDTH:, :])
    x1_ref[...] = x1
    h2 = _rms(x1, g_ref[...])
    h2_ref[...] = h2
    ints, gates, run = _route_tile(_dot_hi(h2, wr_ref[...]) + br_ref[...], run_ref[...])
    ri_ref[...] = ints
    rg_ref[...] = gates
    run_ref[...] = run
    cnt_ref[...] = run.astype(jnp.int32)


def _out_proj(x, y_rwkv, y_attn, wo, ln2, wr, br, tm):
    t, d = x.shape
    row = lambda i: (i, 0)
    fixed = lambda i: (0, 0)
    return pl.pallas_call(
        _out_proj_kernel,
        grid=(t // tm,),
        in_specs=[
            pl.BlockSpec((tm, d), row),
            pl.BlockSpec((tm, RWKV_WIDTH), row),
            pl.BlockSpec((tm, ATTN_WIDTH), row),
            pl.BlockSpec((d, d), fixed),
            pl.BlockSpec((1, d), fixed),
            pl.BlockSpec((d, LANES), fixed),
            pl.BlockSpec((1, LANES), fixed),
        ],
        out_specs=[pl.BlockSpec((tm, d), row), pl.BlockSpec((tm, d), row), pl.BlockSpec((tm, LANES), row),
                   pl.BlockSpec((tm, LANES), row), pl.BlockSpec((1, LANES), fixed)],
        out_shape=[jax.ShapeDtypeStruct((t, d), F32), jax.ShapeDtypeStruct((t, d), F32),
                   jax.ShapeDtypeStruct((t, LANES), jnp.int32), jax.ShapeDtypeStruct((t, LANES), F32),
                   jax.ShapeDtypeStruct((1, LANES), jnp.int32)],
        scratch_shapes=[pltpu.VMEM((1, LANES), F32)],
        compiler_params=_params("arbitrary"),
        name="out_proj_router",
    )(x, y_rwkv, y_attn, wo, ln2, wr, br)


def _moe_kernel(exp_ref, nvalid_ref, start_ref, eid_ref, rank_ref, h_hbm, wgu_ref, wdn_ref, y_hbm,
                xbuf, obuf, slot_ref, gsem, ssem):
    b = pl.program_id(0)
    nv = nvalid_ref[b]
    base = b * MOE_BLOCK
    n_tok = h_hbm.shape[0]

    @pl.when(b == 0)
    def _():
        xbuf[...] = jnp.zeros_like(xbuf)

        def place(a, carry):
            slot_ref[start_ref[eid_ref[a]] + rank_ref[a]] = a
            return carry

        lax.fori_loop(0, eid_ref.shape[0], place, 0)

    def gather_copy(r):
        tok = lax.div(slot_ref[base + r], TOP_K_INNER)
        return pltpu.make_async_copy(h_hbm.at[pl.ds(tok, 1), :], xbuf.at[pl.ds(r, 1), :], gsem)

    def scatter_copy(r):
        a = slot_ref[base + r]
        dst = lax.rem(a, TOP_K_INNER) * n_tok + lax.div(a, TOP_K_INNER)
        return pltpu.make_async_copy(obuf.at[pl.ds(r, 1), :], y_hbm.at[pl.ds(dst, 1), :], ssem)

    def run(copy_of, start):
        def body(r, carry):
            if start:
                copy_of(r).start()
            else:
                copy_of(r).wait()
            return carry

        lax.fori_loop(0, nv, body, 0)

    @pl.when(nv > 0)
    def _():
        run(gather_copy, True)
        run(gather_copy, False)

        x = xbuf[...].astype(BF16)
        step = 256
        for j in range(D_EXPERT // step):
            wg = wgu_ref[0, :, j * step:(j + 1) * step].astype(BF16)
            wu = wgu_ref[0, :, D_EXPERT + j * step:D_EXPERT + (j + 1) * step].astype(BF16)
            hg = _dot(x, wg)
            hu = _dot(x, wu)
            act = (hg * jax.nn.sigmoid(hg) * hu).astype(BF16)
            part = _dot(act, wdn_ref[0, j * step:(j + 1) * step, :].astype(BF16))
            if j == 0:
                obuf[...] = part
            else:
                obuf[...] += part

        run(scatter_copy, True)
        run(scatter_copy, False)


def _moe(h2, w_gu, w_dn, blk_exp, nvalid, pad_start, eid, rank):
    t, d = h2.shape
    n_blocks = blk_exp.shape[0]
    imap = lambda b, e, *_: (e[b], 0, 0)
    return pl.pallas_call(
        _moe_kernel,
        grid_spec=pltpu.PrefetchScalarGridSpec(
            num_scalar_prefetch=5,
            grid=(n_blocks,),
            in_specs=[
                pl.BlockSpec(memory_space=pl.ANY),
                pl.BlockSpec((1, d, 2 * D_EXPERT), imap),
                pl.BlockSpec((1, D_EXPERT, d), imap),
            ],
            out_specs=pl.BlockSpec(memory_space=pl.ANY),
            scratch_shapes=[
                pltpu.VMEM((MOE_BLOCK, d), F32),
                pltpu.VMEM((MOE_BLOCK, d), F32),
                pltpu.SMEM((n_blocks * MOE_BLOCK,), jnp.int32),
                pltpu.SemaphoreType.DMA,
                pltpu.SemaphoreType.DMA,
            ],
        ),
        out_shape=jax.ShapeDtypeStruct((TOP_K_INNER * t, d), F32),
        compiler_params=_params("arbitrary"),
        name="moe_experts",
    )(blk_exp, nvalid, pad_start, eid, rank, h2, w_gu, w_dn)


def _block_tables(counts, m_assign):
    padded = ((counts + MOE_BLOCK - 1) // MOE_BLOCK) * MOE_BLOCK
    pad_end = jnp.cumsum(padded)
    pad_start = pad_end - padded
    n_blocks = -(-m_assign // MOE_BLOCK) + N_EXPERTS
    blk_start = jnp.arange(n_blocks, dtype=jnp.int32) * MOE_BLOCK
    blk_exp = jnp.minimum(jnp.sum(pad_end[None, :] <= blk_start[:, None], axis=1), N_EXPERTS - 1).astype(jnp.int32)
    nvalid = jnp.clip(counts[blk_exp] - (blk_start - pad_start[blk_exp]), 0, MOE_BLOCK)
    nvalid = jnp.where(blk_start < pad_end[-1], nvalid, 0).astype(jnp.int32)
    return blk_exp, nvalid, pad_start.astype(jnp.int32)


def _final_kernel(x_ref, ya_ref, yb_ref, rg_ref, g_ref, o_ref):
    x = x_ref[...] + rg_ref[:, 0:1] * ya_ref[...] + rg_ref[:, 1:2] * yb_ref[...]
    o_ref[...] = _rms(x, g_ref[...])


def _final(x1, y2, gates, gain, tm):
    t, d = x1.shape
    nt = t // tm
    return pl.pallas_call(
        _final_kernel,
        grid=(nt,),
        in_specs=[
            pl.BlockSpec((tm, d), lambda i: (i, 0)),
            pl.BlockSpec((tm, d), lambda i: (i, 0)),
            pl.BlockSpec((tm, d), lambda i: (i + nt, 0)),
            pl.BlockSpec((tm, LANES), lambda i: (i, 0)),
            pl.BlockSpec((1, d), lambda i: (0, 0)),
        ],
        out_specs=pl.BlockSpec((tm, d), lambda i: (i, 0)),
        out_shape=jax.ShapeDtypeStruct((t, d), F32),
        compiler_params=_params("parallel"),
        name="final_norm",
    )(x1, y2, y2, gates, gain)


def _pad_cols(a, n):
    return jnp.pad(a, ((0, 0), (0, n - a.shape[1])))


def _rwkv_mix(p_rwkv, mu, w0, wdu, a0, wau, wgu, k_k, k_a, r_k, gn_w, gn_b):
    row = lambda a: a.reshape(1, -1)
    outs = _prep(p_rwkv, _pad_cols(row(mu), RWKV_COLS_PAD), row(w0), wdu, row(a0), wau,
                 jnp.pad(wgu, ((0, GATE_LORA_PAD - GATE_LORA), (0, 0))), row(k_k), row(k_a), row(r_k), tm=256)
    return _scan(*outs, row(gn_w), row(gn_b))


def _mix_and_route(x, y_rwkv, y_attn, w_out, ln2, w_rg, b_rg, w_re, b_re):
    row = lambda a: a.reshape(1, -1)
    wr = _pad_cols(jnp.concatenate([w_rg, w_re], axis=1), LANES)
    br = _pad_cols(row(jnp.concatenate([b_rg, b_re])), LANES)
    return _out_proj(x, y_rwkv, y_attn, w_out.astype(BF16), row(ln2), wr, br, tm=256)


def _experts(h2, route_ints, counts, w_gu, w_dn):
    t = h2.shape[0]
    blk_exp, nvalid, pad_start = _block_tables(counts[0, :N_EXPERTS], t * TOP_K_INNER)
    eid = route_ints[:, 0:TOP_K_INNER].reshape(-1)
    rank = route_ints[:, TOP_K_INNER:2 * TOP_K_INNER].reshape(-1)
    return _moe(h2, w_gu, w_dn, blk_exp, nvalid, pad_start, eid, rank)


def _layer(x, positions, ln1, w_in, mu, w0, wdu, a0, wau, wgu, k_k, k_a, r_k, gn_w, gn_b, sinks, w_out, ln2,
           w_rg, b_rg, w_re, b_re, w_gu, w_dn, ln_f):
    t = x.shape[0]
    row = lambda a: a.reshape(1, -1)
    w_rwkv = _pad_cols(w_in[:, :RWKV_COLS], RWKV_COLS_PAD).astype(BF16)
    w_attn = w_in[:, RWKV_COLS:].astype(BF16)
    p_rwkv = _in_proj(x, row(ln1), w_rwkv, tm=min(512, t), tn=RWKV_COLS_PAD // 3)
    qkv = _in_proj(x, row(ln1), w_attn, tm=min(512, t), tn=ATTN_COLS)

    y_rwkv = _rwkv_mix(p_rwkv, mu, w0, wdu, a0, wau, wgu, k_k, k_a, r_k, gn_w, gn_b)

    inv_freq = ROPE_THETA ** (-jnp.arange(0, HEAD_DIM, 2, dtype=F32) / HEAD_DIM)
    freq = jnp.tile(inv_freq, 2 * LANES // HEAD_DIM).reshape(1, LANES)
    y_attn = _attn(qkv, positions.reshape(t, 1), freq, sinks)

    x1, h2, route_ints, gates, counts = _mix_and_route(x, y_rwkv, y_attn, w_out, ln2, w_rg, b_rg, w_re, b_re)
    y2 = _experts(h2, route_ints, counts, w_gu, w_dn)
    return _final(x1, y2, gates, row(ln_f), tm=256)


def kernel(x, positions, ln1, w_in, mu_shift, w_decay0, w_decay_up, a0, w_a_up, w_g_up, k_k, k_a, r_k, gn_w, gn_b,
           sinks, w_out, ln2, w_router_group, b_router_group, w_router_expert, b_router_expert, w_expert_gu,
           w_expert_down, ln_f):
    assert ln1.shape[0] == 1, "one trunk layer"
    outs = [
        _layer(x[i], positions[i], ln1[0], w_in[0], mu_shift[0], w_decay0[0], w_decay_up[0], a0[0], w_a_up[0],
               w_g_up[0], k_k[0], k_a[0], r_k[0], gn_w[0], gn_b[0], sinks[0], w_out[0], ln2[0], w_router_group[0],
               b_router_group[0], w_router_expert[0], b_router_expert[0], w_expert_gu[0], w_expert_down[0], ln_f)
        for i in range(x.shape[0])
    ]
    return jnp.stack(outs, axis=0)
```

```python
import math

import jax
import jax.numpy as jnp
from jax import lax
from jax.experimental import pallas as pl
from jax.experimental.pallas import tpu as pltpu

D_MODEL = 2048
HEAD_DIM = 64
RWKV_WIDTH = 1024
ATTN_WIDTH = 1024
ATTN_KV_HEADS = 4
ATTN_GROUP = 4
ATTN_KV_WIDTH = 256
WINDOW = 128
ROPE_THETA = 10000.0
DECAY_LORA = 64
AAA_LORA = 64
GATE_LORA = 160
GATE_LORA_PAD = 256
RWKV_COLS = 3 * RWKV_WIDTH + DECAY_LORA + AAA_LORA + GATE_LORA
RWKV_COLS_PAD = 3 * RWKV_WIDTH + DECAY_LORA + AAA_LORA + GATE_LORA_PAD
ATTN_COLS = ATTN_WIDTH + 2 * ATTN_KV_WIDTH
N_GROUPS = 8
EXPERTS_PER_GROUP = 8
N_EXPERTS = 64
TOP_K_INNER = 2
D_EXPERT = 768
MOE_BLOCK = 256
RMS_EPS = 1e-6
RWKV_GN_EPS = 64e-5

LANES = 128
CHUNK = 64
SCAN_CHUNKS = 8
VMEM_LIMIT = 56 * 1024 * 1024

F32 = jnp.float32
BF16 = jnp.bfloat16


def _dot(a, b):
    return jnp.dot(a, b, preferred_element_type=F32)


def _split(x, parts):
    out = []
    for _ in range(parts - 1):
        hi = x.astype(BF16)
        out.append(hi)
        x = x - hi.astype(F32)
    out.append(x.astype(BF16))
    return out


def _dot_x3(a, b):
    ah, al = _split(a, 2)
    bh, bl = _split(b, 2)
    return _dot(ah, bh) + (_dot(ah, bl) + _dot(al, bh))


def _dot_sel(a, sel):
    ah, al = _split(a, 2)
    return _dot(ah, sel) + _dot(al, sel)


def _dot_nt(a, b):
    return lax.dot_general(a, b, (((1,), (1,)), ((), ())), preferred_element_type=F32)


def _dot_tn(a, b):
    return lax.dot_general(a, b, (((0,), (0,)), ((), ())), preferred_element_type=F32)


def _params(*sem):
    return pltpu.CompilerParams(dimension_semantics=sem, vmem_limit_bytes=VMEM_LIMIT)


def _head_sum_matrix():
    r = lax.broadcasted_iota(jnp.int32, (LANES, LANES), 0) // HEAD_DIM
    c = lax.broadcasted_iota(jnp.int32, (LANES, LANES), 1) // HEAD_DIM
    return (r == c).astype(BF16)


def _head_sums(x, bd):
    parts = [_dot_sel(x[:, j * LANES:(j + 1) * LANES], bd) for j in range(x.shape[1] // LANES)]
    return jnp.concatenate(parts, axis=1)


def _rms(x, gain):
    return x * lax.rsqrt(jnp.mean(x * x, axis=-1, keepdims=True) + RMS_EPS) * gain


def _in_proj_kernel(x_ref, g_ref, w_ref, o_ref, h_ref):
    @pl.when(pl.program_id(1) == 0)
    def _():
        h_ref[...] = _rms(x_ref[...], g_ref[...]).astype(BF16)

    o_ref[...] = _dot(h_ref[...], w_ref[...])


def _in_proj(x, gain, w, tm, tn):
    t, d = x.shape
    n = w.shape[1]
    return pl.pallas_call(
        _in_proj_kernel,
        grid=(t // tm, n // tn),
        in_specs=[
            pl.BlockSpec((tm, d), lambda i, j: (i, 0)),
            pl.BlockSpec((1, d), lambda i, j: (0, 0)),
            pl.BlockSpec((d, tn), lambda i, j: (0, j)),
        ],
        out_specs=pl.BlockSpec((tm, tn), lambda i, j: (i, j)),
        out_shape=jax.ShapeDtypeStruct((t, n), F32),
        scratch_shapes=[pltpu.VMEM((tm, d), BF16)],
        compiler_params=_params("parallel", "arbitrary"),
        name="in_proj",
    )(x, gain, w)


def _prep_kernel(p_ref, prev_ref, mu_ref, w0_ref, wdu_ref, a0_ref, wau_ref, wgu_ref, kk_ref, ka_ref, rk_ref,
                 at_ref, rt_ref, bt_ref, kt_ref, bh_ref, kh_ref, v_ref, gam_ref, g_ref, bv_ref):
    i = pl.program_id(0)
    tm = p_ref.shape[0]
    w = RWKV_WIDTH
    p = p_ref[...]
    prev_row = jnp.where(i == 0, 0.0, prev_ref[7:8, :])
    row = lax.broadcasted_iota(jnp.int32, p.shape, 0)
    shifted = jnp.where(row == 0, prev_row, pltpu.roll(p, 1, 0))
    ps = p + (shifted - p) * mu_ref[...]

    r = ps[:, 0:w]
    k = ps[:, w:2 * w]
    v = ps[:, 2 * w:3 * w]
    wd = ps[:, 3 * w:3 * w + DECAY_LORA]
    ad = ps[:, 3 * w + DECAY_LORA:3 * w + DECAY_LORA + AAA_LORA]
    gd = ps[:, 3 * w + DECAY_LORA + AAA_LORA:]

    z = w0_ref[...] + _dot_x3(jnp.tanh(wd), wdu_ref[...])
    logw = -math.exp(-0.5) * jax.nn.sigmoid(z)
    alr = jax.nn.sigmoid(a0_ref[...] + _dot_x3(ad, wau_ref[...]))
    g_ref[...] = _dot_x3(jax.nn.sigmoid(gd), wgu_ref[...])

    bd = _head_sum_matrix()
    kk = k * kk_ref[...]
    kk = kk / jnp.maximum(jnp.sqrt(_head_sums(kk * kk, bd)), 1e-12)
    kmod = k * (1.0 + (alr - 1.0) * ka_ref[...])
    b = kk * alr
    bv_ref[...] = _head_sums(r * kmod * rk_ref[...], bd) * v
    v_ref[...] = v.astype(BF16)

    rr = lax.broadcasted_iota(jnp.int32, (tm, tm), 0)
    cc = lax.broadcasted_iota(jnp.int32, (tm, tm), 1)
    tri = ((cc <= rr) & (cc // CHUNK == rr // CHUNK)).astype(BF16)
    l1, l2, l3 = _split(logw, 3)
    cum = _dot(tri, l1) + (_dot(tri, l2) + _dot(tri, l3))
    tot_rows = []
    for c in range(tm // CHUNK):
        last = cum[c * CHUNK + CHUNK - 1:c * CHUNK + CHUNK, :]
        gam_ref[c] = jnp.exp(last)
        tot_rows.append(jnp.broadcast_to(last, (CHUNK, w)))
    tot = jnp.concatenate(tot_rows, axis=0)

    e_neg = jnp.exp(-cum)
    e_rem = jnp.exp(tot - cum)
    at_ref[...] = (-kk * jnp.exp(cum - logw)).astype(BF16)
    rt_ref[...] = (r * jnp.exp(cum)).astype(BF16)
    bt_ref[...] = (b * e_neg).astype(BF16)
    kt_ref[...] = (kmod * e_neg).astype(BF16)
    bh_ref[...] = (b * e_rem).astype(BF16)
    kh_ref[...] = (kmod * e_rem).astype(BF16)


def _prep(p_rwkv, mu, w0, wdu, a0, wau, wgu, k_k, k_a, r_k, tm):
    t = p_rwkv.shape[0]
    w = RWKV_WIDTH
    cp = RWKV_COLS_PAD
    nc = tm // CHUNK
    row = lambda i: (i, 0)
    fixed = lambda i: (0, 0)
    vec = pl.BlockSpec((1, w), fixed)
    big_bf = jax.ShapeDtypeStruct((t, w), BF16)
    big_f32 = jax.ShapeDtypeStruct((t, w), F32)
    out_tile = pl.BlockSpec((tm, w), row)
    return pl.pallas_call(
        _prep_kernel,
        grid=(t // tm,),
        in_specs=[
            pl.BlockSpec((tm, cp), row),
            pl.BlockSpec((8, cp), lambda i: (jnp.maximum(i * (tm // 8) - 1, 0), 0)),
            pl.BlockSpec((1, cp), fixed),
            vec, pl.BlockSpec((DECAY_LORA, w), fixed),
            vec, pl.BlockSpec((AAA_LORA, w), fixed),
            pl.BlockSpec((GATE_LORA_PAD, w), fixed),
            vec, vec, vec,
        ],
        out_specs=[out_tile] * 7 + [pl.BlockSpec((nc, 1, w), lambda i: (i, 0, 0)), out_tile, out_tile],
        out_shape=[big_bf] * 7 + [jax.ShapeDtypeStruct((t // CHUNK, 1, w), F32), big_f32, big_f32],
        compiler_params=_params("parallel"),
        name="rwkv_prep",
    )(p_rwkv, p_rwkv, mu, w0, wdu, a0, wau, wgu, k_k, k_a, r_k)


def _chunk_factors(nchunks, own, strict, incl, eye, at_ref, rt_ref, bt_ref, kt_ref, bh_ref, kh_ref, v_ref, gam_ref):
    js = range(nchunks)
    n = 2 * CHUNK

    def stack(ref):
        xs = [ref[pl.ds(j * CHUNK, CHUNK), :] for j in js]
        return [jnp.where(own, jnp.concatenate([x, x], axis=0), jnp.zeros((), x.dtype)) for x in xs]

    a_s, r_s, b_s, k_s = stack(at_ref), stack(rt_ref), stack(bt_ref), stack(kt_ref)
    bh_s, kh_s, v_s = stack(bh_ref), stack(kh_ref), stack(v_ref)

    prod = [_dot_nt(jnp.concatenate([a_s[j], r_s[j]], axis=0), jnp.concatenate([b_s[j], k_s[j]], axis=0)) for j in js]
    a_ak = [jnp.where(strict, prod[j][:n, n:], 0.0).astype(BF16) for j in js]
    a_rb = [jnp.where(incl, prod[j][n:, :n], 0.0).astype(BF16) for j in js]
    a_rk = [jnp.where(incl, prod[j][n:, n:], 0.0).astype(BF16) for j in js]

    pq = [jnp.concatenate([a_s[j].astype(F32), _dot(a_ak[j], v_s[j])], axis=1) for j in js]
    nk_b = [jnp.where(strict, prod[j][:n, :n], 0.0).astype(BF16) for j in js]
    span = 1
    while True:
        pq = [pq[j] + _dot(nk_b[j], pq[j].astype(BF16)) for j in js]
        span *= 2
        if span >= CHUNK:
            break
        nk_b = [_dot(nk_b[j], nk_b[j]).astype(BF16) for j in js]
    pq_b = [pq[j].astype(BF16) for j in js]

    mn = [_dot_tn(bh_s[j], pq_b[j]) for j in js]
    kv = [_dot_tn(kh_s[j], v_s[j]) for j in js]
    yy = [_dot(a_rb[j], pq_b[j]) for j in js]
    yv = [_dot(a_rk[j], v_s[j]) for j in js]
    out = []
    for j in js:
        m_mat = mn[j][:, :LANES] + jnp.where(eye, jnp.broadcast_to(gam_ref[j], (LANES, LANES)), 0.0)
        n_mat = mn[j][:, LANES:] + kv[j]
        y_c = r_s[j].astype(F32) + yy[j][:, :LANES]
        y_n = yy[j][:, LANES:] + yv[j]
        out.append((m_mat, n_mat, y_c, y_n))
    return out


def _scan_kernel(at_ref, rt_ref, bt_ref, kt_ref, bh_ref, kh_ref, v_ref, gam_ref, g_ref, bv_ref, gnw_ref, gnb_ref,
                 y_ref, h_ref):
    @pl.when(pl.program_id(1) == 0)
    def _():
        h_ref[...] = jnp.zeros_like(h_ref)

    n = 2 * CHUNK
    own = (lax.broadcasted_iota(jnp.int32, (n, LANES), 1) // HEAD_DIM
           == lax.broadcasted_iota(jnp.int32, (n, LANES), 0) // CHUNK)
    ti = lax.broadcasted_iota(jnp.int32, (n, n), 0)
    si = lax.broadcasted_iota(jnp.int32, (n, n), 1)
    same = (ti // CHUNK) == (si // CHUNK)
    strict = same & (si < ti)
    incl = same & (si <= ti)
    eye = ti == si

    factors = _chunk_factors(SCAN_CHUNKS, own, strict, incl, eye, at_ref, rt_ref, bt_ref, kt_ref, bh_ref, kh_ref,
                             v_ref, gam_ref)

    h = h_ref[...]
    ys = []
    for m_mat, n_mat, y_c, y_n in factors:
        y_st = _dot(y_c.astype(BF16), h.astype(BF16)) + y_n
        h = _dot_x3(m_mat, h) + n_mat
        ys.append(y_st[:CHUNK] + y_st[CHUNK:])
    h_ref[...] = h
    y = jnp.concatenate(ys, axis=0)

    bd = _head_sum_matrix()
    mean = _dot_sel(y, bd) * (1.0 / HEAD_DIM)
    yc = y - mean
    var = _dot_sel(yc * yc, bd) * (1.0 / HEAD_DIM)
    yn = yc * lax.rsqrt(var + RWKV_GN_EPS) * gnw_ref[...] + gnb_ref[...]
    y_ref[...] = ((yn + bv_ref[...]) * g_ref[...]).astype(BF16)


def _scan(at, rt, bt, kt, bh, kh, v, gam, g, bv, gn_w, gn_b):
    t = at.shape[0]
    rows = SCAN_CHUNKS * CHUNK
    tile = pl.BlockSpec((rows, LANES), lambda p, c: (c, p))
    vec = pl.BlockSpec((1, LANES), lambda p, c: (0, p))
    return pl.pallas_call(
        _scan_kernel,
        grid=(RWKV_WIDTH // LANES, t // rows),
        in_specs=[tile] * 7 + [pl.BlockSpec((SCAN_CHUNKS, 1, LANES), lambda p, c: (c, 0, p)), tile, tile, vec, vec],
        out_specs=tile,
        out_shape=jax.ShapeDtypeStruct((t, RWKV_WIDTH), BF16),
        scratch_shapes=[pltpu.VMEM((LANES, LANES), F32)],
        compiler_params=_params("parallel", "arbitrary"),
        name="rwkv_scan",
    )(at, rt, bt, kt, bh, kh, v, gam, g, bv, gn_w, gn_b)


def _attn_kernel(sink_ref, qkv_ref, pos_ref, freq_ref, o_ref, kprev_ref, vprev_ref):
    nb = pl.program_id(0)

    @pl.when(nb == 0)
    def _():
        kprev_ref[...] = jnp.zeros_like(kprev_ref)
        vprev_ref[...] = jnp.zeros_like(vprev_ref)

    ang = pos_ref[...].astype(F32) * freq_ref[...]
    cos = jnp.cos(ang)
    lane = lax.broadcasted_iota(jnp.int32, ang.shape, 1)
    sin = jnp.where(lane % HEAD_DIM < HEAD_DIM // 2, -1.0, 1.0) * jnp.sin(ang)
    first_half = lane % HEAD_DIM < HEAD_DIM // 2

    def rope(x):
        swapped = jnp.where(first_half, pltpu.roll(x, LANES - HEAD_DIM // 2, 1), pltpu.roll(x, HEAD_DIM // 2, 1))
        return x * cos + swapped * sin

    scale = HEAD_DIM ** -0.5
    qi = lax.broadcasted_iota(jnp.int32, (WINDOW, 2 * WINDOW), 0) + WINDOW
    ki = lax.broadcasted_iota(jnp.int32, (WINDOW, 2 * WINDOW), 1)
    rel = qi - ki
    mask = (rel >= 0) & (rel < WINDOW) & ((nb > 0) | (ki >= WINDOW))

    k_cur = jnp.concatenate([rope(qkv_ref[:, ATTN_WIDTH + j * LANES:ATTN_WIDTH + (j + 1) * LANES])
                             for j in range(ATTN_KV_WIDTH // LANES)], axis=1)
    v_cur = qkv_ref[:, ATTN_WIDTH + ATTN_KV_WIDTH:]
    k_band = jnp.concatenate([kprev_ref[...], k_cur], axis=0)
    v_band = jnp.concatenate([vprev_ref[...], v_cur], axis=0)
    k_heads = [k_band[:, g * HEAD_DIM:(g + 1) * HEAD_DIM].astype(BF16) for g in range(ATTN_KV_HEADS)]
    v_heads = [v_band[:, g * HEAD_DIM:(g + 1) * HEAD_DIM].astype(BF16) for g in range(ATTN_KV_HEADS)]

    heads = range(ATTN_WIDTH // HEAD_DIM)
    per_group = LANES // HEAD_DIM
    q_groups = [rope(qkv_ref[:, j * LANES:(j + 1) * LANES]) * scale for j in range(ATTN_WIDTH // LANES)]
    q = [q_groups[h // per_group][:, (h % per_group) * HEAD_DIM:(h % per_group + 1) * HEAD_DIM].astype(BF16)
         for h in heads]
    s = [jnp.where(mask, _dot_nt(q[h], k_heads[h // ATTN_GROUP]), -jnp.inf) for h in heads]
    probs = []
    for h in heads:
        sink = sink_ref[h]
        m = jnp.maximum(jnp.max(s[h], axis=-1, keepdims=True), sink)
        pexp = jnp.exp(s[h] - m)
        denom = jnp.sum(pexp, axis=-1, keepdims=True) + jnp.exp(sink - m)
        probs.append((pexp / denom).astype(BF16))
    o = [_dot(probs[h], v_heads[h // ATTN_GROUP]) for h in heads]
    for j in range(ATTN_WIDTH // LANES):
        o_ref[:, j * LANES:(j + 1) * LANES] = jnp.concatenate(o[j * per_group:(j + 1) * per_group],
                                                               axis=1).astype(BF16)

    kprev_ref[...] = k_cur
    vprev_ref[...] = v_cur


def _attn(qkv, pos_col, freq, sinks):
    t = qkv.shape[0]
    return pl.pallas_call(
        _attn_kernel,
        grid_spec=pltpu.PrefetchScalarGridSpec(
            num_scalar_prefetch=1,
            grid=(t // WINDOW,),
            in_specs=[
                pl.BlockSpec((WINDOW, ATTN_COLS), lambda i, s: (i, 0)),
                pl.BlockSpec((WINDOW, 1), lambda i, s: (i, 0)),
                pl.BlockSpec((1, LANES), lambda i, s: (0, 0)),
            ],
            out_specs=pl.BlockSpec((WINDOW, ATTN_WIDTH), lambda i, s: (i, 0)),
            scratch_shapes=[pltpu.VMEM((WINDOW, ATTN_KV_WIDTH), F32), pltpu.VMEM((WINDOW, ATTN_KV_WIDTH), F32)],
        ),
        out_shape=jax.ShapeDtypeStruct((t, ATTN_WIDTH), BF16),
        compiler_params=_params("arbitrary"),
        name="swa_attn",
    )(sinks, qkv, pos_col, freq)


def _route_tile(lg, run):
    tm = lg.shape[0]
    lane = lax.broadcasted_iota(jnp.int32, lg.shape, 1)
    lane_f = lane.astype(F32)
    ninf = -jnp.inf

    def top(vals):
        best = jnp.max(vals, axis=-1, keepdims=True)
        idx = jnp.min(jnp.where(vals == best, lane_f, float(LANES)), axis=-1, keepdims=True)
        return best, idx

    gl = jnp.where(lane < N_GROUPS, lg, ninf)
    gmax, gidx = top(gl)
    pg_top = 1.0 / jnp.sum(jnp.exp(gl - gmax), axis=-1, keepdims=True)
    first = N_GROUPS + EXPERTS_PER_GROUP * gidx
    el = jnp.where((lane_f >= first) & (lane_f < first + EXPERTS_PER_GROUP), lg, ninf)
    v0, i0 = top(el)
    v1, i1 = top(jnp.where(lane_f == i0, ninf, el))
    ex = jnp.exp(v1 - v0)
    g0 = pg_top / (1.0 + ex)
    g1 = pg_top * ex / (1.0 + ex)
    e0 = i0 - N_GROUPS
    e1 = i1 - N_GROUPS

    hot0 = (lane_f == e0).astype(BF16)
    hot1 = (lane_f == e1).astype(BF16)
    rr = lax.broadcasted_iota(jnp.int32, (tm, tm), 0)
    cc = lax.broadcasted_iota(jnp.int32, (tm, tm), 1)
    before = (cc < rr).astype(BF16)
    cum = _dot(before, jnp.concatenate([hot0, hot1], axis=1))
    h0 = hot0.astype(F32)
    h1 = hot1.astype(F32)
    tot0 = jnp.sum(h0, axis=0, keepdims=True)
    tot1 = jnp.sum(h1, axis=0, keepdims=True)
    rank0 = jnp.sum(h0 * (run + cum[:, :LANES]), axis=-1, keepdims=True)
    rank1 = jnp.sum(h1 * (run + tot0 + cum[:, LANES:]), axis=-1, keepdims=True)
    ints = jnp.where(lane == 0, e0, jnp.where(lane == 1, e1, jnp.where(lane == 2, rank0, jnp.where(lane == 3, rank1, 0.0))))
    gates = jnp.where(lane == 0, g0, jnp.where(lane == 1, g1, 0.0))
    return ints.astype(jnp.int32), gates, run + tot0 + tot1


def _out_proj_kernel(x_ref, ya_ref, yb_ref, wo_ref, g_ref, wr_ref, br_ref,
                     x1_ref, h2_ref, ri_ref, rg_ref, cnt_ref, run_ref):
    @pl.when(pl.program_id(0) == 0)
    def _():
        run_ref[...] = jnp.zeros_like(run_ref)

    x1 = x_ref[...] + _dot(ya_ref[...], wo_ref[:RWKV_WIDTH, :]) + _dot(yb_ref[...], wo_ref[RWKV_WIDTH:, :])
    x1_ref[...] = x1
    h2 = _rms(x1, g_ref[...])
    h2_ref[...] = h2
    ints, gates, run = _route_tile(_dot_x3(h2, wr_ref[...]) + br_ref[...], run_ref[...])
    ri_ref[...] = ints
    rg_ref[...] = gates
    run_ref[...] = run
    cnt_ref[...] = run.astype(jnp.int32)


def _out_proj(x, y_rwkv, y_attn, wo, ln2, wr, br, tm):
    t, d = x.shape
    row = lambda i: (i, 0)
    fixed = lambda i: (0, 0)
    return pl.pallas_call(
        _out_proj_kernel,
        grid=(t // tm,),
        in_specs=[
            pl.BlockSpec((tm, d), row),
            pl.BlockSpec((tm, RWKV_WIDTH), row),
            pl.BlockSpec((tm, ATTN_WIDTH), row),
            pl.BlockSpec((d, d), fixed),
            pl.BlockSpec((1, d), fixed),
            pl.BlockSpec((d, LANES), fixed),
            pl.BlockSpec((1, LANES), fixed),
        ],
        out_specs=[pl.BlockSpec((tm, d), row), pl.BlockSpec((tm, d), row), pl.BlockSpec((tm, LANES), row),
                   pl.BlockSpec((tm, LANES), row), pl.BlockSpec((1, LANES), fixed)],
        out_shape=[jax.ShapeDtypeStruct((t, d), F32), jax.ShapeDtypeStruct((t, d), F32),
                   jax.ShapeDtypeStruct((t, LANES), jnp.int32), jax.ShapeDtypeStruct((t, LANES), F32),
                   jax.ShapeDtypeStruct((1, LANES), jnp.int32)],
        scratch_shapes=[pltpu.VMEM((1, LANES), F32)],
        compiler_params=_params("arbitrary"),
        name="out_proj_router",
    )(x, y_rwkv, y_attn, wo, ln2, wr, br)


ROW_UNROLL = 8


def _for_rows(n, fn):
    full = lax.div(n, ROW_UNROLL)

    def group(i, carry):
        for u in range(ROW_UNROLL):
            fn(i * ROW_UNROLL + u)
        return carry

    def single(r, carry):
        fn(r)
        return carry

    lax.fori_loop(0, full, group, 0)
    lax.fori_loop(full * ROW_UNROLL, n, single, 0)


def _moe_kernel(exp_ref, nvalid_ref, start_ref, eid_ref, rank_ref, h_hbm, wgu_ref, wdn_ref, y_hbm,
                xbuf, obuf, slot_ref, gsem, ssem):
    b = pl.program_id(0)
    last = pl.num_programs(0) - 1
    cur = lax.rem(b, 2)
    nv = nvalid_ref[b]
    n_tok = h_hbm.shape[0]

    def gather_copy(blk, buf, r):
        tok = lax.div(slot_ref[blk * MOE_BLOCK + r], TOP_K_INNER)
        return pltpu.make_async_copy(h_hbm.at[pl.ds(tok, 1), :], xbuf.at[buf, pl.ds(r, 1), :], gsem.at[buf])

    def scatter_copy(blk, buf, r):
        a = slot_ref[blk * MOE_BLOCK + r]
        dst = lax.rem(a, TOP_K_INNER) * n_tok + lax.div(a, TOP_K_INNER)
        return pltpu.make_async_copy(obuf.at[buf, pl.ds(r, 1), :], y_hbm.at[pl.ds(dst, 1), :], ssem.at[buf])

    def gather_start(blk, buf):
        _for_rows(nvalid_ref[blk], lambda r: gather_copy(blk, buf, r).start())

    def gather_wait(blk, buf):
        _for_rows(nvalid_ref[blk], lambda r: gather_copy(blk, buf, r).wait())

    def scatter_start(blk, buf):
        _for_rows(nvalid_ref[blk], lambda r: scatter_copy(blk, buf, r).start())

    def scatter_wait(blk, buf):
        _for_rows(nvalid_ref[blk], lambda r: scatter_copy(blk, buf, r).wait())

    @pl.when(b == 0)
    def _():
        xbuf[...] = jnp.zeros_like(xbuf)

        def place(i, carry):
            for u in range(ROW_UNROLL):
                a = i * ROW_UNROLL + u
                slot_ref[start_ref[eid_ref[a]] + rank_ref[a]] = a
            return carry

        lax.fori_loop(0, eid_ref.shape[0] // ROW_UNROLL, place, 0)
        gather_start(0, 0)

    @pl.when(b < last)
    def _():
        gather_start(b + 1, 1 - cur)

    @pl.when(b >= 2)
    def _():
        scatter_wait(b - 2, cur)

    @pl.when(nv > 0)
    def _():
        gather_wait(b, cur)
        x = xbuf[cur].astype(BF16)
        out = obuf.at[cur]
        step = 256
        for j in range(D_EXPERT // step):
            wg = wgu_ref[0, :, j * step:(j + 1) * step].astype(BF16)
            wu = wgu_ref[0, :, D_EXPERT + j * step:D_EXPERT + (j + 1) * step].astype(BF16)
            hg = _dot(x, wg)
            hu = _dot(x, wu)
            act = (hg * jax.nn.sigmoid(hg) * hu).astype(BF16)
            part = _dot(act, wdn_ref[0, j * step:(j + 1) * step, :].astype(BF16))
            if j == 0:
                out[...] = part
            else:
                out[...] += part
        scatter_start(b, cur)

    @pl.when(b == last)
    def _():
        @pl.when(b >= 1)
        def _():
            scatter_wait(b - 1, 1 - cur)

        scatter_wait(b, cur)


def _moe(h2, w_gu, w_dn, blk_exp, nvalid, pad_start, eid, rank):
    t, d = h2.shape
    n_blocks = blk_exp.shape[0]
    imap = lambda b, e, *_: (e[b], 0, 0)
    return pl.pallas_call(
        _moe_kernel,
        grid_spec=pltpu.PrefetchScalarGridSpec(
            num_scalar_prefetch=5,
            grid=(n_blocks,),
            in_specs=[
                pl.BlockSpec(memory_space=pl.ANY),
                pl.BlockSpec((1, d, 2 * D_EXPERT), imap),
                pl.BlockSpec((1, D_EXPERT, d), imap),
            ],
            out_specs=pl.BlockSpec(memory_space=pl.ANY),
            scratch_shapes=[
                pltpu.VMEM((2, MOE_BLOCK, d), F32),
                pltpu.VMEM((2, MOE_BLOCK, d), F32),
                pltpu.SMEM((n_blocks * MOE_BLOCK,), jnp.int32),
                pltpu.SemaphoreType.DMA((2,)),
                pltpu.SemaphoreType.DMA((2,)),
            ],
        ),
        out_shape=jax.ShapeDtypeStruct((TOP_K_INNER * t, d), F32),
        compiler_params=_params("arbitrary"),
        name="moe_experts",
    )(blk_exp, nvalid, pad_start, eid, rank, h2, w_gu, w_dn)


def _block_tables(counts, m_assign):
    padded = ((counts + MOE_BLOCK - 1) // MOE_BLOCK) * MOE_BLOCK
    pad_end = jnp.cumsum(padded)
    pad_start = pad_end - padded
    n_blocks = -(-m_assign // MOE_BLOCK) + N_EXPERTS
    blk_start = jnp.arange(n_blocks, dtype=jnp.int32) * MOE_BLOCK
    blk_exp = jnp.minimum(jnp.sum(pad_end[None, :] <= blk_start[:, None], axis=1), N_EXPERTS - 1).astype(jnp.int32)
    nvalid = jnp.clip(counts[blk_exp] - (blk_start - pad_start[blk_exp]), 0, MOE_BLOCK)
    nvalid = jnp.where(blk_start < pad_end[-1], nvalid, 0).astype(jnp.int32)
    return blk_exp, nvalid, pad_start.astype(jnp.int32)


def _final_kernel(x_ref, ya_ref, yb_ref, rg_ref, g_ref, o_ref):
    x = x_ref[...] + rg_ref[:, 0:1] * ya_ref[...] + rg_ref[:, 1:2] * yb_ref[...]
    o_ref[...] = _rms(x, g_ref[...])


def _final(x1, y2, gates, gain, tm):
    t, d = x1.shape
    nt = t // tm
    return pl.pallas_call(
        _final_kernel,
        grid=(nt,),
        in_specs=[
            pl.BlockSpec((tm, d), lambda i: (i, 0)),
            pl.BlockSpec((tm, d), lambda i: (i, 0)),
            pl.BlockSpec((tm, d), lambda i: (i + nt, 0)),
            pl.BlockSpec((tm, LANES), lambda i: (i, 0)),
            pl.BlockSpec((1, d), lambda i: (0, 0)),
        ],
        out_specs=pl.BlockSpec((tm, d), lambda i: (i, 0)),
        out_shape=jax.ShapeDtypeStruct((t, d), F32),
        compiler_params=_params("parallel"),
        name="final_norm",
    )(x1, y2, y2, gates, gain)


def _pad_cols(a, n):
    return jnp.pad(a, ((0, 0), (0, n - a.shape[1])))


def _rwkv_mix(p_rwkv, mu, w0, wdu, a0, wau, wgu, k_k, k_a, r_k, gn_w, gn_b):
    row = lambda a: a.reshape(1, -1)
    outs = _prep(p_rwkv, _pad_cols(row(mu), RWKV_COLS_PAD), row(w0), wdu, row(a0), wau,
                 jnp.pad(wgu, ((0, GATE_LORA_PAD - GATE_LORA), (0, 0))), row(k_k), row(k_a), row(r_k), tm=256)
    return _scan(*outs, row(gn_w), row(gn_b))


def _mix_and_route(x, y_rwkv, y_attn, w_out, ln2, w_rg, b_rg, w_re, b_re):
    row = lambda a: a.reshape(1, -1)
    wr = _pad_cols(jnp.concatenate([w_rg, w_re], axis=1), LANES)
    br = _pad_cols(row(jnp.concatenate([b_rg, b_re])), LANES)
    return _out_proj(x, y_rwkv, y_attn, w_out.astype(BF16), row(ln2), wr, br, tm=256)


def _experts(h2, route_ints, counts, w_gu, w_dn):
    t = h2.shape[0]
    blk_exp, nvalid, pad_start = _block_tables(counts[0, :N_EXPERTS], t * TOP_K_INNER)
    eid = route_ints[:, 0:TOP_K_INNER].reshape(-1)
    rank = route_ints[:, TOP_K_INNER:2 * TOP_K_INNER].reshape(-1)
    return _moe(h2, w_gu, w_dn, blk_exp, nvalid, pad_start, eid, rank)


def _layer(x, positions, ln1, w_in, mu, w0, wdu, a0, wau, wgu, k_k, k_a, r_k, gn_w, gn_b, sinks, w_out, ln2,
           w_rg, b_rg, w_re, b_re, w_gu, w_dn, ln_f):
    t = x.shape[0]
    row = lambda a: a.reshape(1, -1)
    w_rwkv = _pad_cols(w_in[:, :RWKV_COLS], RWKV_COLS_PAD).astype(BF16)
    w_attn = w_in[:, RWKV_COLS:].astype(BF16)
    p_rwkv = _in_proj(x, row(ln1), w_rwkv, tm=min(512, t), tn=RWKV_COLS_PAD // 3)
    qkv = _in_proj(x, row(ln1), w_attn, tm=min(512, t), tn=ATTN_COLS)

    y_rwkv = _rwkv_mix(p_rwkv, mu, w0, wdu, a0, wau, wgu, k_k, k_a, r_k, gn_w, gn_b)

    inv_freq = ROPE_THETA ** (-jnp.arange(0, HEAD_DIM, 2, dtype=F32) / HEAD_DIM)
    freq = jnp.tile(inv_freq, 2 * LANES // HEAD_DIM).reshape(1, LANES)
    y_attn = _attn(qkv, positions.reshape(t, 1), freq, sinks)

    x1, h2, route_ints, gates, counts = _mix_and_route(x, y_rwkv, y_attn, w_out, ln2, w_rg, b_rg, w_re, b_re)
    y2 = _experts(h2, route_ints, counts, w_gu, w_dn)
    return _final(x1, y2, gates, row(ln_f), tm=256)


def kernel(x, positions, ln1, w_in, mu_shift, w_decay0, w_decay_up, a0, w_a_up, w_g_up, k_k, k_a, r_k, gn_w, gn_b,
           sinks, w_out, ln2, w_router_group, b_router_group, w_router_expert, b_router_expert, w_expert_gu,
           w_expert_down, ln_f):
    assert ln1.shape[0] == 1, "one trunk layer"
    outs = [
        _layer(x[i], positions[i], ln1[0], w_in[0], mu_shift[0], w_decay0[0], w_decay_up[0], a0[0], w_a_up[0],
               w_g_up[0], k_k[0], k_a[0], r_k[0], gn_w[0], gn_b[0], sinks[0], w_out[0], ln2[0], w_router_group[0],
               b_router_group[0], w_router_expert[0], b_router_expert[0], w_expert_gu[0], w_expert_down[0], ln_f)
        for i in range(x.shape[0])
    ]
    return jnp.stack(outs, axis=0)
```

```python
import math

import jax
import jax.numpy as jnp
from jax import lax
from jax.experimental import pallas as pl
from jax.experimental.pallas import tpu as pltpu

D_MODEL = 2048
HEAD_DIM = 64
RWKV_WIDTH = 1024
ATTN_WIDTH = 1024
ATTN_KV_HEADS = 4
ATTN_GROUP = 4
ATTN_KV_WIDTH = 256
WINDOW = 128
ROPE_THETA = 10000.0
DECAY_LORA = 64
AAA_LORA = 64
GATE_LORA = 160
GATE_LORA_PAD = 256
RWKV_COLS = 3 * RWKV_WIDTH + DECAY_LORA + AAA_LORA + GATE_LORA
RWKV_COLS_PAD = 3 * RWKV_WIDTH + DECAY_LORA + AAA_LORA + GATE_LORA_PAD
ATTN_COLS = ATTN_WIDTH + 2 * ATTN_KV_WIDTH
N_GROUPS = 8
EXPERTS_PER_GROUP = 8
N_EXPERTS = 64
TOP_K_INNER = 2
D_EXPERT = 768
MOE_BLOCK = 256
RMS_EPS = 1e-6
RWKV_GN_EPS = 64e-5

LANES = 128
CHUNK = 64
SCAN_CHUNKS = 8
VMEM_LIMIT = 56 * 1024 * 1024

F32 = jnp.float32
BF16 = jnp.bfloat16


def _dot(a, b):
    return jnp.dot(a, b, preferred_element_type=F32)


def _split(x, parts):
    out = []
    for _ in range(parts - 1):
        hi = x.astype(BF16)
        out.append(hi)
        x = x - hi.astype(F32)
    out.append(x.astype(BF16))
    return out


def _dot_x3(a, b):
    ah, al = _split(a, 2)
    bh, bl = _split(b, 2)
    return _dot(ah, bh) + (_dot(ah, bl) + _dot(al, bh))


def _dot_sel(a, sel):
    ah, al = _split(a, 2)
    return _dot(ah, sel) + _dot(al, sel)


def _dot_nt(a, b):
    return lax.dot_general(a, b, (((1,), (1,)), ((), ())), preferred_element_type=F32)


def _dot_tn(a, b):
    return lax.dot_general(a, b, (((0,), (0,)), ((), ())), preferred_element_type=F32)


def _params(*sem):
    return pltpu.CompilerParams(dimension_semantics=sem, vmem_limit_bytes=VMEM_LIMIT)


def _head_sum_matrix():
    r = lax.broadcasted_iota(jnp.int32, (LANES, LANES), 0) // HEAD_DIM
    c = lax.broadcasted_iota(jnp.int32, (LANES, LANES), 1) // HEAD_DIM
    return (r == c).astype(BF16)


def _head_sums(x, bd):
    parts = [_dot_sel(x[:, j * LANES:(j + 1) * LANES], bd) for j in range(x.shape[1] // LANES)]
    return jnp.concatenate(parts, axis=1)


def _rms(x, gain):
    return x * lax.rsqrt(jnp.mean(x * x, axis=-1, keepdims=True) + RMS_EPS) * gain


def _in_proj_kernel(x_ref, g_ref, w_ref, o_ref, h_ref):
    @pl.when(pl.program_id(1) == 0)
    def _():
        h_ref[...] = _rms(x_ref[...], g_ref[...]).astype(BF16)

    o_ref[...] = _dot(h_ref[...], w_ref[...])


def _in_proj(x, gain, w, tm, tn):
    t, d = x.shape
    n = w.shape[1]
    return pl.pallas_call(
        _in_proj_kernel,
        grid=(t // tm, n // tn),
        in_specs=[
            pl.BlockSpec((tm, d), lambda i, j: (i, 0)),
            pl.BlockSpec((1, d), lambda i, j: (0, 0)),
            pl.BlockSpec((d, tn), lambda i, j: (0, j)),
        ],
        out_specs=pl.BlockSpec((tm, tn), lambda i, j: (i, j)),
        out_shape=jax.ShapeDtypeStruct((t, n), F32),
        scratch_shapes=[pltpu.VMEM((tm, d), BF16)],
        compiler_params=_params("parallel", "arbitrary"),
        name="in_proj",
    )(x, gain, w)


def _prep_kernel(p_ref, prev_ref, mu_ref, w0_ref, wdu_ref, a0_ref, wau_ref, wgu_ref, kk_ref, ka_ref, rk_ref,
                 at_ref, rt_ref, bt_ref, kt_ref, bh_ref, kh_ref, v_ref, gam_ref, g_ref, bv_ref):
    i = pl.program_id(0)
    tm = p_ref.shape[0]
    w = RWKV_WIDTH
    p = p_ref[...]
    prev_row = jnp.where(i == 0, 0.0, prev_ref[7:8, :])
    row = lax.broadcasted_iota(jnp.int32, p.shape, 0)
    shifted = jnp.where(row == 0, prev_row, pltpu.roll(p, 1, 0))
    ps = p + (shifted - p) * mu_ref[...]

    r = ps[:, 0:w]
    k = ps[:, w:2 * w]
    v = ps[:, 2 * w:3 * w]
    wd = ps[:, 3 * w:3 * w + DECAY_LORA]
    ad = ps[:, 3 * w + DECAY_LORA:3 * w + DECAY_LORA + AAA_LORA]
    gd = ps[:, 3 * w + DECAY_LORA + AAA_LORA:]

    z = w0_ref[...] + _dot_x3(jnp.tanh(wd), wdu_ref[...])
    logw = -math.exp(-0.5) * jax.nn.sigmoid(z)
    alr = jax.nn.sigmoid(a0_ref[...] + _dot_x3(ad, wau_ref[...]))
    g_ref[...] = _dot_x3(jax.nn.sigmoid(gd), wgu_ref[...])

    bd = _head_sum_matrix()
    kk = k * kk_ref[...]
    kk = kk / jnp.maximum(jnp.sqrt(_head_sums(kk * kk, bd)), 1e-12)
    kmod = k * (1.0 + (alr - 1.0) * ka_ref[...])
    b = kk * alr
    bv_ref[...] = _head_sums(r * kmod * rk_ref[...], bd) * v
    v_ref[...] = v.astype(BF16)

    rr = lax.broadcasted_iota(jnp.int32, (tm, tm), 0)
    cc = lax.broadcasted_iota(jnp.int32, (tm, tm), 1)
    tri = ((cc <= rr) & (cc // CHUNK == rr // CHUNK)).astype(BF16)
    l1, l2, l3 = _split(logw, 3)
    cum = _dot(tri, l1) + (_dot(tri, l2) + _dot(tri, l3))
    tot_rows = []
    for c in range(tm // CHUNK):
        last = cum[c * CHUNK + CHUNK - 1:c * CHUNK + CHUNK, :]
        gam_ref[c] = jnp.exp(last)
        tot_rows.append(jnp.broadcast_to(last, (CHUNK, w)))
    tot = jnp.concatenate(tot_rows, axis=0)

    e_neg = jnp.exp(-cum)
    e_rem = jnp.exp(tot - cum)
    at_ref[...] = (-kk * jnp.exp(cum - logw)).astype(BF16)
    rt_ref[...] = (r * jnp.exp(cum)).astype(BF16)
    bt_ref[...] = (b * e_neg).astype(BF16)
    kt_ref[...] = (kmod * e_neg).astype(BF16)
    bh_ref[...] = (b * e_rem).astype(BF16)
    kh_ref[...] = (kmod * e_rem).astype(BF16)


def _prep(p_rwkv, mu, w0, wdu, a0, wau, wgu, k_k, k_a, r_k, tm):
    t = p_rwkv.shape[0]
    w = RWKV_WIDTH
    cp = RWKV_COLS_PAD
    nc = tm // CHUNK
    row = lambda i: (i, 0)
    fixed = lambda i: (0, 0)
    vec = pl.BlockSpec((1, w), fixed)
    big_bf = jax.ShapeDtypeStruct((t, w), BF16)
    big_f32 = jax.ShapeDtypeStruct((t, w), F32)
    out_tile = pl.BlockSpec((tm, w), row)
    return pl.pallas_call(
        _prep_kernel,
        grid=(t // tm,),
        in_specs=[
            pl.BlockSpec((tm, cp), row),
            pl.BlockSpec((8, cp), lambda i: (jnp.maximum(i * (tm // 8) - 1, 0), 0)),
            pl.BlockSpec((1, cp), fixed),
            vec, pl.BlockSpec((DECAY_LORA, w), fixed),
            vec, pl.BlockSpec((AAA_LORA, w), fixed),
            pl.BlockSpec((GATE_LORA_PAD, w), fixed),
            vec, vec, vec,
        ],
        out_specs=[out_tile] * 7 + [pl.BlockSpec((nc, 1, w), lambda i: (i, 0, 0)), out_tile, out_tile],
        out_shape=[big_bf] * 7 + [jax.ShapeDtypeStruct((t // CHUNK, 1, w), F32), big_f32, big_f32],
        compiler_params=_params("parallel"),
        name="rwkv_prep",
    )(p_rwkv, p_rwkv, mu, w0, wdu, a0, wau, wgu, k_k, k_a, r_k)


def _chunk_factors(nchunks, own, strict, incl, eye, at_ref, rt_ref, bt_ref, kt_ref, bh_ref, kh_ref, v_ref, gam_ref,
                   tick):
    js = range(nchunks)
    n = 2 * CHUNK

    def stack(ref):
        xs = [ref[pl.ds(j * CHUNK, CHUNK), :] for j in js]
        return [jnp.where(own, jnp.concatenate([x, x], axis=0), jnp.zeros((), x.dtype)) for x in xs]

    a_s, r_s, b_s, k_s = stack(at_ref), stack(rt_ref), stack(bt_ref), stack(kt_ref)
    bh_s, kh_s, v_s = stack(bh_ref), stack(kh_ref), stack(v_ref)

    prod = [_dot_nt(jnp.concatenate([a_s[j], r_s[j]], axis=0), jnp.concatenate([b_s[j], k_s[j]], axis=0)) for j in js]
    a_ak = [jnp.where(strict, prod[j][:n, n:], 0.0).astype(BF16) for j in js]
    a_rb = [jnp.where(incl, prod[j][n:, :n], 0.0).astype(BF16) for j in js]
    a_rk = [jnp.where(incl, prod[j][n:, n:], 0.0).astype(BF16) for j in js]

    tick()
    pq = [jnp.concatenate([a_s[j].astype(F32), _dot(a_ak[j], v_s[j])], axis=1) for j in js]
    nk_b = [jnp.where(strict, prod[j][:n, :n], 0.0).astype(BF16) for j in js]
    span = 1
    while True:
        tick()
        pq = [pq[j] + _dot(nk_b[j], pq[j].astype(BF16)) for j in js]
        span *= 2
        if span >= CHUNK:
            break
        tick()
        nk_b = [_dot(nk_b[j], nk_b[j]).astype(BF16) for j in js]
    pq_b = [pq[j].astype(BF16) for j in js]

    mn = [_dot_tn(bh_s[j], pq_b[j]) for j in js]
    kv = [_dot_tn(kh_s[j], v_s[j]) for j in js]
    yy = [_dot(a_rb[j], pq_b[j]) for j in js]
    yv = [_dot(a_rk[j], v_s[j]) for j in js]
    out = []
    for j in js:
        m_mat = mn[j][:, :LANES] + jnp.where(eye, jnp.broadcast_to(gam_ref[j], (LANES, LANES)), 0.0)
        n_mat = mn[j][:, LANES:] + kv[j]
        y_c = r_s[j].astype(F32) + yy[j][:, :LANES]
        y_n = yy[j][:, LANES:] + yv[j]
        out.append((m_mat, n_mat, y_c, y_n))
    return out


def _scan_kernel(at_ref, rt_ref, bt_ref, kt_ref, bh_ref, kh_ref, v_ref, gam_ref, g_ref, bv_ref, gnw_ref, gnb_ref,
                 y_ref, h_ref, f_ref):
    @pl.when(pl.program_id(1) == 0)
    def _():
        h_ref[...] = jnp.zeros_like(h_ref)
        f_ref[...] = jnp.zeros_like(f_ref)

    chain = {"h": h_ref[...], "ys": []}

    def tick():
        j = len(chain["ys"])
        if j < SCAN_CHUNKS:
            h = chain["h"]
            y_st = _dot(f_ref[j, 2].astype(BF16), h.astype(BF16)) + f_ref[j, 3]
            chain["h"] = _dot_x3(f_ref[j, 0], h) + f_ref[j, 1]
            chain["ys"].append(y_st[:CHUNK] + y_st[CHUNK:])

    n = 2 * CHUNK
    own = (lax.broadcasted_iota(jnp.int32, (n, LANES), 1) // HEAD_DIM
           == lax.broadcasted_iota(jnp.int32, (n, LANES), 0) // CHUNK)
    ti = lax.broadcasted_iota(jnp.int32, (n, n), 0)
    si = lax.broadcasted_iota(jnp.int32, (n, n), 1)
    same = (ti // CHUNK) == (si // CHUNK)
    strict = same & (si < ti)
    incl = same & (si <= ti)
    eye = ti == si

    factors = _chunk_factors(SCAN_CHUNKS, own, strict, incl, eye, at_ref, rt_ref, bt_ref, kt_ref, bh_ref, kh_ref,
                             v_ref, gam_ref, tick)
    while len(chain["ys"]) < SCAN_CHUNKS:
        tick()
    h_ref[...] = chain["h"]
    y = jnp.concatenate(chain["ys"], axis=0)

    bd = _head_sum_matrix()
    mean = _dot_sel(y, bd) * (1.0 / HEAD_DIM)
    yc = y - mean
    var = _dot_sel(yc * yc, bd) * (1.0 / HEAD_DIM)
    yn = yc * lax.rsqrt(var + RWKV_GN_EPS) * gnw_ref[...] + gnb_ref[...]
    y_ref[...] = ((yn + bv_ref[...]) * g_ref[...]).astype(BF16)

    for j, mats in enumerate(factors):
        for i, mat in enumerate(mats):
            f_ref[j, i] = mat


def _scan(at, rt, bt, kt, bh, kh, v, gam, g, bv, gn_w, gn_b):
    t = at.shape[0]
    rows = SCAN_CHUNKS * CHUNK
    steps = t // rows
    tile = pl.BlockSpec((rows, LANES), lambda p, s: (jnp.minimum(s, steps - 1), p))
    lagged = pl.BlockSpec((rows, LANES), lambda p, s: (jnp.maximum(s - 1, 0), p))
    vec = pl.BlockSpec((1, LANES), lambda p, s: (0, p))
    gam_spec = pl.BlockSpec((SCAN_CHUNKS, 1, LANES), lambda p, s: (jnp.minimum(s, steps - 1), 0, p))
    return pl.pallas_call(
        _scan_kernel,
        grid=(RWKV_WIDTH // LANES, steps + 1),
        in_specs=[tile] * 7 + [gam_spec, lagged, lagged, vec, vec],
        out_specs=lagged,
        out_shape=jax.ShapeDtypeStruct((t, RWKV_WIDTH), BF16),
        scratch_shapes=[pltpu.VMEM((LANES, LANES), F32), pltpu.VMEM((SCAN_CHUNKS, 4, 2 * CHUNK, LANES), F32)],
        compiler_params=_params("parallel", "arbitrary"),
        name="rwkv_scan",
    )(at, rt, bt, kt, bh, kh, v, gam, g, bv, gn_w, gn_b)


def _attn_kernel(sink_ref, qkv_ref, pos_ref, freq_ref, o_ref, kprev_ref, vprev_ref):
    nb = pl.program_id(0)

    @pl.when(nb == 0)
    def _():
        kprev_ref[...] = jnp.zeros_like(kprev_ref)
        vprev_ref[...] = jnp.zeros_like(vprev_ref)

    ang = pos_ref[...].astype(F32) * freq_ref[...]
    cos = jnp.cos(ang)
    lane = lax.broadcasted_iota(jnp.int32, ang.shape, 1)
    sin = jnp.where(lane % HEAD_DIM < HEAD_DIM // 2, -1.0, 1.0) * jnp.sin(ang)
    first_half = lane % HEAD_DIM < HEAD_DIM // 2

    def rope(x):
        swapped = jnp.where(first_half, pltpu.roll(x, LANES - HEAD_DIM // 2, 1), pltpu.roll(x, HEAD_DIM // 2, 1))
        return x * cos + swapped * sin

    scale = HEAD_DIM ** -0.5
    qi = lax.broadcasted_iota(jnp.int32, (WINDOW, 2 * WINDOW), 0) + WINDOW
    ki = lax.broadcasted_iota(jnp.int32, (WINDOW, 2 * WINDOW), 1)
    rel = qi - ki
    mask = (rel >= 0) & (rel < WINDOW) & ((nb > 0) | (ki >= WINDOW))

    k_cur = jnp.concatenate([rope(qkv_ref[:, ATTN_WIDTH + j * LANES:ATTN_WIDTH + (j + 1) * LANES])
                             for j in range(ATTN_KV_WIDTH // LANES)], axis=1)
    v_cur = qkv_ref[:, ATTN_WIDTH + ATTN_KV_WIDTH:]
    k_band = jnp.concatenate([kprev_ref[...], k_cur], axis=0)
    v_band = jnp.concatenate([vprev_ref[...], v_cur], axis=0)
    k_heads = [k_band[:, g * HEAD_DIM:(g + 1) * HEAD_DIM].astype(BF16) for g in range(ATTN_KV_HEADS)]
    v_heads = [v_band[:, g * HEAD_DIM:(g + 1) * HEAD_DIM].astype(BF16) for g in range(ATTN_KV_HEADS)]

    heads = range(ATTN_WIDTH // HEAD_DIM)
    per_group = LANES // HEAD_DIM
    q_groups = [rope(qkv_ref[:, j * LANES:(j + 1) * LANES]) * scale for j in range(ATTN_WIDTH // LANES)]
    q = [q_groups[h // per_group][:, (h % per_group) * HEAD_DIM:(h % per_group + 1) * HEAD_DIM].astype(BF16)
         for h in heads]
    s = [jnp.where(mask, _dot_nt(q[h], k_heads[h // ATTN_GROUP]), -jnp.inf) for h in heads]
    probs = []
    for h in heads:
        sink = sink_ref[h]
        m = jnp.maximum(jnp.max(s[h], axis=-1, keepdims=True), sink)
        pexp = jnp.exp(s[h] - m)
        denom = jnp.sum(pexp, axis=-1, keepdims=True) + jnp.exp(sink - m)
        probs.append((pexp / denom).astype(BF16))
    o = [_dot(probs[h], v_heads[h // ATTN_GROUP]) for h in heads]
    for j in range(ATTN_WIDTH // LANES):
        o_ref[:, j * LANES:(j + 1) * LANES] = jnp.concatenate(o[j * per_group:(j + 1) * per_group],
                                                               axis=1).astype(BF16)

    kprev_ref[...] = k_cur
    vprev_ref[...] = v_cur


def _attn(qkv, pos_col, freq, sinks):
    t = qkv.shape[0]
    return pl.pallas_call(
        _attn_kernel,
        grid_spec=pltpu.PrefetchScalarGridSpec(
            num_scalar_prefetch=1,
            grid=(t // WINDOW,),
            in_specs=[
                pl.BlockSpec((WINDOW, ATTN_COLS), lambda i, s: (i, 0)),
                pl.BlockSpec((WINDOW, 1), lambda i, s: (i, 0)),
                pl.BlockSpec((1, LANES), lambda i, s: (0, 0)),
            ],
            out_specs=pl.BlockSpec((WINDOW, ATTN_WIDTH), lambda i, s: (i, 0)),
            scratch_shapes=[pltpu.VMEM((WINDOW, ATTN_KV_WIDTH), F32), pltpu.VMEM((WINDOW, ATTN_KV_WIDTH), F32)],
        ),
        out_shape=jax.ShapeDtypeStruct((t, ATTN_WIDTH), BF16),
        compiler_params=_params("arbitrary"),
        name="swa_attn",
    )(sinks, qkv, pos_col, freq)


def _route_tile(lg, run):
    tm = lg.shape[0]
    lane = lax.broadcasted_iota(jnp.int32, lg.shape, 1)
    lane_f = lane.astype(F32)
    ninf = -jnp.inf

    def top(vals):
        best = jnp.max(vals, axis=-1, keepdims=True)
        idx = jnp.min(jnp.where(vals == best, lane_f, float(LANES)), axis=-1, keepdims=True)
        return best, idx

    gl = jnp.where(lane < N_GROUPS, lg, ninf)
    gmax, gidx = top(gl)
    pg_top = 1.0 / jnp.sum(jnp.exp(gl - gmax), axis=-1, keepdims=True)
    first = N_GROUPS + EXPERTS_PER_GROUP * gidx
    el = jnp.where((lane_f >= first) & (lane_f < first + EXPERTS_PER_GROUP), lg, ninf)
    v0, i0 = top(el)
    v1, i1 = top(jnp.where(lane_f == i0, ninf, el))
    ex = jnp.exp(v1 - v0)
    g0 = pg_top / (1.0 + ex)
    g1 = pg_top * ex / (1.0 + ex)
    e0 = i0 - N_GROUPS
    e1 = i1 - N_GROUPS

    hot0 = (lane_f == e0).astype(BF16)
    hot1 = (lane_f == e1).astype(BF16)
    rr = lax.broadcasted_iota(jnp.int32, (tm, tm), 0)
    cc = lax.broadcasted_iota(jnp.int32, (tm, tm), 1)
    before = (cc < rr).astype(BF16)
    cum = _dot(before, jnp.concatenate([hot0, hot1], axis=1))
    h0 = hot0.astype(F32)
    h1 = hot1.astype(F32)
    tot0 = jnp.sum(h0, axis=0, keepdims=True)
    tot1 = jnp.sum(h1, axis=0, keepdims=True)
    rank0 = jnp.sum(h0 * (run + cum[:, :LANES]), axis=-1, keepdims=True)
    rank1 = jnp.sum(h1 * (run + tot0 + cum[:, LANES:]), axis=-1, keepdims=True)
    ints = jnp.where(lane == 0, e0, jnp.where(lane == 1, e1, jnp.where(lane == 2, rank0, jnp.where(lane == 3, rank1, 0.0))))
    gates = jnp.where(lane == 0, g0, jnp.where(lane == 1, g1, 0.0))
    return ints.astype(jnp.int32), gates, run + tot0 + tot1


def _out_proj_kernel(x_ref, ya_ref, yb_ref, wo_ref, g_ref, wr_ref, br_ref,
                     x1_ref, h2_ref, ri_ref, rg_ref, cnt_ref, run_ref):
    @pl.when(pl.program_id(0) == 0)
    def _():
        run_ref[...] = jnp.zeros_like(run_ref)

    x1 = x_ref[...] + _dot(ya_ref[...], wo_ref[:RWKV_WIDTH, :]) + _dot(yb_ref[...], wo_ref[RWKV_WIDTH:, :])
    x1_ref[...] = x1
    h2 = _rms(x1, g_ref[...])
    h2_ref[...] = h2
    ints, gates, run = _route_tile(_dot_x3(h2, wr_ref[...]) + br_ref[...], run_ref[...])
    ri_ref[...] = ints
    rg_ref[...] = gates
    run_ref[...] = run
    cnt_ref[...] = run.astype(jnp.int32)


def _out_proj(x, y_rwkv, y_attn, wo, ln2, wr, br, tm):
    t, d = x.shape
    row = lambda i: (i, 0)
    fixed = lambda i: (0, 0)
    return pl.pallas_call(
        _out_proj_kernel,
        grid=(t // tm,),
        in_specs=[
            pl.BlockSpec((tm, d), row),
            pl.BlockSpec((tm, RWKV_WIDTH), row),
            pl.BlockSpec((tm, ATTN_WIDTH), row),
            pl.BlockSpec((d, d), fixed),
            pl.BlockSpec((1, d), fixed),
            pl.BlockSpec((d, LANES), fixed),
            pl.BlockSpec((1, LANES), fixed),
        ],
        out_specs=[pl.BlockSpec((tm, d), row), pl.BlockSpec((tm, d), row), pl.BlockSpec((tm, LANES), row),
                   pl.BlockSpec((tm, LANES), row), pl.BlockSpec((1, LANES), fixed)],
        out_shape=[jax.ShapeDtypeStruct((t, d), F32), jax.ShapeDtypeStruct((t, d), F32),
                   jax.ShapeDtypeStruct((t, LANES), jnp.int32), jax.ShapeDtypeStruct((t, LANES), F32),
                   jax.ShapeDtypeStruct((1, LANES), jnp.int32)],
        scratch_shapes=[pltpu.VMEM((1, LANES), F32)],
        compiler_params=_params("arbitrary"),
        name="out_proj_router",
    )(x, y_rwkv, y_attn, wo, ln2, wr, br)


SUBLANES = 8


def _for_rows(n, fn):
    full = lax.shift_right_logical(n, 3)

    def group(i, carry):
        for u in range(SUBLANES):
            fn(i, u)
        return carry

    def single(r, carry):
        fn(full, r)
        return carry

    lax.fori_loop(0, full, group, 0)
    lax.fori_loop(0, n - full * SUBLANES, single, 0)


def _moe_kernel(exp_ref, nvalid_ref, start_ref, eid_ref, rank_ref, h_hbm, wgu_ref, wdn_ref, y_hbm,
                xbuf, obuf, slot_ref, gsem, ssem):
    b = pl.program_id(0)
    last = pl.num_programs(0) - 1
    cur = lax.rem(b, 2)
    nv = nvalid_ref[b]
    n_tok = h_hbm.shape[0]

    def gather_copy(blk, buf, i, u):
        a = slot_ref[blk * MOE_BLOCK + i * SUBLANES + u]
        tok = lax.shift_right_logical(a, 1)
        return pltpu.make_async_copy(h_hbm.at[pl.ds(tok, 1), :], xbuf.at[buf, i, pl.ds(u, 1), :], gsem.at[buf])

    def scatter_copy(blk, buf, i, u):
        a = slot_ref[blk * MOE_BLOCK + i * SUBLANES + u]
        dst = (a & 1) * n_tok + lax.shift_right_logical(a, 1)
        return pltpu.make_async_copy(obuf.at[buf, i, pl.ds(u, 1), :], y_hbm.at[pl.ds(dst, 1), :], ssem.at[buf])

    def gather_start(blk, buf):
        _for_rows(nvalid_ref[blk], lambda i, u: gather_copy(blk, buf, i, u).start())

    def gather_wait(blk, buf):
        _for_rows(nvalid_ref[blk], lambda i, u: gather_copy(blk, buf, i, u).wait())

    def scatter_start(blk, buf):
        _for_rows(nvalid_ref[blk], lambda i, u: scatter_copy(blk, buf, i, u).start())

    def scatter_wait(blk, buf):
        _for_rows(nvalid_ref[blk], lambda i, u: scatter_copy(blk, buf, i, u).wait())

    @pl.when(b == 0)
    def _():
        xbuf[...] = jnp.zeros_like(xbuf)

        def place(i, carry):
            for u in range(SUBLANES):
                a = i * SUBLANES + u
                slot_ref[start_ref[eid_ref[a]] + rank_ref[a]] = a
            return carry

        lax.fori_loop(0, eid_ref.shape[0] // SUBLANES, place, 0)
        gather_start(0, 0)

    @pl.when(b < last)
    def _():
        gather_start(b + 1, 1 - cur)

    @pl.when(b >= 2)
    def _():
        scatter_wait(b - 2, cur)

    @pl.when(nv > 0)
    def _():
        gather_wait(b, cur)
        x = xbuf[cur].reshape(MOE_BLOCK, D_MODEL).astype(BF16)
        out = obuf.at[cur]
        step = 256
        for j in range(D_EXPERT // step):
            wg = wgu_ref[0, :, j * step:(j + 1) * step].astype(BF16)
            wu = wgu_ref[0, :, D_EXPERT + j * step:D_EXPERT + (j + 1) * step].astype(BF16)
            hg = _dot(x, wg)
            hu = _dot(x, wu)
            act = (hg * jax.nn.sigmoid(hg) * hu).astype(BF16)
            part = _dot(act, wdn_ref[0, j * step:(j + 1) * step, :].astype(BF16))
            part = part.reshape(MOE_BLOCK // SUBLANES, SUBLANES, D_MODEL)
            if j == 0:
                out[...] = part
            else:
                out[...] += part
        scatter_start(b, cur)

    @pl.when(b == last)
    def _():
        @pl.when(b >= 1)
        def _():
            scatter_wait(b - 1, 1 - cur)

        scatter_wait(b, cur)


def _moe(h2, w_gu, w_dn, blk_exp, nvalid, pad_start, eid, rank):
    t, d = h2.shape
    n_blocks = blk_exp.shape[0]
    imap = lambda b, e, *_: (e[b], 0, 0)
    return pl.pallas_call(
        _moe_kernel,
        grid_spec=pltpu.PrefetchScalarGridSpec(
            num_scalar_prefetch=5,
            grid=(n_blocks,),
            in_specs=[
                pl.BlockSpec(memory_space=pl.ANY),
                pl.BlockSpec((1, d, 2 * D_EXPERT), imap),
                pl.BlockSpec((1, D_EXPERT, d), imap),
            ],
            out_specs=pl.BlockSpec(memory_space=pl.ANY),
            scratch_shapes=[
                pltpu.VMEM((2, MOE_BLOCK // SUBLANES, SUBLANES, d), F32),
                pltpu.VMEM((2, MOE_BLOCK // SUBLANES, SUBLANES, d), F32),
                pltpu.SMEM((n_blocks * MOE_BLOCK,), jnp.int32),
                pltpu.SemaphoreType.DMA((2,)),
                pltpu.SemaphoreType.DMA((2,)),
            ],
        ),
        out_shape=jax.ShapeDtypeStruct((TOP_K_INNER * t, d), F32),
        compiler_params=_params("arbitrary"),
        name="moe_experts",
    )(blk_exp, nvalid, pad_start, eid, rank, h2, w_gu, w_dn)


def _block_tables(counts, m_assign):
    padded = ((counts + MOE_BLOCK - 1) // MOE_BLOCK) * MOE_BLOCK
    pad_end = jnp.cumsum(padded)
    pad_start = pad_end - padded
    n_blocks = -(-m_assign // MOE_BLOCK) + N_EXPERTS
    blk_start = jnp.arange(n_blocks, dtype=jnp.int32) * MOE_BLOCK
    blk_exp = jnp.minimum(jnp.sum(pad_end[None, :] <= blk_start[:, None], axis=1), N_EXPERTS - 1).astype(jnp.int32)
    nvalid = jnp.clip(counts[blk_exp] - (blk_start - pad_start[blk_exp]), 0, MOE_BLOCK)
    nvalid = jnp.where(blk_start < pad_end[-1], nvalid, 0).astype(jnp.int32)
    return blk_exp, nvalid, pad_start.astype(jnp.int32)


def _final_kernel(x_ref, ya_ref, yb_ref, rg_ref, g_ref, o_ref):
    x = x_ref[...] + rg_ref[:, 0:1] * ya_ref[...] + rg_ref[:, 1:2] * yb_ref[...]
    o_ref[...] = _rms(x, g_ref[...])


def _final(x1, y2, gates, gain, tm):
    t, d = x1.shape
    nt = t // tm
    return pl.pallas_call(
        _final_kernel,
        grid=(nt,),
        in_specs=[
            pl.BlockSpec((tm, d), lambda i: (i, 0)),
            pl.BlockSpec((tm, d), lambda i: (i, 0)),
            pl.BlockSpec((tm, d), lambda i: (i + nt, 0)),
            pl.BlockSpec((tm, LANES), lambda i: (i, 0)),
            pl.BlockSpec((1, d), lambda i: (0, 0)),
        ],
        out_specs=pl.BlockSpec((tm, d), lambda i: (i, 0)),
        out_shape=jax.ShapeDtypeStruct((t, d), F32),
        compiler_params=_params("parallel"),
        name="final_norm",
    )(x1, y2, y2, gates, gain)


def _pad_cols(a, n):
    return jnp.pad(a, ((0, 0), (0, n - a.shape[1])))


def _rwkv_mix(p_rwkv, mu, w0, wdu, a0, wau, wgu, k_k, k_a, r_k, gn_w, gn_b):
    row = lambda a: a.reshape(1, -1)
    outs = _prep(p_rwkv, _pad_cols(row(mu), RWKV_COLS_PAD), row(w0), wdu, row(a0), wau,
                 jnp.pad(wgu, ((0, GATE_LORA_PAD - GATE_LORA), (0, 0))), row(k_k), row(k_a), row(r_k), tm=256)
    return _scan(*outs, row(gn_w), row(gn_b))


def _mix_and_route(x, y_rwkv, y_attn, w_out, ln2, w_rg, b_rg, w_re, b_re):
    row = lambda a: a.reshape(1, -1)
    wr = _pad_cols(jnp.concatenate([w_rg, w_re], axis=1), LANES)
    br = _pad_cols(row(jnp.concatenate([b_rg, b_re])), LANES)
    return _out_proj(x, y_rwkv, y_attn, w_out.astype(BF16), row(ln2), wr, br, tm=256)


def _experts(h2, route_ints, counts, w_gu, w_dn):
    t = h2.shape[0]
    blk_exp, nvalid, pad_start = _block_tables(counts[0, :N_EXPERTS], t * TOP_K_INNER)
    eid = route_ints[:, 0:TOP_K_INNER].reshape(-1)
    rank = route_ints[:, TOP_K_INNER:2 * TOP_K_INNER].reshape(-1)
    return _moe(h2, w_gu, w_dn, blk_exp, nvalid, pad_start, eid, rank)


def _layer(x, positions, ln1, w_in, mu, w0, wdu, a0, wau, wgu, k_k, k_a, r_k, gn_w, gn_b, sinks, w_out, ln2,
           w_rg, b_rg, w_re, b_re, w_gu, w_dn, ln_f):
    t = x.shape[0]
    row = lambda a: a.reshape(1, -1)
    w_rwkv = _pad_cols(w_in[:, :RWKV_COLS], RWKV_COLS_PAD).astype(BF16)
    w_attn = w_in[:, RWKV_COLS:].astype(BF16)
    p_rwkv = _in_proj(x, row(ln1), w_rwkv, tm=min(512, t), tn=RWKV_COLS_PAD // 3)
    qkv = _in_proj(x, row(ln1), w_attn, tm=min(512, t), tn=ATTN_COLS)

    y_rwkv = _rwkv_mix(p_rwkv, mu, w0, wdu, a0, wau, wgu, k_k, k_a, r_k, gn_w, gn_b)

    inv_freq = ROPE_THETA ** (-jnp.arange(0, HEAD_DIM, 2, dtype=F32) / HEAD_DIM)
    freq = jnp.tile(inv_freq, 2 * LANES // HEAD_DIM).reshape(1, LANES)
    y_attn = _attn(qkv, positions.reshape(t, 1), freq, sinks)

    x1, h2, route_ints, gates, counts = _mix_and_route(x, y_rwkv, y_attn, w_out, ln2, w_rg, b_rg, w_re, b_re)
    y2 = _experts(h2, route_ints, counts, w_gu, w_dn)
    return _final(x1, y2, gates, row(ln_f), tm=256)


def kernel(x, positions, ln1, w_in, mu_shift, w_decay0, w_decay_up, a0, w_a_up, w_g_up, k_k, k_a, r_k, gn_w, gn_b,
           sinks, w_out, ln2, w_router_group, b_router_group, w_router_expert, b_router_expert, w_expert_gu,
           w_expert_down, ln_f):
    assert ln1.shape[0] == 1, "one trunk layer"
    outs = [
        _layer(x[i], positions[i], ln1[0], w_in[0], mu_shift[0], w_decay0[0], w_decay_up[0], a0[0], w_a_up[0],
               w_g_up[0], k_k[0], k_a[0], r_k[0], gn_w[0], gn_b[0], sinks[0], w_out[0], ln2[0], w_router_group[0],
               b_router_group[0], w_router_expert[0], b_router_expert[0], w_expert_gu[0], w_expert_down[0], ln_f)
        for i in range(x.shape[0])
    ]
    return jnp.stack(outs, axis=0)
```

```python
import math

import jax
import jax.numpy as jnp
from jax import lax
from jax.experimental import pallas as pl
from jax.experimental.pallas import tpu as pltpu

D_MODEL = 2048
HEAD_DIM = 64
RWKV_WIDTH = 1024
ATTN_WIDTH = 1024
ATTN_KV_HEADS = 4
ATTN_GROUP = 4
ATTN_KV_WIDTH = 256
WINDOW = 128
ROPE_THETA = 10000.0
DECAY_LORA = 64
AAA_LORA = 64
GATE_LORA = 160
GATE_LORA_PAD = 256
RWKV_COLS = 3 * RWKV_WIDTH + DECAY_LORA + AAA_LORA + GATE_LORA
RWKV_COLS_PAD = 3 * RWKV_WIDTH + DECAY_LORA + AAA_LORA + GATE_LORA_PAD
ATTN_COLS = ATTN_WIDTH + 2 * ATTN_KV_WIDTH
N_GROUPS = 8
EXPERTS_PER_GROUP = 8
N_EXPERTS = 64
TOP_K_INNER = 2
D_EXPERT = 768
MOE_BLOCK = 256
RMS_EPS = 1e-6
RWKV_GN_EPS = 64e-5

LANES = 128
SUBLANES = 8
CHUNK = 64
SCAN_CHUNKS = 8
VMEM_LIMIT = 56 * 1024 * 1024

F32 = jnp.float32
BF16 = jnp.bfloat16


def _dot(a, b):
    return jnp.dot(a, b, preferred_element_type=F32)


def _split(x, parts):
    out = []
    for _ in range(parts - 1):
        hi = x.astype(BF16)
        out.append(hi)
        x = x - hi.astype(F32)
    out.append(x.astype(BF16))
    return out


def _dot_x3(a, b):
    ah, al = _split(a, 2)
    bh, bl = _split(b, 2)
    return _dot(ah, bh) + (_dot(ah, bl) + _dot(al, bh))


def _dot_sel(a, sel):
    ah, al = _split(a, 2)
    return _dot(ah, sel) + _dot(al, sel)


def _dot_nt(a, b):
    return lax.dot_general(a, b, (((1,), (1,)), ((), ())), preferred_element_type=F32)


def _dot_tn(a, b):
    return lax.dot_general(a, b, (((0,), (0,)), ((), ())), preferred_element_type=F32)


def _params(*sem):
    return pltpu.CompilerParams(dimension_semantics=sem, vmem_limit_bytes=VMEM_LIMIT)


def _head_sum_matrix():
    r = lax.broadcasted_iota(jnp.int32, (LANES, LANES), 0) // HEAD_DIM
    c = lax.broadcasted_iota(jnp.int32, (LANES, LANES), 1) // HEAD_DIM
    return (r == c).astype(BF16)


def _head_sums(x, bd):
    parts = [_dot_sel(x[:, j * LANES:(j + 1) * LANES], bd) for j in range(x.shape[1] // LANES)]
    return jnp.concatenate(parts, axis=1)


def _rms(x, gain):
    return x * lax.rsqrt(jnp.mean(x * x, axis=-1, keepdims=True) + RMS_EPS) * gain


def _in_proj_kernel(x_ref, g_ref, w_ref, o_ref, h_ref):
    @pl.when(pl.program_id(1) == 0)
    def _():
        h_ref[...] = _rms(x_ref[...], g_ref[...]).astype(BF16)

    o_ref[...] = _dot(h_ref[...], w_ref[...])


def _in_proj(x, gain, w, tm, tn):
    t, d = x.shape
    n = w.shape[1]
    return pl.pallas_call(
        _in_proj_kernel,
        grid=(t // tm, n // tn),
        in_specs=[
            pl.BlockSpec((tm, d), lambda i, j: (i, 0)),
            pl.BlockSpec((1, d), lambda i, j: (0, 0)),
            pl.BlockSpec((d, tn), lambda i, j: (0, j)),
        ],
        out_specs=pl.BlockSpec((tm, tn), lambda i, j: (i, j)),
        out_shape=jax.ShapeDtypeStruct((t, n), F32),
        scratch_shapes=[pltpu.VMEM((tm, d), BF16)],
        compiler_params=_params("parallel", "arbitrary"),
        name="in_proj",
    )(x, gain, w)


def _prep_kernel(p_ref, prev_ref, mu_ref, w0_ref, wdu_ref, a0_ref, wau_ref, wgu_ref, kk_ref, ka_ref, rk_ref,
                 at_ref, rt_ref, bt_ref, kt_ref, bh_ref, kh_ref, v_ref, gam_ref, g_ref, bv_ref):
    i = pl.program_id(0)
    tm = p_ref.shape[0]
    w = RWKV_WIDTH
    p = p_ref[...]
    prev_row = jnp.where(i == 0, 0.0, prev_ref[7:8, :])
    row = lax.broadcasted_iota(jnp.int32, p.shape, 0)
    shifted = jnp.where(row == 0, prev_row, pltpu.roll(p, 1, 0))
    ps = p + (shifted - p) * mu_ref[...]

    r = ps[:, 0:w]
    k = ps[:, w:2 * w]
    v = ps[:, 2 * w:3 * w]
    wd = ps[:, 3 * w:3 * w + DECAY_LORA]
    ad = ps[:, 3 * w + DECAY_LORA:3 * w + DECAY_LORA + AAA_LORA]
    gd = ps[:, 3 * w + DECAY_LORA + AAA_LORA:]

    z = w0_ref[...] + _dot_x3(jnp.tanh(wd), wdu_ref[...])
    logw = -math.exp(-0.5) * jax.nn.sigmoid(z)
    alr = jax.nn.sigmoid(a0_ref[...] + _dot_x3(ad, wau_ref[...]))
    g_ref[...] = _dot_x3(jax.nn.sigmoid(gd), wgu_ref[...])

    bd = _head_sum_matrix()
    kk = k * kk_ref[...]
    kk = kk / jnp.maximum(jnp.sqrt(_head_sums(kk * kk, bd)), 1e-12)
    kmod = k * (1.0 + (alr - 1.0) * ka_ref[...])
    b = kk * alr
    bv_ref[...] = _head_sums(r * kmod * rk_ref[...], bd) * v
    v_ref[...] = v.astype(BF16)

    rr = lax.broadcasted_iota(jnp.int32, (tm, tm), 0)
    cc = lax.broadcasted_iota(jnp.int32, (tm, tm), 1)
    tri = ((cc <= rr) & (cc // CHUNK == rr // CHUNK)).astype(BF16)
    l1, l2, l3 = _split(logw, 3)
    cum = _dot(tri, l1) + (_dot(tri, l2) + _dot(tri, l3))
    tot_rows = []
    for c in range(tm // CHUNK):
        last = cum[c * CHUNK + CHUNK - 1:c * CHUNK + CHUNK, :]
        gam_ref[c] = jnp.exp(last)
        tot_rows.append(jnp.broadcast_to(last, (CHUNK, w)))
    tot = jnp.concatenate(tot_rows, axis=0)

    e_neg = jnp.exp(-cum)
    e_rem = jnp.exp(tot - cum)
    at_ref[...] = (-kk * jnp.exp(cum - logw)).astype(BF16)
    rt_ref[...] = (r * jnp.exp(cum)).astype(BF16)
    bt_ref[...] = (b * e_neg).astype(BF16)
    kt_ref[...] = (kmod * e_neg).astype(BF16)
    bh_ref[...] = (b * e_rem).astype(BF16)
    kh_ref[...] = (kmod * e_rem).astype(BF16)


def _prep(p_rwkv, mu, w0, wdu, a0, wau, wgu, k_k, k_a, r_k, tm):
    t = p_rwkv.shape[0]
    w = RWKV_WIDTH
    cp = RWKV_COLS_PAD
    nc = tm // CHUNK
    row = lambda i: (i, 0)
    fixed = lambda i: (0, 0)
    vec = pl.BlockSpec((1, w), fixed)
    big_bf = jax.ShapeDtypeStruct((t, w), BF16)
    big_f32 = jax.ShapeDtypeStruct((t, w), F32)
    out_tile = pl.BlockSpec((tm, w), row)
    return pl.pallas_call(
        _prep_kernel,
        grid=(t // tm,),
        in_specs=[
            pl.BlockSpec((tm, cp), row),
            pl.BlockSpec((8, cp), lambda i: (jnp.maximum(i * (tm // 8) - 1, 0), 0)),
            pl.BlockSpec((1, cp), fixed),
            vec, pl.BlockSpec((DECAY_LORA, w), fixed),
            vec, pl.BlockSpec((AAA_LORA, w), fixed),
            pl.BlockSpec((GATE_LORA_PAD, w), fixed),
            vec, vec, vec,
        ],
        out_specs=[out_tile] * 7 + [pl.BlockSpec((nc, 1, w), lambda i: (i, 0, 0)), out_tile, out_tile],
        out_shape=[big_bf] * 7 + [jax.ShapeDtypeStruct((t // CHUNK, 1, w), F32), big_f32, big_f32],
        compiler_params=_params("parallel"),
        name="rwkv_prep",
    )(p_rwkv, p_rwkv, mu, w0, wdu, a0, wau, wgu, k_k, k_a, r_k)


def _chunk_factors(nchunks, own, strict, incl, eye, at_ref, rt_ref, bt_ref, kt_ref, bh_ref, kh_ref, v_ref, gam_ref,
                   tick):
    js = range(nchunks)
    n = 2 * CHUNK

    def stack(ref):
        xs = [ref[pl.ds(j * CHUNK, CHUNK), :] for j in js]
        return [jnp.where(own, jnp.concatenate([x, x], axis=0), jnp.zeros((), x.dtype)) for x in xs]

    a_s, r_s, b_s, k_s = stack(at_ref), stack(rt_ref), stack(bt_ref), stack(kt_ref)
    bh_s, kh_s, v_s = stack(bh_ref), stack(kh_ref), stack(v_ref)

    prod = [_dot_nt(jnp.concatenate([a_s[j], r_s[j]], axis=0), jnp.concatenate([b_s[j], k_s[j]], axis=0)) for j in js]
    a_ak = [jnp.where(strict, prod[j][:n, n:], 0.0).astype(BF16) for j in js]
    a_rb = [jnp.where(incl, prod[j][n:, :n], 0.0).astype(BF16) for j in js]
    a_rk = [jnp.where(incl, prod[j][n:, n:], 0.0).astype(BF16) for j in js]

    tick()
    pq = [jnp.concatenate([a_s[j].astype(F32), _dot(a_ak[j], v_s[j])], axis=1) for j in js]
    nk_b = [jnp.where(strict, prod[j][:n, :n], 0.0).astype(BF16) for j in js]
    span = 1
    while True:
        tick()
        pq = [pq[j] + _dot(nk_b[j], pq[j].astype(BF16)) for j in js]
        span *= 2
        if span >= CHUNK:
            break
        tick()
        nk_b = [_dot(nk_b[j], nk_b[j]).astype(BF16) for j in js]
    pq_b = [pq[j].astype(BF16) for j in js]

    mn = [_dot_tn(bh_s[j], pq_b[j]) for j in js]
    kv = [_dot_tn(kh_s[j], v_s[j]) for j in js]
    yy = [_dot(a_rb[j], pq_b[j]) for j in js]
    yv = [_dot(a_rk[j], v_s[j]) for j in js]
    out = []
    for j in js:
        m_mat = mn[j][:, :LANES] + jnp.where(eye, jnp.broadcast_to(gam_ref[j], (LANES, LANES)), 0.0)
        n_mat = mn[j][:, LANES:] + kv[j]
        y_c = r_s[j].astype(F32) + yy[j][:, :LANES]
        y_n = yy[j][:, LANES:] + yv[j]
        out.append((m_mat, n_mat, y_c, y_n))
    return out


def _scan_kernel(at_ref, rt_ref, bt_ref, kt_ref, bh_ref, kh_ref, v_ref, gam_ref, g_ref, bv_ref, gnw_ref, gnb_ref,
                 y_ref, h_ref, f_ref):
    @pl.when(pl.program_id(1) == 0)
    def _():
        h_ref[...] = jnp.zeros_like(h_ref)
        f_ref[...] = jnp.zeros_like(f_ref)

    chain = {"h": h_ref[...], "ys": []}

    def tick():
        j = len(chain["ys"])
        if j < SCAN_CHUNKS:
            h = chain["h"]
            y_st = _dot(f_ref[j, 2].astype(BF16), h.astype(BF16)) + f_ref[j, 3]
            chain["h"] = _dot_x3(f_ref[j, 0], h) + f_ref[j, 1]
            chain["ys"].append(y_st[:CHUNK] + y_st[CHUNK:])

    n = 2 * CHUNK
    own = (lax.broadcasted_iota(jnp.int32, (n, LANES), 1) // HEAD_DIM
           == lax.broadcasted_iota(jnp.int32, (n, LANES), 0) // CHUNK)
    ti = lax.broadcasted_iota(jnp.int32, (n, n), 0)
    si = lax.broadcasted_iota(jnp.int32, (n, n), 1)
    same = (ti // CHUNK) == (si // CHUNK)
    strict = same & (si < ti)
    incl = same & (si <= ti)
    eye = ti == si

    factors = _chunk_factors(SCAN_CHUNKS, own, strict, incl, eye, at_ref, rt_ref, bt_ref, kt_ref, bh_ref, kh_ref,
                             v_ref, gam_ref, tick)
    while len(chain["ys"]) < SCAN_CHUNKS:
        tick()
    h_ref[...] = chain["h"]
    y = jnp.concatenate(chain["ys"], axis=0)

    low = lax.broadcasted_iota(jnp.int32, y.shape, 1) < HEAD_DIM

    def head_mean(x):
        s_low = jnp.sum(jnp.where(low, x, 0.0), axis=-1, keepdims=True)
        s_all = jnp.sum(x, axis=-1, keepdims=True)
        return jnp.where(low, s_low, s_all - s_low) * (1.0 / HEAD_DIM)

    yc = y - head_mean(y)
    var = head_mean(yc * yc)
    yn = yc * lax.rsqrt(var + RWKV_GN_EPS) * gnw_ref[...] + gnb_ref[...]
    y_ref[...] = ((yn + bv_ref[...]) * g_ref[...]).astype(BF16)

    for j, mats in enumerate(factors):
        for i, mat in enumerate(mats):
            f_ref[j, i] = mat


def _scan(at, rt, bt, kt, bh, kh, v, gam, g, bv, gn_w, gn_b):
    t = at.shape[0]
    rows = SCAN_CHUNKS * CHUNK
    steps = t // rows
    tile = pl.BlockSpec((rows, LANES), lambda p, s: (jnp.minimum(s, steps - 1), p))
    lagged = pl.BlockSpec((rows, LANES), lambda p, s: (jnp.maximum(s - 1, 0), p))
    vec = pl.BlockSpec((1, LANES), lambda p, s: (0, p))
    gam_spec = pl.BlockSpec((SCAN_CHUNKS, 1, LANES), lambda p, s: (jnp.minimum(s, steps - 1), 0, p))
    return pl.pallas_call(
        _scan_kernel,
        grid=(RWKV_WIDTH // LANES, steps + 1),
        in_specs=[tile] * 7 + [gam_spec, lagged, lagged, vec, vec],
        out_specs=lagged,
        out_shape=jax.ShapeDtypeStruct((t, RWKV_WIDTH), BF16),
        scratch_shapes=[pltpu.VMEM((LANES, LANES), F32), pltpu.VMEM((SCAN_CHUNKS, 4, 2 * CHUNK, LANES), F32)],
        compiler_params=_params("parallel", "arbitrary"),
        name="rwkv_scan",
    )(at, rt, bt, kt, bh, kh, v, gam, g, bv, gn_w, gn_b)


def _attn_kernel(sink_ref, qkv_ref, pos_ref, freq_ref, o_ref, kprev_ref, vprev_ref):
    nb = pl.program_id(0)

    @pl.when(nb == 0)
    def _():
        kprev_ref[...] = jnp.zeros_like(kprev_ref)
        vprev_ref[...] = jnp.zeros_like(vprev_ref)

    ang = pos_ref[...].astype(F32) * freq_ref[...]
    cos = jnp.cos(ang)
    lane = lax.broadcasted_iota(jnp.int32, ang.shape, 1)
    sin = jnp.where(lane % HEAD_DIM < HEAD_DIM // 2, -1.0, 1.0) * jnp.sin(ang)
    first_half = lane % HEAD_DIM < HEAD_DIM // 2

    def rope(x):
        swapped = jnp.where(first_half, pltpu.roll(x, LANES - HEAD_DIM // 2, 1), pltpu.roll(x, HEAD_DIM // 2, 1))
        return x * cos + swapped * sin

    scale = HEAD_DIM ** -0.5
    qi = lax.broadcasted_iota(jnp.int32, (WINDOW, 2 * WINDOW), 0) + WINDOW
    ki = lax.broadcasted_iota(jnp.int32, (WINDOW, 2 * WINDOW), 1)
    rel = qi - ki
    mask = (rel >= 0) & (rel < WINDOW) & ((nb > 0) | (ki >= WINDOW))

    k_cur = jnp.concatenate([rope(qkv_ref[:, ATTN_WIDTH + j * LANES:ATTN_WIDTH + (j + 1) * LANES])
                             for j in range(ATTN_KV_WIDTH // LANES)], axis=1)
    v_cur = qkv_ref[:, ATTN_WIDTH + ATTN_KV_WIDTH:]
    k_band = jnp.concatenate([kprev_ref[...], k_cur], axis=0)
    v_band = jnp.concatenate([vprev_ref[...], v_cur], axis=0)
    k_heads = [k_band[:, g * HEAD_DIM:(g + 1) * HEAD_DIM].astype(BF16) for g in range(ATTN_KV_HEADS)]
    v_heads = [v_band[:, g * HEAD_DIM:(g + 1) * HEAD_DIM].astype(BF16) for g in range(ATTN_KV_HEADS)]

    heads = range(ATTN_WIDTH // HEAD_DIM)
    per_group = LANES // HEAD_DIM
    q_groups = [rope(qkv_ref[:, j * LANES:(j + 1) * LANES]) * scale for j in range(ATTN_WIDTH // LANES)]
    q = [q_groups[h // per_group][:, (h % per_group) * HEAD_DIM:(h % per_group + 1) * HEAD_DIM].astype(BF16)
         for h in heads]
    s = [jnp.where(mask, _dot_nt(q[h], k_heads[h // ATTN_GROUP]), -jnp.inf) for h in heads]
    probs = []
    for h in heads:
        sink = sink_ref[h]
        m = jnp.maximum(jnp.max(s[h], axis=-1, keepdims=True), sink)
        pexp = jnp.exp(s[h] - m)
        denom = jnp.sum(pexp, axis=-1, keepdims=True) + jnp.exp(sink - m)
        probs.append((pexp / denom).astype(BF16))
    o = [_dot(probs[h], v_heads[h // ATTN_GROUP]) for h in heads]
    for j in range(ATTN_WIDTH // LANES):
        o_ref[:, j * LANES:(j + 1) * LANES] = jnp.concatenate(o[j * per_group:(j + 1) * per_group],
                                                               axis=1).astype(BF16)

    kprev_ref[...] = k_cur
    vprev_ref[...] = v_cur


def _attn(qkv, pos_col, freq, sinks):
    t = qkv.shape[0]
    return pl.pallas_call(
        _attn_kernel,
        grid_spec=pltpu.PrefetchScalarGridSpec(
            num_scalar_prefetch=1,
            grid=(t // WINDOW,),
            in_specs=[
                pl.BlockSpec((WINDOW, ATTN_COLS), lambda i, s: (i, 0)),
                pl.BlockSpec((WINDOW, 1), lambda i, s: (i, 0)),
                pl.BlockSpec((1, LANES), lambda i, s: (0, 0)),
            ],
            out_specs=pl.BlockSpec((WINDOW, ATTN_WIDTH), lambda i, s: (i, 0)),
            scratch_shapes=[pltpu.VMEM((WINDOW, ATTN_KV_WIDTH), F32), pltpu.VMEM((WINDOW, ATTN_KV_WIDTH), F32)],
        ),
        out_shape=jax.ShapeDtypeStruct((t, ATTN_WIDTH), BF16),
        compiler_params=_params("arbitrary"),
        name="swa_attn",
    )(sinks, qkv, pos_col, freq)


def _route_tile(lg, run):
    tm = lg.shape[0]
    lane = lax.broadcasted_iota(jnp.int32, lg.shape, 1)
    lane_f = lane.astype(F32)
    ninf = -jnp.inf

    def top(vals):
        best = jnp.max(vals, axis=-1, keepdims=True)
        idx = jnp.min(jnp.where(vals == best, lane_f, float(LANES)), axis=-1, keepdims=True)
        return best, idx

    gl = jnp.where(lane < N_GROUPS, lg, ninf)
    gmax, gidx = top(gl)
    pg_top = 1.0 / jnp.sum(jnp.exp(gl - gmax), axis=-1, keepdims=True)
    first = N_GROUPS + EXPERTS_PER_GROUP * gidx
    el = jnp.where((lane_f >= first) & (lane_f < first + EXPERTS_PER_GROUP), lg, ninf)
    v0, i0 = top(el)
    v1, i1 = top(jnp.where(lane_f == i0, ninf, el))
    ex = jnp.exp(v1 - v0)
    g0 = pg_top / (1.0 + ex)
    g1 = pg_top * ex / (1.0 + ex)
    e0 = i0 - N_GROUPS
    e1 = i1 - N_GROUPS

    hot0 = (lane_f == e0).astype(BF16)
    hot1 = (lane_f == e1).astype(BF16)
    rr = lax.broadcasted_iota(jnp.int32, (tm, tm), 0)
    cc = lax.broadcasted_iota(jnp.int32, (tm, tm), 1)
    before = (cc < rr).astype(BF16)
    cum = _dot(before, jnp.concatenate([hot0, hot1], axis=1))
    h0 = hot0.astype(F32)
    h1 = hot1.astype(F32)
    tot0 = jnp.sum(h0, axis=0, keepdims=True)
    tot1 = jnp.sum(h1, axis=0, keepdims=True)
    rank0 = jnp.sum(h0 * (run + cum[:, :LANES]), axis=-1, keepdims=True)
    rank1 = jnp.sum(h1 * (run + tot0 + cum[:, LANES:]), axis=-1, keepdims=True)
    ints = jnp.where(lane == 0, e0, jnp.where(lane == 1, e1, jnp.where(lane == 2, rank0, jnp.where(lane == 3, rank1, 0.0))))
    gates = jnp.where(lane == 0, g0, jnp.where(lane == 1, g1, 0.0))
    return ints.astype(jnp.int32), gates, run + tot0 + tot1


def _out_proj_kernel(x_ref, ya_ref, yb_ref, wo_ref, g_ref, wr_ref, br_ref,
                     x1_ref, h2_ref, ri_ref, rg_ref, cnt_ref, run_ref):
    @pl.when(pl.program_id(0) == 0)
    def _():
        run_ref[...] = jnp.zeros_like(run_ref)

    x1 = x_ref[...] + _dot(ya_ref[...], wo_ref[:RWKV_WIDTH, :]) + _dot(yb_ref[...], wo_ref[RWKV_WIDTH:, :])
    x1_ref[...] = x1
    h2 = _rms(x1, g_ref[...])
    h2_ref[...] = h2
    ints, gates, run = _route_tile(_dot_x3(h2, wr_ref[...]) + br_ref[...], run_ref[...])
    ri_ref[...] = ints
    rg_ref[...] = gates
    run_ref[...] = run
    cnt_ref[...] = run.astype(jnp.int32)


def _out_proj(x, y_rwkv, y_attn, wo, ln2, wr, br, tm):
    t, d = x.shape
    row = lambda i: (i, 0)
    fixed = lambda i: (0, 0)
    return pl.pallas_call(
        _out_proj_kernel,
        grid=(t // tm,),
        in_specs=[
            pl.BlockSpec((tm, d), row),
            pl.BlockSpec((tm, RWKV_WIDTH), row),
            pl.BlockSpec((tm, ATTN_WIDTH), row),
            pl.BlockSpec((d, d), fixed),
            pl.BlockSpec((1, d), fixed),
            pl.BlockSpec((d, LANES), fixed),
            pl.BlockSpec((1, LANES), fixed),
        ],
        out_specs=[pl.BlockSpec((tm, d), row), pl.BlockSpec((tm, d), row), pl.BlockSpec((tm, LANES), row),
                   pl.BlockSpec((tm, LANES), row), pl.BlockSpec((1, LANES), fixed)],
        out_shape=[jax.ShapeDtypeStruct((t, d), F32), jax.ShapeDtypeStruct((t, d), F32),
                   jax.ShapeDtypeStruct((t, LANES), jnp.int32), jax.ShapeDtypeStruct((t, LANES), F32),
                   jax.ShapeDtypeStruct((1, LANES), jnp.int32)],
        scratch_shapes=[pltpu.VMEM((1, LANES), F32)],
        compiler_params=_params("arbitrary"),
        name="out_proj_router",
    )(x, y_rwkv, y_attn, wo, ln2, wr, br)


def _for_rows(n, fn):
    full = lax.shift_right_logical(n, 3)

    def group(i, carry):
        for u in range(SUBLANES):
            fn(i, u)
        return carry

    def single(r, carry):
        fn(full, r)
        return carry

    lax.fori_loop(0, full, group, 0)
    lax.fori_loop(0, n - full * SUBLANES, single, 0)


def _moe_kernel(exp_ref, nvalid_ref, start_ref, eid_ref, rank_ref, h_hbm, wgu_ref, wdn_ref, y_hbm,
                xbuf, obuf, slot_ref, gsem, ssem):
    b = pl.program_id(0)
    last = pl.num_programs(0) - 1
    cur = lax.rem(b, 2)
    nv = nvalid_ref[b]
    n_tok = h_hbm.shape[0]

    def gather_copy(blk, buf, i, u):
        a = slot_ref[blk * MOE_BLOCK + i * SUBLANES + u]
        tok = lax.shift_right_logical(a, 1)
        return pltpu.make_async_copy(h_hbm.at[pl.ds(tok, 1), :], xbuf.at[buf, i, pl.ds(u, 1), :], gsem.at[buf])

    def scatter_copy(blk, buf, i, u):
        a = slot_ref[blk * MOE_BLOCK + i * SUBLANES + u]
        dst = (a & 1) * n_tok + lax.shift_right_logical(a, 1)
        return pltpu.make_async_copy(obuf.at[buf, i, pl.ds(u, 1), :], y_hbm.at[pl.ds(dst, 1), :], ssem.at[buf])

    def gather_start(blk, buf):
        _for_rows(nvalid_ref[blk], lambda i, u: gather_copy(blk, buf, i, u).start())

    def gather_wait(blk, buf):
        _for_rows(nvalid_ref[blk], lambda i, u: gather_copy(blk, buf, i, u).wait())

    def scatter_start(blk, buf):
        _for_rows(nvalid_ref[blk], lambda i, u: scatter_copy(blk, buf, i, u).start())

    def scatter_wait(blk, buf):
        _for_rows(nvalid_ref[blk], lambda i, u: scatter_copy(blk, buf, i, u).wait())

    @pl.when(b == 0)
    def _():
        xbuf[...] = jnp.zeros_like(xbuf)

        def place(i, carry):
            for u in range(SUBLANES):
                a = i * SUBLANES + u
                slot_ref[start_ref[eid_ref[a]] + rank_ref[a]] = a
            return carry

        lax.fori_loop(0, eid_ref.shape[0] // SUBLANES, place, 0)
        gather_start(0, 0)

    @pl.when(b < last)
    def _():
        gather_start(b + 1, 1 - cur)

    @pl.when(b >= 2)
    def _():
        scatter_wait(b - 2, cur)

    def experts(rows):
        tiles = rows // SUBLANES
        x = xbuf[cur, :tiles].reshape(rows, D_MODEL).astype(BF16)
        out = obuf.at[cur, :tiles]
        step = 256
        for j in range(D_EXPERT // step):
            wg = wgu_ref[0, :, j * step:(j + 1) * step].astype(BF16)
            wu = wgu_ref[0, :, D_EXPERT + j * step:D_EXPERT + (j + 1) * step].astype(BF16)
            hg = _dot(x, wg)
            hu = _dot(x, wu)
            act = (hg * jax.nn.sigmoid(hg) * hu).astype(BF16)
            part = _dot(act, wdn_ref[0, j * step:(j + 1) * step, :].astype(BF16))
            part = part.reshape(tiles, SUBLANES, D_MODEL)
            if j == 0:
                out[...] = part
            else:
                out[...] += part

    @pl.when(nv > 0)
    def _():
        gather_wait(b, cur)

        @pl.when(nv > MOE_BLOCK // 2)
        def _():
            experts(MOE_BLOCK)

        @pl.when(nv <= MOE_BLOCK // 2)
        def _():
            experts(MOE_BLOCK // 2)

        scatter_start(b, cur)

    @pl.when(b == last)
    def _():
        @pl.when(b >= 1)
        def _():
            scatter_wait(b - 1, 1 - cur)

        scatter_wait(b, cur)


def _moe(h2, w_gu, w_dn, blk_exp, nvalid, pad_start, eid, rank):
    t, d = h2.shape
    n_blocks = blk_exp.shape[0]
    imap = lambda b, e, *_: (e[b], 0, 0)
    return pl.pallas_call(
        _moe_kernel,
        grid_spec=pltpu.PrefetchScalarGridSpec(
            num_scalar_prefetch=5,
            grid=(n_blocks,),
            in_specs=[
                pl.BlockSpec(memory_space=pl.ANY),
                pl.BlockSpec((1, d, 2 * D_EXPERT), imap),
                pl.BlockSpec((1, D_EXPERT, d), imap),
            ],
            out_specs=pl.BlockSpec(memory_space=pl.ANY),
            scratch_shapes=[
                pltpu.VMEM((2, MOE_BLOCK // SUBLANES, SUBLANES, d), F32),
                pltpu.VMEM((2, MOE_BLOCK // SUBLANES, SUBLANES, d), F32),
                pltpu.SMEM((n_blocks * MOE_BLOCK,), jnp.int32),
                pltpu.SemaphoreType.DMA((2,)),
                pltpu.SemaphoreType.DMA((2,)),
            ],
        ),
        out_shape=jax.ShapeDtypeStruct((TOP_K_INNER * t, d), F32),
        compiler_params=_params("arbitrary"),
        name="moe_experts",
    )(blk_exp, nvalid, pad_start, eid, rank, h2, w_gu, w_dn)


def _block_tables(counts, m_assign):
    padded = ((counts + MOE_BLOCK - 1) // MOE_BLOCK) * MOE_BLOCK
    pad_end = jnp.cumsum(padded)
    pad_start = pad_end - padded
    n_blocks = -(-m_assign // MOE_BLOCK) + N_EXPERTS
    blk_start = jnp.arange(n_blocks, dtype=jnp.int32) * MOE_BLOCK
    blk_exp = jnp.minimum(jnp.sum(pad_end[None, :] <= blk_start[:, None], axis=1), N_EXPERTS - 1).astype(jnp.int32)
    nvalid = jnp.clip(counts[blk_exp] - (blk_start - pad_start[blk_exp]), 0, MOE_BLOCK)
    nvalid = jnp.where(blk_start < pad_end[-1], nvalid, 0).astype(jnp.int32)
    return blk_exp, nvalid, pad_start.astype(jnp.int32)


def _final_kernel(x_ref, ya_ref, yb_ref, rg_ref, g_ref, o_ref):
    x = x_ref[...] + rg_ref[:, 0:1] * ya_ref[...] + rg_ref[:, 1:2] * yb_ref[...]
    o_ref[...] = _rms(x, g_ref[...])


def _final(x1, y2, gates, gain, tm):
    t, d = x1.shape
    nt = t // tm
    return pl.pallas_call(
        _final_kernel,
        grid=(nt,),
        in_specs=[
            pl.BlockSpec((tm, d), lambda i: (i, 0)),
            pl.BlockSpec((tm, d), lambda i: (i, 0)),
            pl.BlockSpec((tm, d), lambda i: (i + nt, 0)),
            pl.BlockSpec((tm, LANES), lambda i: (i, 0)),
            pl.BlockSpec((1, d), lambda i: (0, 0)),
        ],
        out_specs=pl.BlockSpec((tm, d), lambda i: (i, 0)),
        out_shape=jax.ShapeDtypeStruct((t, d), F32),
        compiler_params=_params("parallel"),
        name="final_norm",
    )(x1, y2, y2, gates, gain)


def _pad_cols(a, n):
    return jnp.pad(a, ((0, 0), (0, n - a.shape[1])))


def _rwkv_mix(p_rwkv, mu, w0, wdu, a0, wau, wgu, k_k, k_a, r_k, gn_w, gn_b):
    row = lambda a: a.reshape(1, -1)
    outs = _prep(p_rwkv, _pad_cols(row(mu), RWKV_COLS_PAD), row(w0), wdu, row(a0), wau,
                 jnp.pad(wgu, ((0, GATE_LORA_PAD - GATE_LORA), (0, 0))), row(k_k), row(k_a), row(r_k), tm=256)
    return _scan(*outs, row(gn_w), row(gn_b))


def _mix_and_route(x, y_rwkv, y_attn, w_out, ln2, w_rg, b_rg, w_re, b_re):
    row = lambda a: a.reshape(1, -1)
    wr = _pad_cols(jnp.concatenate([w_rg, w_re], axis=1), LANES)
    br = _pad_cols(row(jnp.concatenate([b_rg, b_re])), LANES)
    return _out_proj(x, y_rwkv, y_attn, w_out.astype(BF16), row(ln2), wr, br, tm=256)


def _experts(h2, route_ints, counts, w_gu, w_dn):
    t = h2.shape[0]
    blk_exp, nvalid, pad_start = _block_tables(counts[0, :N_EXPERTS], t * TOP_K_INNER)
    eid = route_ints[:, 0:TOP_K_INNER].reshape(-1)
    rank = route_ints[:, TOP_K_INNER:2 * TOP_K_INNER].reshape(-1)
    return _moe(h2, w_gu, w_dn, blk_exp, nvalid, pad_start, eid, rank)


def _layer(x, positions, ln1, w_in, mu, w0, wdu, a0, wau, wgu, k_k, k_a, r_k, gn_w, gn_b, sinks, w_out, ln2,
           w_rg, b_rg, w_re, b_re, w_gu, w_dn, ln_f):
    t = x.shape[0]
    row = lambda a: a.reshape(1, -1)
    w_rwkv = _pad_cols(w_in[:, :RWKV_COLS], RWKV_COLS_PAD).astype(BF16)
    w_attn = w_in[:, RWKV_COLS:].astype(BF16)
    p_rwkv = _in_proj(x, row(ln1), w_rwkv, tm=min(1024, t), tn=RWKV_COLS_PAD // 3)
    qkv = _in_proj(x, row(ln1), w_attn, tm=min(1024, t), tn=ATTN_COLS)

    y_rwkv = _rwkv_mix(p_rwkv, mu, w0, wdu, a0, wau, wgu, k_k, k_a, r_k, gn_w, gn_b)

    inv_freq = ROPE_THETA ** (-jnp.arange(0, HEAD_DIM, 2, dtype=F32) / HEAD_DIM)
    freq = jnp.tile(inv_freq, 2 * LANES // HEAD_DIM).reshape(1, LANES)
    y_attn = _attn(qkv, positions.reshape(t, 1), freq, sinks)

    x1, h2, route_ints, gates, counts = _mix_and_route(x, y_rwkv, y_attn, w_out, ln2, w_rg, b_rg, w_re, b_re)
    y2 = _experts(h2, route_ints, counts, w_gu, w_dn)
    return _final(x1, y2, gates, row(ln_f), tm=256)


def kernel(x, positions, ln1, w_in, mu_shift, w_decay0, w_decay_up, a0, w_a_up, w_g_up, k_k, k_a, r_k, gn_w, gn_b,
           sinks, w_out, ln2, w_router_group, b_router_group, w_router_expert, b_router_expert, w_expert_gu,
           w_expert_down, ln_f):
    assert ln1.shape[0] == 1, "one trunk layer"
    outs = [
        _layer(x[i], positions[i], ln1[0], w_in[0], mu_shift[0], w_decay0[0], w_decay_up[0], a0[0], w_a_up[0],
               w_g_up[0], k_k[0], k_a[0], r_k[0], gn_w[0], gn_b[0], sinks[0], w_out[0], ln2[0], w_router_group[0],
               b_router_group[0], w_router_expert[0], b_router_expert[0], w_expert_gu[0], w_expert_down[0], ln_f)
        for i in range(x.shape[0])
    ]
    return jnp.stack(outs, axis=0)
```

```python
import math

import jax
import jax.numpy as jnp
from jax import lax
from jax.experimental import pallas as pl
from jax.experimental.pallas import tpu as pltpu

D_MODEL = 2048
HEAD_DIM = 64
RWKV_WIDTH = 1024
ATTN_WIDTH = 1024
ATTN_KV_HEADS = 4
ATTN_GROUP = 4
ATTN_KV_WIDTH = 256
WINDOW = 128
ROPE_THETA = 10000.0
DECAY_LORA = 64
AAA_LORA = 64
GATE_LORA = 160
GATE_LORA_PAD = 256
RWKV_COLS = 3 * RWKV_WIDTH + DECAY_LORA + AAA_LORA + GATE_LORA
RWKV_COLS_PAD = 3 * RWKV_WIDTH + DECAY_LORA + AAA_LORA + GATE_LORA_PAD
ATTN_COLS = ATTN_WIDTH + 2 * ATTN_KV_WIDTH
N_GROUPS = 8
EXPERTS_PER_GROUP = 8
N_EXPERTS = 64
TOP_K_INNER = 2
D_EXPERT = 768
MOE_BLOCK = 256
RMS_EPS = 1e-6
RWKV_GN_EPS = 64e-5

LANES = 128
SUBLANES = 8
CHUNK = 64
SCAN_CHUNKS = 8
EXPERT_SPLIT = 3
VMEM_LIMIT = 56 * 1024 * 1024

F32 = jnp.float32
BF16 = jnp.bfloat16


def _dot(a, b):
    return jnp.dot(a, b, preferred_element_type=F32)


def _split(x, parts):
    out = []
    for _ in range(parts - 1):
        hi = x.astype(BF16)
        out.append(hi)
        x = x - hi.astype(F32)
    out.append(x.astype(BF16))
    return out


def _dot_x3(a, b):
    ah, al = _split(a, 2)
    bh, bl = _split(b, 2)
    return _dot(ah, bh) + (_dot(ah, bl) + _dot(al, bh))


def _dot_sel(a, sel):
    ah, al = _split(a, 2)
    return _dot(ah, sel) + _dot(al, sel)


def _dot_nt(a, b):
    return lax.dot_general(a, b, (((1,), (1,)), ((), ())), preferred_element_type=F32)


def _dot_tn(a, b):
    return lax.dot_general(a, b, (((0,), (0,)), ((), ())), preferred_element_type=F32)


def _params(*sem):
    return pltpu.CompilerParams(dimension_semantics=sem, vmem_limit_bytes=VMEM_LIMIT)


def _head_sum_matrix():
    r = lax.broadcasted_iota(jnp.int32, (LANES, LANES), 0) // HEAD_DIM
    c = lax.broadcasted_iota(jnp.int32, (LANES, LANES), 1) // HEAD_DIM
    return (r == c).astype(BF16)


def _head_sums(x, bd):
    parts = [_dot_sel(x[:, j * LANES:(j + 1) * LANES], bd) for j in range(x.shape[1] // LANES)]
    return jnp.concatenate(parts, axis=1)


def _rms(x, gain):
    return x * lax.rsqrt(jnp.mean(x * x, axis=-1, keepdims=True) + RMS_EPS) * gain


def _in_proj_kernel(x_ref, g_ref, w_ref, o_ref, h_ref):
    @pl.when(pl.program_id(1) == 0)
    def _():
        h_ref[...] = _rms(x_ref[...], g_ref[...]).astype(BF16)

    o_ref[...] = _dot(h_ref[...], w_ref[...]).astype(o_ref.dtype)


def _in_proj(x, gain, w, tm, tn, out_dtype):
    t, d = x.shape
    n = w.shape[1]
    return pl.pallas_call(
        _in_proj_kernel,
        grid=(t // tm, n // tn),
        in_specs=[
            pl.BlockSpec((tm, d), lambda i, j: (i, 0)),
            pl.BlockSpec((1, d), lambda i, j: (0, 0)),
            pl.BlockSpec((d, tn), lambda i, j: (0, j)),
        ],
        out_specs=pl.BlockSpec((tm, tn), lambda i, j: (i, j)),
        out_shape=jax.ShapeDtypeStruct((t, n), out_dtype),
        scratch_shapes=[pltpu.VMEM((tm, d), BF16)],
        compiler_params=_params("parallel", "arbitrary"),
        name="in_proj",
    )(x, gain, w)


def _prep_kernel(p_ref, prev_ref, mu_ref, w0_ref, wdu_ref, a0_ref, wau_ref, wgu_ref, kk_ref, ka_ref, rk_ref,
                 at_ref, rt_ref, bt_ref, kt_ref, bh_ref, kh_ref, v_ref, gam_ref, g_ref, bv_ref):
    i = pl.program_id(0)
    tm = p_ref.shape[0]
    w = RWKV_WIDTH
    p = p_ref[...]
    prev_row = jnp.where(i == 0, 0.0, prev_ref[7:8, :])
    row = lax.broadcasted_iota(jnp.int32, p.shape, 0)
    shifted = jnp.where(row == 0, prev_row, pltpu.roll(p, 1, 0))
    ps = p + (shifted - p) * mu_ref[...]

    r = ps[:, 0:w]
    k = ps[:, w:2 * w]
    v = ps[:, 2 * w:3 * w]
    wd = ps[:, 3 * w:3 * w + DECAY_LORA]
    ad = ps[:, 3 * w + DECAY_LORA:3 * w + DECAY_LORA + AAA_LORA]
    gd = ps[:, 3 * w + DECAY_LORA + AAA_LORA:]

    z = w0_ref[...] + _dot_x3(jnp.tanh(wd), wdu_ref[...])
    logw = -math.exp(-0.5) * jax.nn.sigmoid(z)
    alr = jax.nn.sigmoid(a0_ref[...] + _dot_x3(ad, wau_ref[...]))
    g_ref[...] = _dot_x3(jax.nn.sigmoid(gd), wgu_ref[...])

    bd = _head_sum_matrix()
    kk = k * kk_ref[...]
    kk = kk / jnp.maximum(jnp.sqrt(_head_sums(kk * kk, bd)), 1e-12)
    kmod = k * (1.0 + (alr - 1.0) * ka_ref[...])
    b = kk * alr
    bv_ref[...] = _head_sums(r * kmod * rk_ref[...], bd) * v
    v_ref[...] = v.astype(BF16)

    rr = lax.broadcasted_iota(jnp.int32, (tm, tm), 0)
    cc = lax.broadcasted_iota(jnp.int32, (tm, tm), 1)
    tri = ((cc <= rr) & (cc // CHUNK == rr // CHUNK)).astype(BF16)
    l1, l2, l3 = _split(logw, 3)
    cum = _dot(tri, l1) + (_dot(tri, l2) + _dot(tri, l3))
    tot_rows = []
    for c in range(tm // CHUNK):
        last = cum[c * CHUNK + CHUNK - 1:c * CHUNK + CHUNK, :]
        gam_ref[c] = jnp.exp(last)
        tot_rows.append(jnp.broadcast_to(last, (CHUNK, w)))
    tot = jnp.concatenate(tot_rows, axis=0)

    e_neg = jnp.exp(-cum)
    e_rem = jnp.exp(tot - cum)
    at_ref[...] = (-kk * jnp.exp(cum - logw)).astype(BF16)
    rt_ref[...] = (r * jnp.exp(cum)).astype(BF16)
    bt_ref[...] = (b * e_neg).astype(BF16)
    kt_ref[...] = (kmod * e_neg).astype(BF16)
    bh_ref[...] = (b * e_rem).astype(BF16)
    kh_ref[...] = (kmod * e_rem).astype(BF16)


def _prep(p_rwkv, mu, w0, wdu, a0, wau, wgu, k_k, k_a, r_k, tm):
    t = p_rwkv.shape[0]
    w = RWKV_WIDTH
    cp = RWKV_COLS_PAD
    nc = tm // CHUNK
    row = lambda i: (i, 0)
    fixed = lambda i: (0, 0)
    vec = pl.BlockSpec((1, w), fixed)
    big_bf = jax.ShapeDtypeStruct((t, w), BF16)
    big_f32 = jax.ShapeDtypeStruct((t, w), F32)
    out_tile = pl.BlockSpec((tm, w), row)
    return pl.pallas_call(
        _prep_kernel,
        grid=(t // tm,),
        in_specs=[
            pl.BlockSpec((tm, cp), row),
            pl.BlockSpec((8, cp), lambda i: (jnp.maximum(i * (tm // 8) - 1, 0), 0)),
            pl.BlockSpec((1, cp), fixed),
            vec, pl.BlockSpec((DECAY_LORA, w), fixed),
            vec, pl.BlockSpec((AAA_LORA, w), fixed),
            pl.BlockSpec((GATE_LORA_PAD, w), fixed),
            vec, vec, vec,
        ],
        out_specs=[out_tile] * 7 + [pl.BlockSpec((nc, 1, w), lambda i: (i, 0, 0)), out_tile, out_tile],
        out_shape=[big_bf] * 7 + [jax.ShapeDtypeStruct((t // CHUNK, 1, w), F32), big_f32, big_f32],
        compiler_params=_params("parallel"),
        name="rwkv_prep",
    )(p_rwkv, p_rwkv, mu, w0, wdu, a0, wau, wgu, k_k, k_a, r_k)


def _chunk_factors(nchunks, own, strict, incl, eye, at_ref, rt_ref, bt_ref, kt_ref, bh_ref, kh_ref, v_ref, gam_ref,
                   tick):
    js = range(nchunks)
    n = 2 * CHUNK

    def stack(ref):
        xs = [ref[pl.ds(j * CHUNK, CHUNK), :] for j in js]
        return [jnp.where(own, jnp.concatenate([x, x], axis=0), jnp.zeros((), x.dtype)) for x in xs]

    a_s, r_s, b_s, k_s = stack(at_ref), stack(rt_ref), stack(bt_ref), stack(kt_ref)
    bh_s, kh_s, v_s = stack(bh_ref), stack(kh_ref), stack(v_ref)

    prod = [_dot_nt(jnp.concatenate([a_s[j], r_s[j]], axis=0), jnp.concatenate([b_s[j], k_s[j]], axis=0)) for j in js]
    a_ak = [jnp.where(strict, prod[j][:n, n:], 0.0).astype(BF16) for j in js]
    a_rb = [jnp.where(incl, prod[j][n:, :n], 0.0).astype(BF16) for j in js]
    a_rk = [jnp.where(incl, prod[j][n:, n:], 0.0).astype(BF16) for j in js]

    tick()
    pq = [jnp.concatenate([a_s[j].astype(F32), _dot(a_ak[j], v_s[j])], axis=1) for j in js]
    nk_b = [jnp.where(strict, prod[j][:n, :n], 0.0).astype(BF16) for j in js]
    span = 1
    while True:
        tick()
        pq = [pq[j] + _dot(nk_b[j], pq[j].astype(BF16)) for j in js]
        span *= 2
        if span >= CHUNK:
            break
        tick()
        nk_b = [_dot(nk_b[j], nk_b[j]).astype(BF16) for j in js]
    pq_b = [pq[j].astype(BF16) for j in js]

    mn = [_dot_tn(bh_s[j], pq_b[j]) for j in js]
    kv = [_dot_tn(kh_s[j], v_s[j]) for j in js]
    yy = [_dot(a_rb[j], pq_b[j]) for j in js]
    yv = [_dot(a_rk[j], v_s[j]) for j in js]
    out = []
    for j in js:
        m_mat = mn[j][:, :LANES] + jnp.where(eye, jnp.broadcast_to(gam_ref[j], (LANES, LANES)), 0.0)
        n_mat = mn[j][:, LANES:] + kv[j]
        y_c = r_s[j].astype(F32) + yy[j][:, :LANES]
        y_n = yy[j][:, LANES:] + yv[j]
        out.append((m_mat, n_mat, y_c, y_n))
    return out


def _scan_kernel(at_ref, rt_ref, bt_ref, kt_ref, bh_ref, kh_ref, v_ref, gam_ref, g_ref, bv_ref, gnw_ref, gnb_ref,
                 y_ref, h_ref, f_ref):
    @pl.when(pl.program_id(1) == 0)
    def _():
        h_ref[...] = jnp.zeros_like(h_ref)
        f_ref[...] = jnp.zeros_like(f_ref)

    chain = {"h": h_ref[...], "ys": []}

    def tick():
        j = len(chain["ys"])
        if j < SCAN_CHUNKS:
            h = chain["h"]
            y_st = _dot(f_ref[j, 2].astype(BF16), h.astype(BF16)) + f_ref[j, 3]
            chain["h"] = _dot_x3(f_ref[j, 0], h) + f_ref[j, 1]
            chain["ys"].append(y_st[:CHUNK] + y_st[CHUNK:])

    n = 2 * CHUNK
    own = (lax.broadcasted_iota(jnp.int32, (n, LANES), 1) // HEAD_DIM
           == lax.broadcasted_iota(jnp.int32, (n, LANES), 0) // CHUNK)
    ti = lax.broadcasted_iota(jnp.int32, (n, n), 0)
    si = lax.broadcasted_iota(jnp.int32, (n, n), 1)
    same = (ti // CHUNK) == (si // CHUNK)
    strict = same & (si < ti)
    incl = same & (si <= ti)
    eye = ti == si

    factors = _chunk_factors(SCAN_CHUNKS, own, strict, incl, eye, at_ref, rt_ref, bt_ref, kt_ref, bh_ref, kh_ref,
                             v_ref, gam_ref, tick)
    while len(chain["ys"]) < SCAN_CHUNKS:
        tick()
    h_ref[...] = chain["h"]
    y = jnp.concatenate(chain["ys"], axis=0)

    low = lax.broadcasted_iota(jnp.int32, y.shape, 1) < HEAD_DIM

    def head_mean(x):
        s_low = jnp.sum(jnp.where(low, x, 0.0), axis=-1, keepdims=True)
        s_all = jnp.sum(x, axis=-1, keepdims=True)
        return jnp.where(low, s_low, s_all - s_low) * (1.0 / HEAD_DIM)

    yc = y - head_mean(y)
    var = head_mean(yc * yc)
    yn = yc * lax.rsqrt(var + RWKV_GN_EPS) * gnw_ref[...] + gnb_ref[...]
    y_ref[...] = ((yn + bv_ref[...]) * g_ref[...]).astype(BF16)

    for j, mats in enumerate(factors):
        for i, mat in enumerate(mats):
            f_ref[j, i] = mat


def _scan(at, rt, bt, kt, bh, kh, v, gam, g, bv, gn_w, gn_b):
    t = at.shape[0]
    rows = SCAN_CHUNKS * CHUNK
    steps = t // rows
    tile = pl.BlockSpec((rows, LANES), lambda p, s: (jnp.minimum(s, steps - 1), p))
    lagged = pl.BlockSpec((rows, LANES), lambda p, s: (jnp.maximum(s - 1, 0), p))
    vec = pl.BlockSpec((1, LANES), lambda p, s: (0, p))
    gam_spec = pl.BlockSpec((SCAN_CHUNKS, 1, LANES), lambda p, s: (jnp.minimum(s, steps - 1), 0, p))
    return pl.pallas_call(
        _scan_kernel,
        grid=(RWKV_WIDTH // LANES, steps + 1),
        in_specs=[tile] * 7 + [gam_spec, lagged, lagged, vec, vec],
        out_specs=lagged,
        out_shape=jax.ShapeDtypeStruct((t, RWKV_WIDTH), BF16),
        scratch_shapes=[pltpu.VMEM((LANES, LANES), F32), pltpu.VMEM((SCAN_CHUNKS, 4, 2 * CHUNK, LANES), F32)],
        compiler_params=_params("parallel", "arbitrary"),
        name="rwkv_scan",
    )(at, rt, bt, kt, bh, kh, v, gam, g, bv, gn_w, gn_b)


def _attn_kernel(sink_ref, qkv_ref, pos_ref, freq_ref, o_ref, kprev_ref, vprev_ref):
    nb = pl.program_id(0)

    @pl.when(nb == 0)
    def _():
        kprev_ref[...] = jnp.zeros_like(kprev_ref)
        vprev_ref[...] = jnp.zeros_like(vprev_ref)

    ang = pos_ref[...].astype(F32) * freq_ref[...]
    cos = jnp.cos(ang)
    lane = lax.broadcasted_iota(jnp.int32, ang.shape, 1)
    sin = jnp.where(lane % HEAD_DIM < HEAD_DIM // 2, -1.0, 1.0) * jnp.sin(ang)
    first_half = lane % HEAD_DIM < HEAD_DIM // 2

    def rope(x):
        swapped = jnp.where(first_half, pltpu.roll(x, LANES - HEAD_DIM // 2, 1), pltpu.roll(x, HEAD_DIM // 2, 1))
        return x * cos + swapped * sin

    scale = HEAD_DIM ** -0.5
    qi = lax.broadcasted_iota(jnp.int32, (WINDOW, 2 * WINDOW), 0) + WINDOW
    ki = lax.broadcasted_iota(jnp.int32, (WINDOW, 2 * WINDOW), 1)
    rel = qi - ki
    mask = (rel >= 0) & (rel < WINDOW) & ((nb > 0) | (ki >= WINDOW))

    k_cur = jnp.concatenate([rope(qkv_ref[:, ATTN_WIDTH + j * LANES:ATTN_WIDTH + (j + 1) * LANES].astype(F32))
                             for j in range(ATTN_KV_WIDTH // LANES)], axis=1)
    v_cur = qkv_ref[:, ATTN_WIDTH + ATTN_KV_WIDTH:].astype(F32)
    k_band = jnp.concatenate([kprev_ref[...], k_cur], axis=0)
    v_band = jnp.concatenate([vprev_ref[...], v_cur], axis=0)
    k_heads = [k_band[:, g * HEAD_DIM:(g + 1) * HEAD_DIM].astype(BF16) for g in range(ATTN_KV_HEADS)]
    v_heads = [v_band[:, g * HEAD_DIM:(g + 1) * HEAD_DIM].astype(BF16) for g in range(ATTN_KV_HEADS)]

    heads = range(ATTN_WIDTH // HEAD_DIM)
    per_group = LANES // HEAD_DIM
    q_groups = [rope(qkv_ref[:, j * LANES:(j + 1) * LANES].astype(F32)) * scale for j in range(ATTN_WIDTH // LANES)]
    q = [q_groups[h // per_group][:, (h % per_group) * HEAD_DIM:(h % per_group + 1) * HEAD_DIM].astype(BF16)
         for h in heads]
    s = [jnp.where(mask, _dot_nt(q[h], k_heads[h // ATTN_GROUP]), -jnp.inf) for h in heads]
    probs = []
    for h in heads:
        sink = sink_ref[h]
        m = jnp.maximum(jnp.max(s[h], axis=-1, keepdims=True), sink)
        pexp = jnp.exp(s[h] - m)
        denom = jnp.sum(pexp, axis=-1, keepdims=True) + jnp.exp(sink - m)
        probs.append((pexp / denom).astype(BF16))
    o = [_dot(probs[h], v_heads[h // ATTN_GROUP]) for h in heads]
    for j in range(ATTN_WIDTH // LANES):
        o_ref[:, j * LANES:(j + 1) * LANES] = jnp.concatenate(o[j * per_group:(j + 1) * per_group],
                                                               axis=1).astype(BF16)

    kprev_ref[...] = k_cur
    vprev_ref[...] = v_cur


def _attn(qkv, pos_col, freq, sinks):
    t = qkv.shape[0]
    return pl.pallas_call(
        _attn_kernel,
        grid_spec=pltpu.PrefetchScalarGridSpec(
            num_scalar_prefetch=1,
            grid=(t // WINDOW,),
            in_specs=[
                pl.BlockSpec((WINDOW, ATTN_COLS), lambda i, s: (i, 0)),
                pl.BlockSpec((WINDOW, 1), lambda i, s: (i, 0)),
                pl.BlockSpec((1, LANES), lambda i, s: (0, 0)),
            ],
            out_specs=pl.BlockSpec((WINDOW, ATTN_WIDTH), lambda i, s: (i, 0)),
            scratch_shapes=[pltpu.VMEM((WINDOW, ATTN_KV_WIDTH), F32), pltpu.VMEM((WINDOW, ATTN_KV_WIDTH), F32)],
        ),
        out_shape=jax.ShapeDtypeStruct((t, ATTN_WIDTH), BF16),
        compiler_params=_params("arbitrary"),
        name="swa_attn",
    )(sinks, qkv, pos_col, freq)


def _route_tile(lg, run):
    tm = lg.shape[0]
    lane = lax.broadcasted_iota(jnp.int32, lg.shape, 1)
    lane_f = lane.astype(F32)
    ninf = -jnp.inf

    def top(vals):
        best = jnp.max(vals, axis=-1, keepdims=True)
        idx = jnp.min(jnp.where(vals == best, lane_f, float(LANES)), axis=-1, keepdims=True)
        return best, idx

    gl = jnp.where(lane < N_GROUPS, lg, ninf)
    gmax, gidx = top(gl)
    pg_top = 1.0 / jnp.sum(jnp.exp(gl - gmax), axis=-1, keepdims=True)
    first = N_GROUPS + EXPERTS_PER_GROUP * gidx
    el = jnp.where((lane_f >= first) & (lane_f < first + EXPERTS_PER_GROUP), lg, ninf)
    v0, i0 = top(el)
    v1, i1 = top(jnp.where(lane_f == i0, ninf, el))
    ex = jnp.exp(v1 - v0)
    g0 = pg_top / (1.0 + ex)
    g1 = pg_top * ex / (1.0 + ex)
    e0 = i0 - N_GROUPS
    e1 = i1 - N_GROUPS

    hot0 = (lane_f == e0).astype(BF16)
    hot1 = (lane_f == e1).astype(BF16)
    rr = lax.broadcasted_iota(jnp.int32, (tm, tm), 0)
    cc = lax.broadcasted_iota(jnp.int32, (tm, tm), 1)
    before = (cc < rr).astype(BF16)
    cum = _dot(before, jnp.concatenate([hot0, hot1], axis=1))
    h0 = hot0.astype(F32)
    h1 = hot1.astype(F32)
    tot0 = jnp.sum(h0, axis=0, keepdims=True)
    tot1 = jnp.sum(h1, axis=0, keepdims=True)
    rank0 = jnp.sum(h0 * (run + cum[:, :LANES]), axis=-1, keepdims=True)
    rank1 = jnp.sum(h1 * (run + tot0 + cum[:, LANES:]), axis=-1, keepdims=True)
    ints = jnp.where(lane == 0, e0, jnp.where(lane == 1, e1, jnp.where(lane == 2, rank0, jnp.where(lane == 3, rank1, 0.0))))
    gates = jnp.where(lane == 0, g0, jnp.where(lane == 1, g1, 0.0))
    return ints.astype(jnp.int32), gates, run + tot0 + tot1


def _out_proj_kernel(x_ref, ya_ref, yb_ref, wo_ref, g_ref, wr_ref, br_ref,
                     x1_ref, h2_ref, ri_ref, rg_ref, cnt_ref, run_ref):
    @pl.when(pl.program_id(0) == 0)
    def _():
        run_ref[...] = jnp.zeros_like(run_ref)

    x1 = x_ref[...] + _dot(ya_ref[...], wo_ref[:RWKV_WIDTH, :]) + _dot(yb_ref[...], wo_ref[RWKV_WIDTH:, :])
    x1_ref[...] = x1
    h2 = _rms(x1, g_ref[...])
    h2_ref[...] = h2
    ints, gates, run = _route_tile(_dot_x3(h2, wr_ref[...]) + br_ref[...], run_ref[...])
    ri_ref[...] = ints
    rg_ref[...] = gates
    run_ref[...] = run
    cnt_ref[...] = run.astype(jnp.int32)


def _out_proj(x, y_rwkv, y_attn, wo, ln2, wr, br, tm):
    t, d = x.shape
    row = lambda i: (i, 0)
    fixed = lambda i: (0, 0)
    return pl.pallas_call(
        _out_proj_kernel,
        grid=(t // tm,),
        in_specs=[
            pl.BlockSpec((tm, d), row),
            pl.BlockSpec((tm, RWKV_WIDTH), row),
            pl.BlockSpec((tm, ATTN_WIDTH), row),
            pl.BlockSpec((d, d), fixed),
            pl.BlockSpec((1, d), fixed),
            pl.BlockSpec((d, LANES), fixed),
            pl.BlockSpec((1, LANES), fixed),
        ],
        out_specs=[pl.BlockSpec((tm, d), row), pl.BlockSpec((tm, d), row), pl.BlockSpec((tm, LANES), row),
                   pl.BlockSpec((tm, LANES), row), pl.BlockSpec((1, LANES), fixed)],
        out_shape=[jax.ShapeDtypeStruct((t, d), F32), jax.ShapeDtypeStruct((t, d), F32),
                   jax.ShapeDtypeStruct((t, LANES), jnp.int32), jax.ShapeDtypeStruct((t, LANES), F32),
                   jax.ShapeDtypeStruct((1, LANES), jnp.int32)],
        scratch_shapes=[pltpu.VMEM((1, LANES), F32)],
        compiler_params=_params("arbitrary"),
        name="out_proj_router",
    )(x, y_rwkv, y_attn, wo, ln2, wr, br)


def _for_rows(n, fn):
    full = lax.shift_right_logical(n, 3)

    def group(i, carry):
        for u in range(SUBLANES):
            fn(i, u)
        return carry

    def single(r, carry):
        fn(full, r)
        return carry

    lax.fori_loop(0, full, group, 0)
    lax.fori_loop(0, n - full * SUBLANES, single, 0)


def _moe_kernel(exp_ref, nvalid_ref, start_ref, eid_ref, rank_ref, h_hbm, *refs):
    wg_refs, wu_refs, wd_refs = (refs[i * EXPERT_SPLIT:(i + 1) * EXPERT_SPLIT] for i in range(3))
    y_hbm, xbuf, obuf, slot_ref, gsem, ssem = refs[3 * EXPERT_SPLIT:]
    b = pl.program_id(0)
    last = pl.num_programs(0) - 1
    cur = lax.rem(b, 2)
    nv = nvalid_ref[b]
    n_tok = h_hbm.shape[0]

    def gather_copy(blk, buf, i, u):
        a = slot_ref[blk * MOE_BLOCK + i * SUBLANES + u]
        tok = lax.shift_right_logical(a, 1)
        return pltpu.make_async_copy(h_hbm.at[pl.ds(tok, 1), :], xbuf.at[buf, i, pl.ds(u, 1), :], gsem.at[buf])

    def scatter_copy(blk, buf, i, u):
        a = slot_ref[blk * MOE_BLOCK + i * SUBLANES + u]
        dst = (a & 1) * n_tok + lax.shift_right_logical(a, 1)
        return pltpu.make_async_copy(obuf.at[buf, i, pl.ds(u, 1), :], y_hbm.at[pl.ds(dst, 1), :], ssem.at[buf])

    def gather_start(blk, buf):
        _for_rows(nvalid_ref[blk], lambda i, u: gather_copy(blk, buf, i, u).start())

    def gather_wait(blk, buf):
        _for_rows(nvalid_ref[blk], lambda i, u: gather_copy(blk, buf, i, u).wait())

    def scatter_start(blk, buf):
        _for_rows(nvalid_ref[blk], lambda i, u: scatter_copy(blk, buf, i, u).start())

    def scatter_wait(blk, buf):
        _for_rows(nvalid_ref[blk], lambda i, u: scatter_copy(blk, buf, i, u).wait())

    @pl.when(b == 0)
    def _():
        xbuf[...] = jnp.zeros_like(xbuf)

        def place(i, carry):
            for u in range(SUBLANES):
                a = i * SUBLANES + u
                slot_ref[start_ref[eid_ref[a]] + rank_ref[a]] = a
            return carry

        lax.fori_loop(0, eid_ref.shape[0] // SUBLANES, place, 0)
        gather_start(0, 0)

    @pl.when(b < last)
    def _():
        gather_start(b + 1, 1 - cur)

    @pl.when(b >= 2)
    def _():
        scatter_wait(b - 2, cur)

    def experts(rows):
        tiles = rows // SUBLANES
        x = xbuf[cur, :tiles].reshape(rows, D_MODEL).astype(BF16)
        out = obuf.at[cur, :tiles]
        for j in range(EXPERT_SPLIT):
            hg = _dot(x, wg_refs[j][0].astype(BF16))
            hu = _dot(x, wu_refs[j][0].astype(BF16))
            act = (hg * jax.nn.sigmoid(hg) * hu).astype(BF16)
            part = _dot(act, wd_refs[j][0].astype(BF16))
            part = part.reshape(tiles, SUBLANES, D_MODEL)
            if j == 0:
                out[...] = part
            else:
                out[...] += part

    @pl.when(nv > 0)
    def _():
        gather_wait(b, cur)

        @pl.when(nv > MOE_BLOCK // 2)
        def _():
            experts(MOE_BLOCK)

        @pl.when(nv <= MOE_BLOCK // 2)
        def _():
            experts(MOE_BLOCK // 2)

        scatter_start(b, cur)

    @pl.when(b == last)
    def _():
        @pl.when(b >= 1)
        def _():
            scatter_wait(b - 1, 1 - cur)

        scatter_wait(b, cur)


def _moe(h2, w_gu, w_dn, blk_exp, nvalid, pad_start, eid, rank):
    t, d = h2.shape
    n_blocks = blk_exp.shape[0]
    cols = D_EXPERT // EXPERT_SPLIT
    gate_specs = [pl.BlockSpec((1, d, cols), lambda b, e, *_, j=j: (e[b], 0, j)) for j in range(EXPERT_SPLIT)]
    up_specs = [pl.BlockSpec((1, d, cols), lambda b, e, *_, j=j: (e[b], 0, EXPERT_SPLIT + j))
                for j in range(EXPERT_SPLIT)]
    down_specs = [pl.BlockSpec((1, cols, d), lambda b, e, *_, j=j: (e[b], j, 0)) for j in range(EXPERT_SPLIT)]
    return pl.pallas_call(
        _moe_kernel,
        grid_spec=pltpu.PrefetchScalarGridSpec(
            num_scalar_prefetch=5,
            grid=(n_blocks,),
            in_specs=[pl.BlockSpec(memory_space=pl.ANY)] + gate_specs + up_specs + down_specs,
            out_specs=pl.BlockSpec(memory_space=pl.ANY),
            scratch_shapes=[
                pltpu.VMEM((2, MOE_BLOCK // SUBLANES, SUBLANES, d), F32),
                pltpu.VMEM((2, MOE_BLOCK // SUBLANES, SUBLANES, d), F32),
                pltpu.SMEM((n_blocks * MOE_BLOCK,), jnp.int32),
                pltpu.SemaphoreType.DMA((2,)),
                pltpu.SemaphoreType.DMA((2,)),
            ],
        ),
        out_shape=jax.ShapeDtypeStruct((TOP_K_INNER * t, d), F32),
        compiler_params=_params("arbitrary"),
        name="moe_experts",
    )(blk_exp, nvalid, pad_start, eid, rank, h2, *([w_gu] * (2 * EXPERT_SPLIT)), *([w_dn] * EXPERT_SPLIT))


def _block_tables(counts, m_assign):
    padded = ((counts + MOE_BLOCK - 1) // MOE_BLOCK) * MOE_BLOCK
    pad_end = jnp.cumsum(padded)
    pad_start = pad_end - padded
    n_blocks = -(-m_assign // MOE_BLOCK) + N_EXPERTS
    blk_start = jnp.arange(n_blocks, dtype=jnp.int32) * MOE_BLOCK
    blk_exp = jnp.minimum(jnp.sum(pad_end[None, :] <= blk_start[:, None], axis=1), N_EXPERTS - 1).astype(jnp.int32)
    nvalid = jnp.clip(counts[blk_exp] - (blk_start - pad_start[blk_exp]), 0, MOE_BLOCK)
    nvalid = jnp.where(blk_start < pad_end[-1], nvalid, 0).astype(jnp.int32)
    return blk_exp, nvalid, pad_start.astype(jnp.int32)


def _final_kernel(x_ref, ya_ref, yb_ref, rg_ref, g_ref, o_ref):
    x = x_ref[...] + rg_ref[:, 0:1] * ya_ref[...] + rg_ref[:, 1:2] * yb_ref[...]
    o_ref[...] = _rms(x, g_ref[...])


def _final(x1, y2, gates, gain, tm):
    t, d = x1.shape
    nt = t // tm
    return pl.pallas_call(
        _final_kernel,
        grid=(nt,),
        in_specs=[
            pl.BlockSpec((tm, d), lambda i: (i, 0)),
            pl.BlockSpec((tm, d), lambda i: (i, 0)),
            pl.BlockSpec((tm, d), lambda i: (i + nt, 0)),
            pl.BlockSpec((tm, LANES), lambda i: (i, 0)),
            pl.BlockSpec((1, d), lambda i: (0, 0)),
        ],
        out_specs=pl.BlockSpec((tm, d), lambda i: (i, 0)),
        out_shape=jax.ShapeDtypeStruct((t, d), F32),
        compiler_params=_params("parallel"),
        name="final_norm",
    )(x1, y2, y2, gates, gain)


def _pad_cols(a, n):
    return jnp.pad(a, ((0, 0), (0, n - a.shape[1])))


def _rwkv_mix(p_rwkv, mu, w0, wdu, a0, wau, wgu, k_k, k_a, r_k, gn_w, gn_b):
    row = lambda a: a.reshape(1, -1)
    outs = _prep(p_rwkv, _pad_cols(row(mu), RWKV_COLS_PAD), row(w0), wdu, row(a0), wau,
                 jnp.pad(wgu, ((0, GATE_LORA_PAD - GATE_LORA), (0, 0))), row(k_k), row(k_a), row(r_k), tm=256)
    return _scan(*outs, row(gn_w), row(gn_b))


def _mix_and_route(x, y_rwkv, y_attn, w_out, ln2, w_rg, b_rg, w_re, b_re):
    row = lambda a: a.reshape(1, -1)
    wr = _pad_cols(jnp.concatenate([w_rg, w_re], axis=1), LANES)
    br = _pad_cols(row(jnp.concatenate([b_rg, b_re])), LANES)
    return _out_proj(x, y_rwkv, y_attn, w_out.astype(BF16), row(ln2), wr, br, tm=256)


def _experts(h2, route_ints, counts, w_gu, w_dn):
    t = h2.shape[0]
    blk_exp, nvalid, pad_start = _block_tables(counts[0, :N_EXPERTS], t * TOP_K_INNER)
    eid = route_ints[:, 0:TOP_K_INNER].reshape(-1)
    rank = route_ints[:, TOP_K_INNER:2 * TOP_K_INNER].reshape(-1)
    return _moe(h2, w_gu, w_dn, blk_exp, nvalid, pad_start, eid, rank)


def _layer(x, positions, ln1, w_in, mu, w0, wdu, a0, wau, wgu, k_k, k_a, r_k, gn_w, gn_b, sinks, w_out, ln2,
           w_rg, b_rg, w_re, b_re, w_gu, w_dn, ln_f):
    t = x.shape[0]
    row = lambda a: a.reshape(1, -1)
    w_rwkv = _pad_cols(w_in[:, :RWKV_COLS], RWKV_COLS_PAD).astype(BF16)
    w_attn = w_in[:, RWKV_COLS:].astype(BF16)
    p_rwkv = _in_proj(x, row(ln1), w_rwkv, tm=min(1024, t), tn=RWKV_COLS_PAD // 3, out_dtype=F32)
    qkv = _in_proj(x, row(ln1), w_attn, tm=min(1024, t), tn=ATTN_COLS, out_dtype=BF16)

    y_rwkv = _rwkv_mix(p_rwkv, mu, w0, wdu, a0, wau, wgu, k_k, k_a, r_k, gn_w, gn_b)

    inv_freq = ROPE_THETA ** (-jnp.arange(0, HEAD_DIM, 2, dtype=F32) / HEAD_DIM)
    freq = jnp.tile(inv_freq, 2 * LANES // HEAD_DIM).reshape(1, LANES)
    y_attn = _attn(qkv, positions.reshape(t, 1), freq, sinks)

    x1, h2, route_ints, gates, counts = _mix_and_route(x, y_rwkv, y_attn, w_out, ln2, w_rg, b_rg, w_re, b_re)
    y2 = _experts(h2, route_ints, counts, w_gu, w_dn)
    return _final(x1, y2, gates, row(ln_f), tm=256)


def kernel(x, positions, ln1, w_in, mu_shift, w_decay0, w_decay_up, a0, w_a_up, w_g_up, k_k, k_a, r_k, gn_w, gn_b,
           sinks, w_out, ln2, w_router_group, b_router_group, w_router_expert, b_router_expert, w_expert_gu,
           w_expert_down, ln_f):
    assert ln1.shape[0] == 1, "one trunk layer"
    outs = [
        _layer(x[i], positions[i], ln1[0], w_in[0], mu_shift[0], w_decay0[0], w_decay_up[0], a0[0], w_a_up[0],
               w_g_up[0], k_k[0], k_a[0], r_k[0], gn_w[0], gn_b[0], sinks[0], w_out[0], ln2[0], w_router_group[0],
               b_router_group[0], w_router_expert[0], b_router_expert[0], w_expert_gu[0], w_expert_down[0], ln_f)
        for i in range(x.shape[0])
    ]
    return jnp.stack(outs, axis=0)
```

```python
import math

import jax
import jax.numpy as jnp
from jax import lax
from jax.experimental import pallas as pl
from jax.experimental.pallas import tpu as pltpu

D_MODEL = 2048
HEAD_DIM = 64
RWKV_WIDTH = 1024
ATTN_WIDTH = 1024
ATTN_KV_HEADS = 4
ATTN_GROUP = 4
ATTN_KV_WIDTH = 256
WINDOW = 128
ROPE_THETA = 10000.0
DECAY_LORA = 64
AAA_LORA = 64
GATE_LORA = 160
GATE_LORA_PAD = 256
RWKV_COLS = 3 * RWKV_WIDTH + DECAY_LORA + AAA_LORA + GATE_LORA
RWKV_COLS_PAD = 3 * RWKV_WIDTH + DECAY_LORA + AAA_LORA + GATE_LORA_PAD
ATTN_COLS = ATTN_WIDTH + 2 * ATTN_KV_WIDTH
N_GROUPS = 8
EXPERTS_PER_GROUP = 8
N_EXPERTS = 64
TOP_K_INNER = 2
D_EXPERT = 768
MOE_BLOCK = 256
RMS_EPS = 1e-6
RWKV_GN_EPS = 64e-5

LANES = 128
SUBLANES = 8
CHUNK = 64
SCAN_CHUNKS = 8
EXPERT_SPLIT = 3
VMEM_LIMIT = 56 * 1024 * 1024

F32 = jnp.float32
BF16 = jnp.bfloat16


def _dot(a, b):
    return jnp.dot(a, b, preferred_element_type=F32)


def _split(x, parts):
    out = []
    for _ in range(parts - 1):
        hi = x.astype(BF16)
        out.append(hi)
        x = x - hi.astype(F32)
    out.append(x.astype(BF16))
    return out


def _dot_x3(a, b):
    ah, al = _split(a, 2)
    bh, bl = _split(b, 2)
    return _dot(ah, bh) + (_dot(ah, bl) + _dot(al, bh))


def _dot_sel(a, sel):
    ah, al = _split(a, 2)
    return _dot(ah, sel) + _dot(al, sel)


def _dot_nt(a, b):
    return lax.dot_general(a, b, (((1,), (1,)), ((), ())), preferred_element_type=F32)


def _dot_tn(a, b):
    return lax.dot_general(a, b, (((0,), (0,)), ((), ())), preferred_element_type=F32)


def _params(*sem):
    return pltpu.CompilerParams(dimension_semantics=sem, vmem_limit_bytes=VMEM_LIMIT)


def _head_sum_matrix():
    r = lax.broadcasted_iota(jnp.int32, (LANES, LANES), 0) // HEAD_DIM
    c = lax.broadcasted_iota(jnp.int32, (LANES, LANES), 1) // HEAD_DIM
    return (r == c).astype(BF16)


def _head_sums(x, bd):
    parts = [_dot_sel(x[:, j * LANES:(j + 1) * LANES], bd) for j in range(x.shape[1] // LANES)]
    return jnp.concatenate(parts, axis=1)


def _rms(x, gain):
    return x * lax.rsqrt(jnp.mean(x * x, axis=-1, keepdims=True) + RMS_EPS) * gain


def _in_proj_kernel(x_ref, g_ref, w_ref, o_ref, h_ref):
    @pl.when(pl.program_id(1) == 0)
    def _():
        h_ref[...] = _rms(x_ref[...], g_ref[...]).astype(BF16)

    o_ref[...] = _dot(h_ref[...], w_ref[...]).astype(o_ref.dtype)


def _in_proj(x, gain, w, tm, tn, out_dtype):
    t, d = x.shape
    n = w.shape[1]
    return pl.pallas_call(
        _in_proj_kernel,
        grid=(t // tm, n // tn),
        in_specs=[
            pl.BlockSpec((tm, d), lambda i, j: (i, 0)),
            pl.BlockSpec((1, d), lambda i, j: (0, 0)),
            pl.BlockSpec((d, tn), lambda i, j: (0, j)),
        ],
        out_specs=pl.BlockSpec((tm, tn), lambda i, j: (i, j)),
        out_shape=jax.ShapeDtypeStruct((t, n), out_dtype),
        scratch_shapes=[pltpu.VMEM((tm, d), BF16)],
        compiler_params=_params("parallel", "arbitrary"),
        name="in_proj",
    )(x, gain, w)


def _prep_kernel(p_ref, prev_ref, mu_ref, w0_ref, wdu_ref, a0_ref, wau_ref, wgu_ref, kk_ref, ka_ref, rk_ref,
                 at_ref, rt_ref, bt_ref, kt_ref, bh_ref, kh_ref, v_ref, gam_ref, g_ref, bv_ref):
    i = pl.program_id(0)
    tm = p_ref.shape[0]
    w = RWKV_WIDTH
    p = p_ref[...]
    prev_row = jnp.where(i == 0, 0.0, prev_ref[7:8, :])
    row = lax.broadcasted_iota(jnp.int32, p.shape, 0)
    shifted = jnp.where(row == 0, prev_row, pltpu.roll(p, 1, 0))
    ps = p + (shifted - p) * mu_ref[...]

    r = ps[:, 0:w]
    k = ps[:, w:2 * w]
    v = ps[:, 2 * w:3 * w]
    wd = ps[:, 3 * w:3 * w + DECAY_LORA]
    ad = ps[:, 3 * w + DECAY_LORA:3 * w + DECAY_LORA + AAA_LORA]
    gd = ps[:, 3 * w + DECAY_LORA + AAA_LORA:]

    z = w0_ref[...] + _dot_x3(jnp.tanh(wd), wdu_ref[...])
    logw = -math.exp(-0.5) * jax.nn.sigmoid(z)
    alr = jax.nn.sigmoid(a0_ref[...] + _dot_x3(ad, wau_ref[...]))
    g_ref[...] = _dot_x3(jax.nn.sigmoid(gd), wgu_ref[...])

    bd = _head_sum_matrix()
    kk = k * kk_ref[...]
    kk = kk / jnp.maximum(jnp.sqrt(_head_sums(kk * kk, bd)), 1e-12)
    kmod = k * (1.0 + (alr - 1.0) * ka_ref[...])
    b = kk * alr
    bv_ref[...] = _head_sums(r * kmod * rk_ref[...], bd) * v
    v_ref[...] = v.astype(BF16)

    rr = lax.broadcasted_iota(jnp.int32, (tm, tm), 0)
    cc = lax.broadcasted_iota(jnp.int32, (tm, tm), 1)
    tri = ((cc <= rr) & (cc // CHUNK == rr // CHUNK)).astype(BF16)
    l1, l2, l3 = _split(logw, 3)
    cum = _dot(tri, l1) + (_dot(tri, l2) + _dot(tri, l3))
    tot_rows = []
    for c in range(tm // CHUNK):
        last = cum[c * CHUNK + CHUNK - 1:c * CHUNK + CHUNK, :]
        gam_ref[c] = jnp.exp(last)
        tot_rows.append(jnp.broadcast_to(last, (CHUNK, w)))
    tot = jnp.concatenate(tot_rows, axis=0)

    e_neg = jnp.exp(-cum)
    e_rem = jnp.exp(tot - cum)
    at_ref[...] = (-kk * jnp.exp(cum - logw)).astype(BF16)
    rt_ref[...] = (r * jnp.exp(cum)).astype(BF16)
    bt_ref[...] = (b * e_neg).astype(BF16)
    kt_ref[...] = (kmod * e_neg).astype(BF16)
    bh_ref[...] = (b * e_rem).astype(BF16)
    kh_ref[...] = (kmod * e_rem).astype(BF16)


def _prep(p_rwkv, mu, w0, wdu, a0, wau, wgu, k_k, k_a, r_k, tm):
    t = p_rwkv.shape[0]
    w = RWKV_WIDTH
    cp = RWKV_COLS_PAD
    nc = tm // CHUNK
    row = lambda i: (i, 0)
    fixed = lambda i: (0, 0)
    vec = pl.BlockSpec((1, w), fixed)
    big_bf = jax.ShapeDtypeStruct((t, w), BF16)
    big_f32 = jax.ShapeDtypeStruct((t, w), F32)
    out_tile = pl.BlockSpec((tm, w), row)
    return pl.pallas_call(
        _prep_kernel,
        grid=(t // tm,),
        in_specs=[
            pl.BlockSpec((tm, cp), row),
            pl.BlockSpec((8, cp), lambda i: (jnp.maximum(i * (tm // 8) - 1, 0), 0)),
            pl.BlockSpec((1, cp), fixed),
            vec, pl.BlockSpec((DECAY_LORA, w), fixed),
            vec, pl.BlockSpec((AAA_LORA, w), fixed),
            pl.BlockSpec((GATE_LORA_PAD, w), fixed),
            vec, vec, vec,
        ],
        out_specs=[out_tile] * 7 + [pl.BlockSpec((nc, 1, w), lambda i: (i, 0, 0)), out_tile, out_tile],
        out_shape=[big_bf] * 7 + [jax.ShapeDtypeStruct((t // CHUNK, 1, w), F32), big_f32, big_f32],
        compiler_params=_params("parallel"),
        name="rwkv_prep",
    )(p_rwkv, p_rwkv, mu, w0, wdu, a0, wau, wgu, k_k, k_a, r_k)


def _chunk_factors(nchunks, own, strict, incl, eye, at_ref, rt_ref, bt_ref, kt_ref, bh_ref, kh_ref, v_ref, gam_ref,
                   tick):
    js = range(nchunks)
    n = 2 * CHUNK

    def stack(ref):
        xs = [ref[pl.ds(j * CHUNK, CHUNK), :] for j in js]
        return [jnp.where(own, jnp.concatenate([x, x], axis=0), jnp.zeros((), x.dtype)) for x in xs]

    a_s, r_s, b_s, k_s = stack(at_ref), stack(rt_ref), stack(bt_ref), stack(kt_ref)
    bh_s, kh_s, v_s = stack(bh_ref), stack(kh_ref), stack(v_ref)

    prod = [_dot_nt(jnp.concatenate([a_s[j], r_s[j]], axis=0), jnp.concatenate([b_s[j], k_s[j]], axis=0)) for j in js]
    a_ak = [jnp.where(strict, prod[j][:n, n:], 0.0).astype(BF16) for j in js]
    a_rb = [jnp.where(incl, prod[j][n:, :n], 0.0).astype(BF16) for j in js]
    a_rk = [jnp.where(incl, prod[j][n:, n:], 0.0).astype(BF16) for j in js]

    tick()
    pq = [jnp.concatenate([a_s[j].astype(F32), _dot(a_ak[j], v_s[j])], axis=1) for j in js]
    nk_b = [jnp.where(strict, prod[j][:n, :n], 0.0).astype(BF16) for j in js]
    span = 1
    while True:
        tick()
        pq = [pq[j] + _dot(nk_b[j], pq[j].astype(BF16)) for j in js]
        span *= 2
        if span >= CHUNK:
            break
        tick()
        nk_b = [_dot(nk_b[j], nk_b[j]).astype(BF16) for j in js]
    pq_b = [pq[j].astype(BF16) for j in js]

    mn = [_dot_tn(bh_s[j], pq_b[j]) for j in js]
    kv = [_dot_tn(kh_s[j], v_s[j]) for j in js]
    yy = [_dot(a_rb[j], pq_b[j]) for j in js]
    yv = [_dot(a_rk[j], v_s[j]) for j in js]
    out = []
    for j in js:
        m_mat = mn[j][:, :LANES] + jnp.where(eye, jnp.broadcast_to(gam_ref[j], (LANES, LANES)), 0.0)
        n_mat = mn[j][:, LANES:] + kv[j]
        y_c = r_s[j].astype(F32) + yy[j][:, :LANES]
        y_n = yy[j][:, LANES:] + yv[j]
        out.append((m_mat, n_mat, y_c, y_n))
    return out


def _scan_kernel(at_ref, rt_ref, bt_ref, kt_ref, bh_ref, kh_ref, v_ref, gam_ref, g_ref, bv_ref, gnw_ref, gnb_ref,
                 y_ref, h_ref, f_ref):
    @pl.when(pl.program_id(1) == 0)
    def _():
        h_ref[...] = jnp.zeros_like(h_ref)
        f_ref[...] = jnp.zeros_like(f_ref)

    chain = {"h": h_ref[...], "ys": []}

    def tick():
        j = len(chain["ys"])
        if j < SCAN_CHUNKS:
            h = chain["h"]
            y_st = _dot(f_ref[j, 2].astype(BF16), h.astype(BF16)) + f_ref[j, 3]
            chain["h"] = _dot_x3(f_ref[j, 0], h) + f_ref[j, 1]
            chain["ys"].append(y_st[:CHUNK] + y_st[CHUNK:])

    n = 2 * CHUNK
    own = (lax.broadcasted_iota(jnp.int32, (n, LANES), 1) // HEAD_DIM
           == lax.broadcasted_iota(jnp.int32, (n, LANES), 0) // CHUNK)
    ti = lax.broadcasted_iota(jnp.int32, (n, n), 0)
    si = lax.broadcasted_iota(jnp.int32, (n, n), 1)
    same = (ti // CHUNK) == (si // CHUNK)
    strict = same & (si < ti)
    incl = same & (si <= ti)
    eye = ti == si

    factors = _chunk_factors(SCAN_CHUNKS, own, strict, incl, eye, at_ref, rt_ref, bt_ref, kt_ref, bh_ref, kh_ref,
                             v_ref, gam_ref, tick)
    while len(chain["ys"]) < SCAN_CHUNKS:
        tick()
    h_ref[...] = chain["h"]
    y = jnp.concatenate(chain["ys"], axis=0)

    low = lax.broadcasted_iota(jnp.int32, y.shape, 1) < HEAD_DIM

    def head_mean(x):
        s_low = jnp.sum(jnp.where(low, x, 0.0), axis=-1, keepdims=True)
        s_all = jnp.sum(x, axis=-1, keepdims=True)
        return jnp.where(low, s_low, s_all - s_low) * (1.0 / HEAD_DIM)

    yc = y - head_mean(y)
    var = head_mean(yc * yc)
    yn = yc * lax.rsqrt(var + RWKV_GN_EPS) * gnw_ref[...] + gnb_ref[...]
    y_ref[...] = ((yn + bv_ref[...]) * g_ref[...]).astype(BF16)

    for j, mats in enumerate(factors):
        for i, mat in enumerate(mats):
            f_ref[j, i] = mat


def _scan(at, rt, bt, kt, bh, kh, v, gam, g, bv, gn_w, gn_b):
    t = at.shape[0]
    rows = SCAN_CHUNKS * CHUNK
    steps = t // rows
    tile = pl.BlockSpec((rows, LANES), lambda p, s: (jnp.minimum(s, steps - 1), p))
    lagged = pl.BlockSpec((rows, LANES), lambda p, s: (jnp.maximum(s - 1, 0), p))
    vec = pl.BlockSpec((1, LANES), lambda p, s: (0, p))
    gam_spec = pl.BlockSpec((SCAN_CHUNKS, 1, LANES), lambda p, s: (jnp.minimum(s, steps - 1), 0, p))
    return pl.pallas_call(
        _scan_kernel,
        grid=(RWKV_WIDTH // LANES, steps + 1),
        in_specs=[tile] * 7 + [gam_spec, lagged, lagged, vec, vec],
        out_specs=lagged,
        out_shape=jax.ShapeDtypeStruct((t, RWKV_WIDTH), BF16),
        scratch_shapes=[pltpu.VMEM((LANES, LANES), F32), pltpu.VMEM((SCAN_CHUNKS, 4, 2 * CHUNK, LANES), F32)],
        compiler_params=_params("parallel", "arbitrary"),
        name="rwkv_scan",
    )(at, rt, bt, kt, bh, kh, v, gam, g, bv, gn_w, gn_b)


def _attn_kernel(sink_ref, qkv_ref, pos_ref, freq_ref, o_ref, kprev_ref, vprev_ref):
    nb = pl.program_id(0)

    @pl.when(nb == 0)
    def _():
        kprev_ref[...] = jnp.zeros_like(kprev_ref)
        vprev_ref[...] = jnp.zeros_like(vprev_ref)

    ang = pos_ref[...].astype(F32) * freq_ref[...]
    cos = jnp.cos(ang)
    lane = lax.broadcasted_iota(jnp.int32, ang.shape, 1)
    sin = jnp.where(lane % HEAD_DIM < HEAD_DIM // 2, -1.0, 1.0) * jnp.sin(ang)
    first_half = lane % HEAD_DIM < HEAD_DIM // 2

    def rope(x):
        swapped = jnp.where(first_half, pltpu.roll(x, LANES - HEAD_DIM // 2, 1), pltpu.roll(x, HEAD_DIM // 2, 1))
        return x * cos + swapped * sin

    scale = HEAD_DIM ** -0.5
    qi = lax.broadcasted_iota(jnp.int32, (WINDOW, 2 * WINDOW), 0) + WINDOW
    ki = lax.broadcasted_iota(jnp.int32, (WINDOW, 2 * WINDOW), 1)
    rel = qi - ki
    mask = (rel >= 0) & (rel < WINDOW) & ((nb > 0) | (ki >= WINDOW))

    k_cur = jnp.concatenate([rope(qkv_ref[:, ATTN_WIDTH + j * LANES:ATTN_WIDTH + (j + 1) * LANES].astype(F32))
                             for j in range(ATTN_KV_WIDTH // LANES)], axis=1)
    v_cur = qkv_ref[:, ATTN_WIDTH + ATTN_KV_WIDTH:].astype(F32)
    k_band = jnp.concatenate([kprev_ref[...], k_cur], axis=0)
    v_band = jnp.concatenate([vprev_ref[...], v_cur], axis=0)
    k_heads = [k_band[:, g * HEAD_DIM:(g + 1) * HEAD_DIM].astype(BF16) for g in range(ATTN_KV_HEADS)]
    v_heads = [v_band[:, g * HEAD_DIM:(g + 1) * HEAD_DIM].astype(BF16) for g in range(ATTN_KV_HEADS)]

    heads = range(ATTN_WIDTH // HEAD_DIM)
    per_group = LANES // HEAD_DIM
    q_groups = [rope(qkv_ref[:, j * LANES:(j + 1) * LANES].astype(F32)) * scale for j in range(ATTN_WIDTH // LANES)]
    q = [q_groups[h // per_group][:, (h % per_group) * HEAD_DIM:(h % per_group + 1) * HEAD_DIM].astype(BF16)
         for h in heads]
    s = [jnp.where(mask, _dot_nt(q[h], k_heads[h // ATTN_GROUP]), -jnp.inf) for h in heads]
    probs = []
    for h in heads:
        sink = sink_ref[h]
        m = jnp.maximum(jnp.max(s[h], axis=-1, keepdims=True), sink)
        pexp = jnp.exp(s[h] - m)
        denom = jnp.sum(pexp, axis=-1, keepdims=True) + jnp.exp(sink - m)
        probs.append((pexp / denom).astype(BF16))
    o = [_dot(probs[h], v_heads[h // ATTN_GROUP]) for h in heads]
    for j in range(ATTN_WIDTH // LANES):
        o_ref[:, j * LANES:(j + 1) * LANES] = jnp.concatenate(o[j * per_group:(j + 1) * per_group],
                                                               axis=1).astype(BF16)

    kprev_ref[...] = k_cur
    vprev_ref[...] = v_cur


def _attn(qkv, pos_col, freq, sinks):
    t = qkv.shape[0]
    return pl.pallas_call(
        _attn_kernel,
        grid_spec=pltpu.PrefetchScalarGridSpec(
            num_scalar_prefetch=1,
            grid=(t // WINDOW,),
            in_specs=[
                pl.BlockSpec((WINDOW, ATTN_COLS), lambda i, s: (i, 0)),
                pl.BlockSpec((WINDOW, 1), lambda i, s: (i, 0)),
                pl.BlockSpec((1, LANES), lambda i, s: (0, 0)),
            ],
            out_specs=pl.BlockSpec((WINDOW, ATTN_WIDTH), lambda i, s: (i, 0)),
            scratch_shapes=[pltpu.VMEM((WINDOW, ATTN_KV_WIDTH), F32), pltpu.VMEM((WINDOW, ATTN_KV_WIDTH), F32)],
        ),
        out_shape=jax.ShapeDtypeStruct((t, ATTN_WIDTH), BF16),
        compiler_params=_params("arbitrary"),
        name="swa_attn",
    )(sinks, qkv, pos_col, freq)


def _route_tile(lg, run):
    tm = lg.shape[0]
    lane = lax.broadcasted_iota(jnp.int32, lg.shape, 1)
    lane_f = lane.astype(F32)
    ninf = -jnp.inf

    def top(vals):
        best = jnp.max(vals, axis=-1, keepdims=True)
        idx = jnp.min(jnp.where(vals == best, lane_f, float(LANES)), axis=-1, keepdims=True)
        return best, idx

    gl = jnp.where(lane < N_GROUPS, lg, ninf)
    gmax, gidx = top(gl)
    pg_top = 1.0 / jnp.sum(jnp.exp(gl - gmax), axis=-1, keepdims=True)
    first = N_GROUPS + EXPERTS_PER_GROUP * gidx
    el = jnp.where((lane_f >= first) & (lane_f < first + EXPERTS_PER_GROUP), lg, ninf)
    v0, i0 = top(el)
    v1, i1 = top(jnp.where(lane_f == i0, ninf, el))
    ex = jnp.exp(v1 - v0)
    g0 = pg_top / (1.0 + ex)
    g1 = pg_top * ex / (1.0 + ex)
    e0 = i0 - N_GROUPS
    e1 = i1 - N_GROUPS

    hot0 = (lane_f == e0).astype(BF16)
    hot1 = (lane_f == e1).astype(BF16)
    rr = lax.broadcasted_iota(jnp.int32, (tm, tm), 0)
    cc = lax.broadcasted_iota(jnp.int32, (tm, tm), 1)
    before = (cc < rr).astype(BF16)
    cum = _dot(before, jnp.concatenate([hot0, hot1], axis=1))
    h0 = hot0.astype(F32)
    h1 = hot1.astype(F32)
    tot0 = jnp.sum(h0, axis=0, keepdims=True)
    tot1 = jnp.sum(h1, axis=0, keepdims=True)
    rank0 = jnp.sum(h0 * (run + cum[:, :LANES]), axis=-1, keepdims=True)
    rank1 = jnp.sum(h1 * (run + tot0 + cum[:, LANES:]), axis=-1, keepdims=True)
    ints = jnp.where(lane == 0, e0, jnp.where(lane == 1, e1, jnp.where(lane == 2, rank0, jnp.where(lane == 3, rank1, 0.0))))
    gates = jnp.where(lane == 0, g0, jnp.where(lane == 1, g1, 0.0))
    return ints.astype(jnp.int32), gates, run + tot0 + tot1


def _out_proj_kernel(x_ref, ya_ref, yb_ref, wo_ref, g_ref, wr_ref, br_ref,
                     x1_ref, h2_ref, ri_ref, rg_ref, cnt_ref, run_ref):
    @pl.when(pl.program_id(0) == 0)
    def _():
        run_ref[...] = jnp.zeros_like(run_ref)

    x1 = x_ref[...] + _dot(ya_ref[...], wo_ref[:RWKV_WIDTH, :]) + _dot(yb_ref[...], wo_ref[RWKV_WIDTH:, :])
    x1_ref[...] = x1
    h2 = _rms(x1, g_ref[...])
    h2_ref[...] = h2
    ints, gates, run = _route_tile(_dot_x3(h2, wr_ref[...]) + br_ref[...], run_ref[...])
    ri_ref[...] = ints
    rg_ref[...] = gates
    run_ref[...] = run
    cnt_ref[...] = run.astype(jnp.int32)


def _out_proj(x, y_rwkv, y_attn, wo, ln2, wr, br, tm):
    t, d = x.shape
    row = lambda i: (i, 0)
    fixed = lambda i: (0, 0)
    return pl.pallas_call(
        _out_proj_kernel,
        grid=(t // tm,),
        in_specs=[
            pl.BlockSpec((tm, d), row),
            pl.BlockSpec((tm, RWKV_WIDTH), row),
            pl.BlockSpec((tm, ATTN_WIDTH), row),
            pl.BlockSpec((d, d), fixed),
            pl.BlockSpec((1, d), fixed),
            pl.BlockSpec((d, LANES), fixed),
            pl.BlockSpec((1, LANES), fixed),
        ],
        out_specs=[pl.BlockSpec((tm, d), row), pl.BlockSpec((tm, d), row), pl.BlockSpec((tm, LANES), row),
                   pl.BlockSpec((tm, LANES), row), pl.BlockSpec((1, LANES), fixed)],
        out_shape=[jax.ShapeDtypeStruct((t, d), F32), jax.ShapeDtypeStruct((t, d), F32),
                   jax.ShapeDtypeStruct((t, LANES), jnp.int32), jax.ShapeDtypeStruct((t, LANES), F32),
                   jax.ShapeDtypeStruct((1, LANES), jnp.int32)],
        scratch_shapes=[pltpu.VMEM((1, LANES), F32)],
        compiler_params=_params("arbitrary"),
        name="out_proj_router",
    )(x, y_rwkv, y_attn, wo, ln2, wr, br)


def _for_rows(n, fn):
    full = lax.shift_right_logical(n, 3)

    def group(i, carry):
        for u in range(SUBLANES):
            fn(i, u)
        return carry

    def single(r, carry):
        fn(full, r)
        return carry

    lax.fori_loop(0, full, group, 0)
    lax.fori_loop(0, n - full * SUBLANES, single, 0)


def _moe_kernel(exp_ref, nvalid_ref, first_ref, wslot_ref, next_ref, start_ref, eid_ref, rank_ref,
                h_hbm, wgu_hbm, wdn_hbm, y_hbm, xbuf, obuf, wgu_buf, wdn_buf, slot_ref, gsem, ssem, wsem):
    b = pl.program_id(0)
    last = pl.num_programs(0) - 1
    cur = lax.rem(b, 2)
    nv = nvalid_ref[b]
    ws = wslot_ref[b]
    n_tok = h_hbm.shape[0]

    def weight_copies(e, buf):
        return (pltpu.make_async_copy(wgu_hbm.at[e], wgu_buf.at[buf], wsem.at[buf]),
                pltpu.make_async_copy(wdn_hbm.at[e], wdn_buf.at[buf], wsem.at[buf]))

    def gather_copy(blk, buf, i, u):
        a = slot_ref[blk * MOE_BLOCK + i * SUBLANES + u]
        tok = lax.shift_right_logical(a, 1)
        return pltpu.make_async_copy(h_hbm.at[pl.ds(tok, 1), :], xbuf.at[buf, i, pl.ds(u, 1), :], gsem.at[buf])

    def scatter_copy(blk, buf, i, u):
        a = slot_ref[blk * MOE_BLOCK + i * SUBLANES + u]
        dst = (a & 1) * n_tok + lax.shift_right_logical(a, 1)
        return pltpu.make_async_copy(obuf.at[buf, i, pl.ds(u, 1), :], y_hbm.at[pl.ds(dst, 1), :], ssem.at[buf])

    def gather_start(blk, buf):
        _for_rows(nvalid_ref[blk], lambda i, u: gather_copy(blk, buf, i, u).start())

    def gather_wait(blk, buf):
        _for_rows(nvalid_ref[blk], lambda i, u: gather_copy(blk, buf, i, u).wait())

    def scatter_start(blk, buf):
        _for_rows(nvalid_ref[blk], lambda i, u: scatter_copy(blk, buf, i, u).start())

    def scatter_wait(blk, buf):
        _for_rows(nvalid_ref[blk], lambda i, u: scatter_copy(blk, buf, i, u).wait())

    @pl.when(b == 0)
    def _():
        xbuf[...] = jnp.zeros_like(xbuf)

        def place(i, carry):
            for u in range(SUBLANES):
                a = i * SUBLANES + u
                slot_ref[start_ref[eid_ref[a]] + rank_ref[a]] = a
            return carry

        for c in weight_copies(exp_ref[0], 0):
            c.start()
        lax.fori_loop(0, eid_ref.shape[0] // SUBLANES, place, 0)
        gather_start(0, 0)

    @pl.when((first_ref[b] == 1) & (next_ref[b] >= 0))
    def _():
        for c in weight_copies(next_ref[b], 1 - ws):
            c.start()

    @pl.when(b < last)
    def _():
        gather_start(b + 1, 1 - cur)

    @pl.when(b >= 2)
    def _():
        scatter_wait(b - 2, cur)

    def experts(rows):
        tiles = rows // SUBLANES
        x = xbuf[cur, :tiles].reshape(rows, D_MODEL).astype(BF16)
        out = obuf.at[cur, :tiles]
        step = D_EXPERT // EXPERT_SPLIT
        for j in range(EXPERT_SPLIT):
            hg = _dot(x, wgu_buf[ws, :, j * step:(j + 1) * step].astype(BF16))
            hu = _dot(x, wgu_buf[ws, :, D_EXPERT + j * step:D_EXPERT + (j + 1) * step].astype(BF16))
            act = (hg * jax.nn.sigmoid(hg) * hu).astype(BF16)
            part = _dot(act, wdn_buf[ws, j * step:(j + 1) * step, :].astype(BF16))
            part = part.reshape(tiles, SUBLANES, D_MODEL)
            if j == 0:
                out[...] = part
            else:
                out[...] += part

    @pl.when(nv > 0)
    def _():
        gather_wait(b, cur)

        @pl.when(first_ref[b] == 1)
        def _():
            for c in weight_copies(exp_ref[b], ws):
                c.wait()

        @pl.when(nv > MOE_BLOCK // 2)
        def _():
            experts(MOE_BLOCK)

        @pl.when(nv <= MOE_BLOCK // 2)
        def _():
            experts(MOE_BLOCK // 2)

        scatter_start(b, cur)

    @pl.when(b == last)
    def _():
        @pl.when(b >= 1)
        def _():
            scatter_wait(b - 1, 1 - cur)

        scatter_wait(b, cur)


def _moe(h2, w_gu, w_dn, tables, pad_start, eid, rank):
    t, d = h2.shape
    n_blocks = tables[0].shape[0]
    hbm = pl.BlockSpec(memory_space=pl.ANY)
    return pl.pallas_call(
        _moe_kernel,
        grid_spec=pltpu.PrefetchScalarGridSpec(
            num_scalar_prefetch=len(tables) + 3,
            grid=(n_blocks,),
            in_specs=[hbm, hbm, hbm],
            out_specs=hbm,
            scratch_shapes=[
                pltpu.VMEM((2, MOE_BLOCK // SUBLANES, SUBLANES, d), F32),
                pltpu.VMEM((2, MOE_BLOCK // SUBLANES, SUBLANES, d), F32),
                pltpu.VMEM((2, d, 2 * D_EXPERT), F32),
                pltpu.VMEM((2, D_EXPERT, d), F32),
                pltpu.SMEM((n_blocks * MOE_BLOCK,), jnp.int32),
                pltpu.SemaphoreType.DMA((2,)),
                pltpu.SemaphoreType.DMA((2,)),
                pltpu.SemaphoreType.DMA((2,)),
            ],
        ),
        out_shape=jax.ShapeDtypeStruct((TOP_K_INNER * t, d), F32),
        compiler_params=_params("arbitrary"),
        name="moe_experts",
    )(*tables, pad_start, eid, rank, h2, w_gu, w_dn)


def _block_tables(counts, m_assign):
    padded = ((counts + MOE_BLOCK - 1) // MOE_BLOCK) * MOE_BLOCK
    pad_end = jnp.cumsum(padded)
    pad_start = pad_end - padded
    n_blocks = -(-m_assign // MOE_BLOCK) + N_EXPERTS
    blk_start = jnp.arange(n_blocks, dtype=jnp.int32) * MOE_BLOCK
    blk_exp = jnp.minimum(jnp.sum(pad_end[None, :] <= blk_start[:, None], axis=1), N_EXPERTS - 1).astype(jnp.int32)
    nvalid = jnp.clip(counts[blk_exp] - (blk_start - pad_start[blk_exp]), 0, MOE_BLOCK)
    nvalid = jnp.where(blk_start < pad_end[-1], nvalid, 0).astype(jnp.int32)
    prev_exp = jnp.concatenate([jnp.full((1,), -1, jnp.int32), blk_exp[:-1]])
    first = ((nvalid > 0) & (blk_exp != prev_exp)).astype(jnp.int32)
    wslot = jnp.maximum(jnp.cumsum(first) - 1, 0) % 2
    ids = jnp.arange(N_EXPERTS, dtype=jnp.int32)
    later = jnp.where((ids[None, :] > ids[:, None]) & (counts[None, :] > 0), ids[None, :], N_EXPERTS)
    next_of = jnp.min(later, axis=1)
    next_exp = jnp.where(next_of == N_EXPERTS, -1, next_of)[blk_exp]
    tables = tuple(a.astype(jnp.int32) for a in (blk_exp, nvalid, first, wslot, next_exp))
    return tables, pad_start.astype(jnp.int32)


def _final_kernel(x_ref, ya_ref, yb_ref, rg_ref, g_ref, o_ref):
    x = x_ref[...] + rg_ref[:, 0:1] * ya_ref[...] + rg_ref[:, 1:2] * yb_ref[...]
    o_ref[...] = _rms(x, g_ref[...])


def _final(x1, y2, gates, gain, tm):
    t, d = x1.shape
    nt = t // tm
    return pl.pallas_call(
        _final_kernel,
        grid=(nt,),
        in_specs=[
            pl.BlockSpec((tm, d), lambda i: (i, 0)),
            pl.BlockSpec((tm, d), lambda i: (i, 0)),
            pl.BlockSpec((tm, d), lambda i: (i + nt, 0)),
            pl.BlockSpec((tm, LANES), lambda i: (i, 0)),
            pl.BlockSpec((1, d), lambda i: (0, 0)),
        ],
        out_specs=pl.BlockSpec((tm, d), lambda i: (i, 0)),
        out_shape=jax.ShapeDtypeStruct((t, d), F32),
        compiler_params=_params("parallel"),
        name="final_norm",
    )(x1, y2, y2, gates, gain)


def _pad_cols(a, n):
    return jnp.pad(a, ((0, 0), (0, n - a.shape[1])))


def _rwkv_mix(p_rwkv, mu, w0, wdu, a0, wau, wgu, k_k, k_a, r_k, gn_w, gn_b):
    row = lambda a: a.reshape(1, -1)
    outs = _prep(p_rwkv, _pad_cols(row(mu), RWKV_COLS_PAD), row(w0), wdu, row(a0), wau,
                 jnp.pad(wgu, ((0, GATE_LORA_PAD - GATE_LORA), (0, 0))), row(k_k), row(k_a), row(r_k), tm=256)
    return _scan(*outs, row(gn_w), row(gn_b))


def _mix_and_route(x, y_rwkv, y_attn, w_out, ln2, w_rg, b_rg, w_re, b_re):
    row = lambda a: a.reshape(1, -1)
    wr = _pad_cols(jnp.concatenate([w_rg, w_re], axis=1), LANES)
    br = _pad_cols(row(jnp.concatenate([b_rg, b_re])), LANES)
    return _out_proj(x, y_rwkv, y_attn, w_out.astype(BF16), row(ln2), wr, br, tm=256)


def _experts(h2, route_ints, counts, w_gu, w_dn):
    t = h2.shape[0]
    tables, pad_start = _block_tables(counts[0, :N_EXPERTS], t * TOP_K_INNER)
    eid = route_ints[:, 0:TOP_K_INNER].reshape(-1)
    rank = route_ints[:, TOP_K_INNER:2 * TOP_K_INNER].reshape(-1)
    return _moe(h2, w_gu, w_dn, tables, pad_start, eid, rank)


def _layer(x, positions, ln1, w_in, mu, w0, wdu, a0, wau, wgu, k_k, k_a, r_k, gn_w, gn_b, sinks, w_out, ln2,
           w_rg, b_rg, w_re, b_re, w_gu, w_dn, ln_f):
    t = x.shape[0]
    row = lambda a: a.reshape(1, -1)
    w_rwkv = _pad_cols(w_in[:, :RWKV_COLS], RWKV_COLS_PAD).astype(BF16)
    w_attn = w_in[:, RWKV_COLS:].astype(BF16)
    p_rwkv = _in_proj(x, row(ln1), w_rwkv, tm=min(1024, t), tn=RWKV_COLS_PAD // 3, out_dtype=F32)
    qkv = _in_proj(x, row(ln1), w_attn, tm=min(1024, t), tn=ATTN_COLS, out_dtype=BF16)

    y_rwkv = _rwkv_mix(p_rwkv, mu, w0, wdu, a0, wau, wgu, k_k, k_a, r_k, gn_w, gn_b)

    inv_freq = ROPE_THETA ** (-jnp.arange(0, HEAD_DIM, 2, dtype=F32) / HEAD_DIM)
    freq = jnp.tile(inv_freq, 2 * LANES // HEAD_DIM).reshape(1, LANES)
    y_attn = _attn(qkv, positions.reshape(t, 1), freq, sinks)

    x1, h2, route_ints, gates, counts = _mix_and_route(x, y_rwkv, y_attn, w_out, ln2, w_rg, b_rg, w_re, b_re)
    y2 = _experts(h2, route_ints, counts, w_gu, w_dn)
    return _final(x1, y2, gates, row(ln_f), tm=256)


def kernel(x, positions, ln1, w_in, mu_shift, w_decay0, w_decay_up, a0, w_a_up, w_g_up, k_k, k_a, r_k, gn_w, gn_b,
           sinks, w_out, ln2, w_router_group, b_router_group, w_router_expert, b_router_expert, w_expert_gu,
           w_expert_down, ln_f):
    assert ln1.shape[0] == 1, "one trunk layer"
    outs = [
        _layer(x[i], positions[i], ln1[0], w_in[0], mu_shift[0], w_decay0[0], w_decay_up[0], a0[0], w_a_up[0],
               w_g_up[0], k_k[0], k_a[0], r_k[0], gn_w[0], gn_b[0], sinks[0], w_out[0], ln2[0], w_router_group[0],
               b_router_group[0], w_router_expert[0], b_router_expert[0], w_expert_gu[0], w_expert_down[0], ln_f)
        for i in range(x.shape[0])
    ]
    return jnp.stack(outs, axis=0)
```

```python
import math

import jax
import jax.numpy as jnp
from jax import lax
from jax.experimental import pallas as pl
from jax.experimental.pallas import tpu as pltpu

D_MODEL = 2048
HEAD_DIM = 64
RWKV_WIDTH = 1024
ATTN_WIDTH = 1024
ATTN_KV_HEADS = 4
ATTN_GROUP = 4
ATTN_KV_WIDTH = 256
WINDOW = 128
ROPE_THETA = 10000.0
DECAY_LORA = 64
AAA_LORA = 64
GATE_LORA = 160
GATE_LORA_PAD = 256
RWKV_COLS = 3 * RWKV_WIDTH + DECAY_LORA + AAA_LORA + GATE_LORA
RWKV_COLS_PAD = 3 * RWKV_WIDTH + DECAY_LORA + AAA_LORA + GATE_LORA_PAD
ATTN_COLS = ATTN_WIDTH + 2 * ATTN_KV_WIDTH
N_GROUPS = 8
EXPERTS_PER_GROUP = 8
N_EXPERTS = 64
TOP_K_INNER = 2
D_EXPERT = 768
MOE_BLOCK = 256
RMS_EPS = 1e-6
RWKV_GN_EPS = 64e-5

LANES = 128
SUBLANES = 8
CHUNK = 64
SCAN_CHUNKS = 8
EXPERT_SPLIT = 3
VMEM_LIMIT = 56 * 1024 * 1024

F32 = jnp.float32
BF16 = jnp.bfloat16


def _dot(a, b):
    return jnp.dot(a, b, preferred_element_type=F32)


def _split(x, parts):
    out = []
    for _ in range(parts - 1):
        hi = x.astype(BF16)
        out.append(hi)
        x = x - hi.astype(F32)
    out.append(x.astype(BF16))
    return out


def _dot_x3(a, b):
    ah, al = _split(a, 2)
    bh, bl = _split(b, 2)
    return _dot(ah, bh) + (_dot(ah, bl) + _dot(al, bh))


def _dot_sel(a, sel):
    ah, al = _split(a, 2)
    return _dot(ah, sel) + _dot(al, sel)


def _dot_nt(a, b):
    return lax.dot_general(a, b, (((1,), (1,)), ((), ())), preferred_element_type=F32)


def _dot_tn(a, b):
    return lax.dot_general(a, b, (((0,), (0,)), ((), ())), preferred_element_type=F32)


def _params(*sem):
    return pltpu.CompilerParams(dimension_semantics=sem, vmem_limit_bytes=VMEM_LIMIT)


def _head_sum_matrix():
    r = lax.broadcasted_iota(jnp.int32, (LANES, LANES), 0) // HEAD_DIM
    c = lax.broadcasted_iota(jnp.int32, (LANES, LANES), 1) // HEAD_DIM
    return (r == c).astype(BF16)


def _head_sums(x, bd):
    parts = [_dot_sel(x[:, j * LANES:(j + 1) * LANES], bd) for j in range(x.shape[1] // LANES)]
    return jnp.concatenate(parts, axis=1)


def _rms(x, gain):
    return x * lax.rsqrt(jnp.mean(x * x, axis=-1, keepdims=True) + RMS_EPS) * gain


def _split_w_in_kernel(w_ref, rwkv_ref, attn_ref):
    w = w_ref[...]
    rwkv_ref[...] = w[:, :RWKV_COLS_PAD].astype(BF16)
    attn_ref[...] = w[:, RWKV_COLS:].astype(BF16)


def _split_w_in(w_in, tk):
    d, n = w_in.shape
    return pl.pallas_call(
        _split_w_in_kernel,
        grid=(d // tk,),
        in_specs=[pl.BlockSpec((tk, n), lambda i: (i, 0))],
        out_specs=[pl.BlockSpec((tk, RWKV_COLS_PAD), lambda i: (i, 0)), pl.BlockSpec((tk, ATTN_COLS), lambda i: (i, 0))],
        out_shape=[jax.ShapeDtypeStruct((d, RWKV_COLS_PAD), BF16), jax.ShapeDtypeStruct((d, ATTN_COLS), BF16)],
        compiler_params=_params("parallel"),
        name="split_w_in",
    )(w_in)


def _in_proj_kernel(x_ref, g_ref, w_ref, o_ref, h_ref):
    @pl.when(pl.program_id(1) == 0)
    def _():
        h_ref[...] = _rms(x_ref[...], g_ref[...]).astype(BF16)

    o_ref[...] = _dot(h_ref[...], w_ref[...]).astype(o_ref.dtype)


def _in_proj(x, gain, w, tm, tn, out_dtype):
    t, d = x.shape
    n = w.shape[1]
    return pl.pallas_call(
        _in_proj_kernel,
        grid=(t // tm, n // tn),
        in_specs=[
            pl.BlockSpec((tm, d), lambda i, j: (i, 0)),
            pl.BlockSpec((1, d), lambda i, j: (0, 0)),
            pl.BlockSpec((d, tn), lambda i, j: (0, j)),
        ],
        out_specs=pl.BlockSpec((tm, tn), lambda i, j: (i, j)),
        out_shape=jax.ShapeDtypeStruct((t, n), out_dtype),
        scratch_shapes=[pltpu.VMEM((tm, d), BF16)],
        compiler_params=_params("parallel", "arbitrary"),
        name="in_proj",
    )(x, gain, w)


def _prep_kernel(p_ref, prev_ref, mu_ref, w0_ref, wdu_ref, a0_ref, wau_ref, wgu_ref, kk_ref, ka_ref, rk_ref,
                 at_ref, rt_ref, bt_ref, kt_ref, bh_ref, kh_ref, v_ref, gam_ref, g_ref, bv_ref):
    i = pl.program_id(0)
    tm = p_ref.shape[0]
    w = RWKV_WIDTH
    p = p_ref[...]
    prev_row = jnp.where(i == 0, 0.0, prev_ref[7:8, :])
    row = lax.broadcasted_iota(jnp.int32, p.shape, 0)
    shifted = jnp.where(row == 0, prev_row, pltpu.roll(p, 1, 0))
    ps = p + (shifted - p) * mu_ref[...]

    r = ps[:, 0:w]
    k = ps[:, w:2 * w]
    v = ps[:, 2 * w:3 * w]
    wd = ps[:, 3 * w:3 * w + DECAY_LORA]
    ad = ps[:, 3 * w + DECAY_LORA:3 * w + DECAY_LORA + AAA_LORA]
    gd = ps[:, 3 * w + DECAY_LORA + AAA_LORA:]

    z = w0_ref[...] + _dot_x3(jnp.tanh(wd), wdu_ref[...])
    logw = -math.exp(-0.5) * jax.nn.sigmoid(z)
    alr = jax.nn.sigmoid(a0_ref[...] + _dot_x3(ad, wau_ref[...]))
    g_ref[...] = _dot_x3(jax.nn.sigmoid(gd), wgu_ref[...])

    bd = _head_sum_matrix()
    kk = k * kk_ref[...]
    kk = kk / jnp.maximum(jnp.sqrt(_head_sums(kk * kk, bd)), 1e-12)
    kmod = k * (1.0 + (alr - 1.0) * ka_ref[...])
    b = kk * alr
    bv_ref[...] = _head_sums(r * kmod * rk_ref[...], bd) * v
    v_ref[...] = v.astype(BF16)

    rr = lax.broadcasted_iota(jnp.int32, (tm, tm), 0)
    cc = lax.broadcasted_iota(jnp.int32, (tm, tm), 1)
    tri = ((cc <= rr) & (cc // CHUNK == rr // CHUNK)).astype(BF16)
    l1, l2, l3 = _split(logw, 3)
    cum = _dot(tri, l1) + (_dot(tri, l2) + _dot(tri, l3))
    tot_rows = []
    for c in range(tm // CHUNK):
        last = cum[c * CHUNK + CHUNK - 1:c * CHUNK + CHUNK, :]
        gam_ref[c] = jnp.exp(last)
        tot_rows.append(jnp.broadcast_to(last, (CHUNK, w)))
    tot = jnp.concatenate(tot_rows, axis=0)

    e_neg = jnp.exp(-cum)
    e_rem = jnp.exp(tot - cum)
    at_ref[...] = (-kk * jnp.exp(cum - logw)).astype(BF16)
    rt_ref[...] = (r * jnp.exp(cum)).astype(BF16)
    bt_ref[...] = (b * e_neg).astype(BF16)
    kt_ref[...] = (kmod * e_neg).astype(BF16)
    bh_ref[...] = (b * e_rem).astype(BF16)
    kh_ref[...] = (kmod * e_rem).astype(BF16)


def _prep(p_rwkv, mu, w0, wdu, a0, wau, wgu, k_k, k_a, r_k, tm):
    t = p_rwkv.shape[0]
    w = RWKV_WIDTH
    cp = RWKV_COLS_PAD
    nc = tm // CHUNK
    row = lambda i: (i, 0)
    fixed = lambda i: (0, 0)
    vec = pl.BlockSpec((1, w), fixed)
    big_bf = jax.ShapeDtypeStruct((t, w), BF16)
    big_f32 = jax.ShapeDtypeStruct((t, w), F32)
    out_tile = pl.BlockSpec((tm, w), row)
    return pl.pallas_call(
        _prep_kernel,
        grid=(t // tm,),
        in_specs=[
            pl.BlockSpec((tm, cp), row),
            pl.BlockSpec((8, cp), lambda i: (jnp.maximum(i * (tm // 8) - 1, 0), 0)),
            pl.BlockSpec((1, cp), fixed),
            vec, pl.BlockSpec((DECAY_LORA, w), fixed),
            vec, pl.BlockSpec((AAA_LORA, w), fixed),
            pl.BlockSpec((GATE_LORA_PAD, w), fixed),
            vec, vec, vec,
        ],
        out_specs=[out_tile] * 7 + [pl.BlockSpec((nc, 1, w), lambda i: (i, 0, 0)), out_tile, out_tile],
        out_shape=[big_bf] * 7 + [jax.ShapeDtypeStruct((t // CHUNK, 1, w), F32), big_f32, big_f32],
        compiler_params=_params("parallel"),
        name="rwkv_prep",
    )(p_rwkv, p_rwkv, mu, w0, wdu, a0, wau, wgu, k_k, k_a, r_k)


def _chunk_factors(nchunks, own, strict, incl, eye, at_ref, rt_ref, bt_ref, kt_ref, bh_ref, kh_ref, v_ref, gam_ref,
                   tick):
    js = range(nchunks)
    n = 2 * CHUNK

    def stack(ref):
        xs = [ref[pl.ds(j * CHUNK, CHUNK), :] for j in js]
        return [jnp.where(own, jnp.concatenate([x, x], axis=0), jnp.zeros((), x.dtype)) for x in xs]

    a_s, r_s, b_s, k_s = stack(at_ref), stack(rt_ref), stack(bt_ref), stack(kt_ref)
    bh_s, kh_s, v_s = stack(bh_ref), stack(kh_ref), stack(v_ref)

    prod = [_dot_nt(jnp.concatenate([a_s[j], r_s[j]], axis=0), jnp.concatenate([b_s[j], k_s[j]], axis=0)) for j in js]
    a_ak = [jnp.where(strict, prod[j][:n, n:], 0.0).astype(BF16) for j in js]
    a_rb = [jnp.where(incl, prod[j][n:, :n], 0.0).astype(BF16) for j in js]
    a_rk = [jnp.where(incl, prod[j][n:, n:], 0.0).astype(BF16) for j in js]

    tick()
    pq = [jnp.concatenate([a_s[j].astype(F32), _dot(a_ak[j], v_s[j])], axis=1) for j in js]
    nk_b = [jnp.where(strict, prod[j][:n, :n], 0.0).astype(BF16) for j in js]
    span = 1
    while True:
        tick()
        pq = [pq[j] + _dot(nk_b[j], pq[j].astype(BF16)) for j in js]
        span *= 2
        if span >= CHUNK:
            break
        tick()
        nk_b = [_dot(nk_b[j], nk_b[j]).astype(BF16) for j in js]
    pq_b = [pq[j].astype(BF16) for j in js]

    zero = jnp.zeros((n, LANES), BF16)
    rhs = [jnp.concatenate([pq_b[j], jnp.concatenate([zero, v_s[j]], axis=1)], axis=0) for j in js]
    mn = [_dot_tn(jnp.concatenate([bh_s[j], kh_s[j]], axis=0), rhs[j]) for j in js]
    yy = [_dot(jnp.concatenate([a_rb[j], a_rk[j]], axis=1), rhs[j]) for j in js]
    out = []
    for j in js:
        m_mat = mn[j][:, :LANES] + jnp.where(eye, jnp.broadcast_to(gam_ref[j], (LANES, LANES)), 0.0)
        n_mat = mn[j][:, LANES:]
        y_c = r_s[j].astype(F32) + yy[j][:, :LANES]
        y_n = yy[j][:, LANES:]
        out.append((m_mat, n_mat, y_c, y_n))
    return out


def _scan_kernel(at_ref, rt_ref, bt_ref, kt_ref, bh_ref, kh_ref, v_ref, gam_ref, g_ref, bv_ref, gnw_ref, gnb_ref,
                 y_ref, h_ref, f_ref):
    @pl.when(pl.program_id(1) == 0)
    def _():
        h_ref[...] = jnp.zeros_like(h_ref)
        f_ref[...] = jnp.zeros_like(f_ref)

    chain = {"h": h_ref[...], "ys": []}

    def tick():
        j = len(chain["ys"])
        if j < SCAN_CHUNKS:
            h = chain["h"]
            y_st = _dot(f_ref[j, 2].astype(BF16), h.astype(BF16)) + f_ref[j, 3]
            chain["h"] = _dot_x3(f_ref[j, 0], h) + f_ref[j, 1]
            chain["ys"].append(y_st[:CHUNK] + y_st[CHUNK:])

    n = 2 * CHUNK
    own = (lax.broadcasted_iota(jnp.int32, (n, LANES), 1) // HEAD_DIM
           == lax.broadcasted_iota(jnp.int32, (n, LANES), 0) // CHUNK)
    ti = lax.broadcasted_iota(jnp.int32, (n, n), 0)
    si = lax.broadcasted_iota(jnp.int32, (n, n), 1)
    same = (ti // CHUNK) == (si // CHUNK)
    strict = same & (si < ti)
    incl = same & (si <= ti)
    eye = ti == si

    factors = _chunk_factors(SCAN_CHUNKS, own, strict, incl, eye, at_ref, rt_ref, bt_ref, kt_ref, bh_ref, kh_ref,
                             v_ref, gam_ref, tick)
    while len(chain["ys"]) < SCAN_CHUNKS:
        tick()
    h_ref[...] = chain["h"]
    y = jnp.concatenate(chain["ys"], axis=0)

    low = lax.broadcasted_iota(jnp.int32, y.shape, 1) < HEAD_DIM

    def head_mean(x):
        s_low = jnp.sum(jnp.where(low, x, 0.0), axis=-1, keepdims=True)
        s_all = jnp.sum(x, axis=-1, keepdims=True)
        return jnp.where(low, s_low, s_all - s_low) * (1.0 / HEAD_DIM)

    yc = y - head_mean(y)
    var = head_mean(yc * yc)
    yn = yc * lax.rsqrt(var + RWKV_GN_EPS) * gnw_ref[...] + gnb_ref[...]
    y_ref[...] = ((yn + bv_ref[...]) * g_ref[...]).astype(BF16)

    for j, mats in enumerate(factors):
        for i, mat in enumerate(mats):
            f_ref[j, i] = mat


def _scan(at, rt, bt, kt, bh, kh, v, gam, g, bv, gn_w, gn_b):
    t = at.shape[0]
    rows = SCAN_CHUNKS * CHUNK
    steps = t // rows
    tile = pl.BlockSpec((rows, LANES), lambda p, s: (jnp.minimum(s, steps - 1), p))
    lagged = pl.BlockSpec((rows, LANES), lambda p, s: (jnp.maximum(s - 1, 0), p))
    vec = pl.BlockSpec((1, LANES), lambda p, s: (0, p))
    gam_spec = pl.BlockSpec((SCAN_CHUNKS, 1, LANES), lambda p, s: (jnp.minimum(s, steps - 1), 0, p))
    return pl.pallas_call(
        _scan_kernel,
        grid=(RWKV_WIDTH // LANES, steps + 1),
        in_specs=[tile] * 7 + [gam_spec, lagged, lagged, vec, vec],
        out_specs=lagged,
        out_shape=jax.ShapeDtypeStruct((t, RWKV_WIDTH), BF16),
        scratch_shapes=[pltpu.VMEM((LANES, LANES), F32), pltpu.VMEM((SCAN_CHUNKS, 4, 2 * CHUNK, LANES), F32)],
        compiler_params=_params("parallel", "arbitrary"),
        name="rwkv_scan",
    )(at, rt, bt, kt, bh, kh, v, gam, g, bv, gn_w, gn_b)


def _attn_kernel(sink_ref, qkv_ref, pos_ref, freq_ref, o_ref, kprev_ref, vprev_ref):
    nb = pl.program_id(0)

    @pl.when(nb == 0)
    def _():
        kprev_ref[...] = jnp.zeros_like(kprev_ref)
        vprev_ref[...] = jnp.zeros_like(vprev_ref)

    ang = pos_ref[...].astype(F32) * freq_ref[...]
    cos = jnp.cos(ang)
    lane = lax.broadcasted_iota(jnp.int32, ang.shape, 1)
    sin = jnp.where(lane % HEAD_DIM < HEAD_DIM // 2, -1.0, 1.0) * jnp.sin(ang)
    first_half = lane % HEAD_DIM < HEAD_DIM // 2

    def rope(x):
        swapped = jnp.where(first_half, pltpu.roll(x, LANES - HEAD_DIM // 2, 1), pltpu.roll(x, HEAD_DIM // 2, 1))
        return x * cos + swapped * sin

    scale = HEAD_DIM ** -0.5
    qi = lax.broadcasted_iota(jnp.int32, (WINDOW, 2 * WINDOW), 0) + WINDOW
    ki = lax.broadcasted_iota(jnp.int32, (WINDOW, 2 * WINDOW), 1)
    rel = qi - ki
    mask = (rel >= 0) & (rel < WINDOW) & ((nb > 0) | (ki >= WINDOW))

    k_cur = jnp.concatenate([rope(qkv_ref[:, ATTN_WIDTH + j * LANES:ATTN_WIDTH + (j + 1) * LANES].astype(F32))
                             for j in range(ATTN_KV_WIDTH // LANES)], axis=1)
    v_cur = qkv_ref[:, ATTN_WIDTH + ATTN_KV_WIDTH:].astype(F32)
    k_band = jnp.concatenate([kprev_ref[...], k_cur], axis=0)
    v_band = jnp.concatenate([vprev_ref[...], v_cur], axis=0)
    k_heads = [k_band[:, g * HEAD_DIM:(g + 1) * HEAD_DIM].astype(BF16) for g in range(ATTN_KV_HEADS)]
    v_heads = [v_band[:, g * HEAD_DIM:(g + 1) * HEAD_DIM].astype(BF16) for g in range(ATTN_KV_HEADS)]

    heads = range(ATTN_WIDTH // HEAD_DIM)
    per_group = LANES // HEAD_DIM
    q_groups = [rope(qkv_ref[:, j * LANES:(j + 1) * LANES].astype(F32)) * scale for j in range(ATTN_WIDTH // LANES)]
    q = [q_groups[h // per_group][:, (h % per_group) * HEAD_DIM:(h % per_group + 1) * HEAD_DIM].astype(BF16)
         for h in heads]
    s = [jnp.where(mask, _dot_nt(q[h], k_heads[h // ATTN_GROUP]), -jnp.inf) for h in heads]
    probs = []
    for h in heads:
        sink = sink_ref[h]
        m = jnp.maximum(jnp.max(s[h], axis=-1, keepdims=True), sink)
        pexp = jnp.exp(s[h] - m)
        denom = jnp.sum(pexp, axis=-1, keepdims=True) + jnp.exp(sink - m)
        probs.append((pexp * (1.0 / denom)).astype(BF16))
    o = [_dot(probs[h], v_heads[h // ATTN_GROUP]) for h in heads]
    for j in range(ATTN_WIDTH // LANES):
        o_ref[:, j * LANES:(j + 1) * LANES] = jnp.concatenate(o[j * per_group:(j + 1) * per_group],
                                                               axis=1).astype(BF16)

    kprev_ref[...] = k_cur
    vprev_ref[...] = v_cur


def _attn(qkv, pos_col, freq, sinks):
    t = qkv.shape[0]
    return pl.pallas_call(
        _attn_kernel,
        grid_spec=pltpu.PrefetchScalarGridSpec(
            num_scalar_prefetch=1,
            grid=(t // WINDOW,),
            in_specs=[
                pl.BlockSpec((WINDOW, ATTN_COLS), lambda i, s: (i, 0)),
                pl.BlockSpec((WINDOW, 1), lambda i, s: (i, 0)),
                pl.BlockSpec((1, LANES), lambda i, s: (0, 0)),
            ],
            out_specs=pl.BlockSpec((WINDOW, ATTN_WIDTH), lambda i, s: (i, 0)),
            scratch_shapes=[pltpu.VMEM((WINDOW, ATTN_KV_WIDTH), F32), pltpu.VMEM((WINDOW, ATTN_KV_WIDTH), F32)],
        ),
        out_shape=jax.ShapeDtypeStruct((t, ATTN_WIDTH), BF16),
        compiler_params=_params("arbitrary"),
        name="swa_attn",
    )(sinks, qkv, pos_col, freq)


def _route_stages(lg, run, out):
    tm = lg.shape[0]
    lane = lax.broadcasted_iota(jnp.int32, lg.shape, 1)
    lane_f = lane.astype(F32)
    ninf = -jnp.inf

    def top(vals):
        best = jnp.max(vals, axis=-1, keepdims=True)
        idx = jnp.min(jnp.where(vals == best, lane_f, float(LANES)), axis=-1, keepdims=True)
        return best, idx

    gl = jnp.where(lane < N_GROUPS, lg, ninf)
    gmax, gidx = top(gl)
    pg_top = 1.0 / jnp.sum(jnp.exp(gl - gmax), axis=-1, keepdims=True)
    yield
    first = N_GROUPS + EXPERTS_PER_GROUP * gidx
    el = jnp.where((lane_f >= first) & (lane_f < first + EXPERTS_PER_GROUP), lg, ninf)
    v0, i0 = top(el)
    yield
    v1, i1 = top(jnp.where(lane_f == i0, ninf, el))
    ex = jnp.exp(v1 - v0)
    g0 = pg_top / (1.0 + ex)
    g1 = pg_top * ex / (1.0 + ex)
    e0 = i0 - N_GROUPS
    e1 = i1 - N_GROUPS
    yield

    hot0 = (lane_f == e0).astype(BF16)
    hot1 = (lane_f == e1).astype(BF16)
    rr = lax.broadcasted_iota(jnp.int32, (tm, tm), 0)
    cc = lax.broadcasted_iota(jnp.int32, (tm, tm), 1)
    before = (cc < rr).astype(BF16)
    cum = _dot(before, jnp.concatenate([hot0, hot1], axis=1))
    h0 = hot0.astype(F32)
    h1 = hot1.astype(F32)
    tot0 = jnp.sum(h0, axis=0, keepdims=True)
    tot1 = jnp.sum(h1, axis=0, keepdims=True)
    rank0 = jnp.sum(h0 * (run + cum[:, :LANES]), axis=-1, keepdims=True)
    rank1 = jnp.sum(h1 * (run + tot0 + cum[:, LANES:]), axis=-1, keepdims=True)
    ints = jnp.where(lane == 0, e0, jnp.where(lane == 1, e1, jnp.where(lane == 2, rank0, jnp.where(lane == 3, rank1, 0.0))))
    out["ints"] = ints.astype(jnp.int32)
    out["gates"] = jnp.where(lane == 0, g0, jnp.where(lane == 1, g1, 0.0))
    out["run"] = run + tot0 + tot1
    yield


OUT_PROJ_SPLIT = 4


def _out_proj_kernel(x_ref, ya_ref, yb_ref, wo_ref, g_ref, wr_ref, br_ref,
                     x1_ref, h2_ref, ri_ref, rg_ref, cnt_ref, run_ref, lg_ref):
    s = pl.program_id(0)

    @pl.when(s == 0)
    def _():
        run_ref[...] = jnp.zeros_like(run_ref)
        lg_ref[...] = jnp.zeros_like(lg_ref)

    routed = {}
    stages = _route_stages(lg_ref[...], run_ref[...], routed)
    ya = ya_ref[...]
    yb = yb_ref[...]
    width = D_MODEL // OUT_PROJ_SPLIT
    ssq = 0.0
    for c in range(OUT_PROJ_SPLIT):
        cols = slice(c * width, (c + 1) * width)
        x1 = x_ref[:, cols] + _dot(ya, wo_ref[:RWKV_WIDTH, cols]) + _dot(yb, wo_ref[RWKV_WIDTH:, cols])
        x1_ref[:, cols] = x1
        ssq = ssq + jnp.sum(x1 * x1, axis=-1, keepdims=True)
        next(stages)

    ri_ref[...] = routed["ints"]
    rg_ref[...] = routed["gates"]
    run = jnp.where(s > 0, routed["run"], run_ref[...])
    run_ref[...] = run
    cnt_ref[...] = run.astype(jnp.int32)

    h2 = x1_ref[...] * lax.rsqrt(ssq * (1.0 / D_MODEL) + RMS_EPS) * g_ref[...]
    h2_ref[...] = h2
    lg_ref[...] = _dot_x3(h2, wr_ref[...]) + br_ref[...]


def _out_proj(x, y_rwkv, y_attn, wo, ln2, wr, br, tm):
    t, d = x.shape
    steps = t // tm
    row = lambda i: (jnp.minimum(i, steps - 1), 0)
    lagged = lambda i: (jnp.maximum(i - 1, 0), 0)
    fixed = lambda i: (0, 0)
    return pl.pallas_call(
        _out_proj_kernel,
        grid=(steps + 1,),
        in_specs=[
            pl.BlockSpec((tm, d), row),
            pl.BlockSpec((tm, RWKV_WIDTH), row),
            pl.BlockSpec((tm, ATTN_WIDTH), row),
            pl.BlockSpec((d, d), fixed),
            pl.BlockSpec((1, d), fixed),
            pl.BlockSpec((d, LANES), fixed),
            pl.BlockSpec((1, LANES), fixed),
        ],
        out_specs=[pl.BlockSpec((tm, d), row), pl.BlockSpec((tm, d), row), pl.BlockSpec((tm, LANES), lagged),
                   pl.BlockSpec((tm, LANES), lagged), pl.BlockSpec((1, LANES), fixed)],
        out_shape=[jax.ShapeDtypeStruct((t, d), F32), jax.ShapeDtypeStruct((t, d), F32),
                   jax.ShapeDtypeStruct((t, LANES), jnp.int32), jax.ShapeDtypeStruct((t, LANES), F32),
                   jax.ShapeDtypeStruct((1, LANES), jnp.int32)],
        scratch_shapes=[pltpu.VMEM((1, LANES), F32), pltpu.VMEM((tm, LANES), F32)],
        compiler_params=_params("arbitrary"),
        name="out_proj_router",
    )(x, y_rwkv, y_attn, wo, ln2, wr, br)


def _for_rows(n, fn):
    full = lax.shift_right_logical(n, 3)

    def group(i, carry):
        for u in range(SUBLANES):
            fn(i, u)
        return carry

    def single(r, carry):
        fn(full, r)
        return carry

    lax.fori_loop(0, full, group, 0)
    lax.fori_loop(0, n - full * SUBLANES, single, 0)


def _moe_kernel(exp_ref, nvalid_ref, first_ref, wslot_ref, next_ref, start_ref, eid_ref, rank_ref,
                h_hbm, wgu_hbm, wdn_hbm, y_hbm, xbuf, obuf, wgu_buf, wdn_buf, slot_ref, gsem, ssem, wsem):
    b = pl.program_id(0)
    last = pl.num_programs(0) - 1
    cur = lax.rem(b, 2)
    nv = nvalid_ref[b]
    ws = wslot_ref[b]
    n_tok = h_hbm.shape[0]

    def weight_copies(e, buf):
        return (pltpu.make_async_copy(wgu_hbm.at[e], wgu_buf.at[buf], wsem.at[buf]),
                pltpu.make_async_copy(wdn_hbm.at[e], wdn_buf.at[buf], wsem.at[buf]))

    def gather_copy(blk, buf, i, u):
        a = slot_ref[blk * MOE_BLOCK + i * SUBLANES + u]
        tok = lax.shift_right_logical(a, 1)
        return pltpu.make_async_copy(h_hbm.at[pl.ds(tok, 1), :], xbuf.at[buf, i, pl.ds(u, 1), :], gsem.at[buf])

    def scatter_copy(blk, buf, i, u):
        a = slot_ref[blk * MOE_BLOCK + i * SUBLANES + u]
        dst = (a & 1) * n_tok + lax.shift_right_logical(a, 1)
        return pltpu.make_async_copy(obuf.at[buf, i, pl.ds(u, 1), :], y_hbm.at[pl.ds(dst, 1), :], ssem.at[buf])

    def gather_start(blk, buf):
        _for_rows(nvalid_ref[blk], lambda i, u: gather_copy(blk, buf, i, u).start())

    def gather_wait(blk, buf):
        _for_rows(nvalid_ref[blk], lambda i, u: gather_copy(blk, buf, i, u).wait())

    def scatter_start(blk, buf):
        _for_rows(nvalid_ref[blk], lambda i, u: scatter_copy(blk, buf, i, u).start())

    def scatter_wait(blk, buf):
        _for_rows(nvalid_ref[blk], lambda i, u: scatter_copy(blk, buf, i, u).wait())

    @pl.when(b == 0)
    def _():
        xbuf[...] = jnp.zeros_like(xbuf)

        def place(i, carry):
            for u in range(SUBLANES):
                a = i * SUBLANES + u
                slot_ref[start_ref[eid_ref[a]] + rank_ref[a]] = a
            return carry

        for c in weight_copies(exp_ref[0], 0):
            c.start()
        lax.fori_loop(0, eid_ref.shape[0] // SUBLANES, place, 0)
        gather_start(0, 0)

    @pl.when((first_ref[b] == 1) & (next_ref[b] >= 0))
    def _():
        for c in weight_copies(next_ref[b], 1 - ws):
            c.start()

    @pl.when(b < last)
    def _():
        gather_start(b + 1, 1 - cur)

    @pl.when(b >= 2)
    def _():
        scatter_wait(b - 2, cur)

    def experts(rows):
        tiles = rows // SUBLANES
        x = xbuf[cur, :tiles].reshape(rows, D_MODEL).astype(BF16)
        out = obuf.at[cur, :tiles]
        step = D_EXPERT // EXPERT_SPLIT
        for j in range(EXPERT_SPLIT):
            hg = _dot(x, wgu_buf[ws, :, j * step:(j + 1) * step].astype(BF16))
            hu = _dot(x, wgu_buf[ws, :, D_EXPERT + j * step:D_EXPERT + (j + 1) * step].astype(BF16))
            act = (hg * jax.nn.sigmoid(hg) * hu).astype(BF16)
            part = _dot(act, wdn_buf[ws, j * step:(j + 1) * step, :].astype(BF16))
            part = part.reshape(tiles, SUBLANES, D_MODEL)
            if j == 0:
                out[...] = part
            else:
                out[...] += part

    @pl.when(nv > 0)
    def _():
        gather_wait(b, cur)

        @pl.when(first_ref[b] == 1)
        def _():
            for c in weight_copies(exp_ref[b], ws):
                c.wait()

        @pl.when(nv > MOE_BLOCK // 2)
        def _():
            experts(MOE_BLOCK)

        @pl.when(nv <= MOE_BLOCK // 2)
        def _():
            experts(MOE_BLOCK // 2)

        scatter_start(b, cur)

    @pl.when(b == last)
    def _():
        @pl.when(b >= 1)
        def _():
            scatter_wait(b - 1, 1 - cur)

        scatter_wait(b, cur)


def _moe(h2, w_gu, w_dn, tables, pad_start, eid, rank):
    t, d = h2.shape
    n_blocks = tables[0].shape[0]
    hbm = pl.BlockSpec(memory_space=pl.ANY)
    return pl.pallas_call(
        _moe_kernel,
        grid_spec=pltpu.PrefetchScalarGridSpec(
            num_scalar_prefetch=len(tables) + 3,
            grid=(n_blocks,),
            in_specs=[hbm, hbm, hbm],
            out_specs=hbm,
            scratch_shapes=[
                pltpu.VMEM((2, MOE_BLOCK // SUBLANES, SUBLANES, d), F32),
                pltpu.VMEM((2, MOE_BLOCK // SUBLANES, SUBLANES, d), F32),
                pltpu.VMEM((2, d, 2 * D_EXPERT), F32),
                pltpu.VMEM((2, D_EXPERT, d), F32),
                pltpu.SMEM((n_blocks * MOE_BLOCK,), jnp.int32),
                pltpu.SemaphoreType.DMA((2,)),
                pltpu.SemaphoreType.DMA((2,)),
                pltpu.SemaphoreType.DMA((2,)),
            ],
        ),
        out_shape=jax.ShapeDtypeStruct((TOP_K_INNER * t, d), F32),
        compiler_params=_params("arbitrary"),
        name="moe_experts",
    )(*tables, pad_start, eid, rank, h2, w_gu, w_dn)


def _block_tables(counts, m_assign):
    padded = ((counts + MOE_BLOCK - 1) // MOE_BLOCK) * MOE_BLOCK
    pad_end = jnp.cumsum(padded)
    pad_start = pad_end - padded
    n_blocks = -(-m_assign // MOE_BLOCK) + N_EXPERTS
    blk_start = jnp.arange(n_blocks, dtype=jnp.int32) * MOE_BLOCK
    blk_exp = jnp.minimum(jnp.sum(pad_end[None, :] <= blk_start[:, None], axis=1), N_EXPERTS - 1).astype(jnp.int32)
    nvalid = jnp.clip(counts[blk_exp] - (blk_start - pad_start[blk_exp]), 0, MOE_BLOCK)
    nvalid = jnp.where(blk_start < pad_end[-1], nvalid, 0).astype(jnp.int32)
    prev_exp = jnp.concatenate([jnp.full((1,), -1, jnp.int32), blk_exp[:-1]])
    first = ((nvalid > 0) & (blk_exp != prev_exp)).astype(jnp.int32)
    wslot = jnp.maximum(jnp.cumsum(first) - 1, 0) % 2
    ids = jnp.arange(N_EXPERTS, dtype=jnp.int32)
    later = jnp.where((ids[None, :] > ids[:, None]) & (counts[None, :] > 0), ids[None, :], N_EXPERTS)
    next_of = jnp.min(later, axis=1)
    next_exp = jnp.where(next_of == N_EXPERTS, -1, next_of)[blk_exp]
    tables = tuple(a.astype(jnp.int32) for a in (blk_exp, nvalid, first, wslot, next_exp))
    return tables, pad_start.astype(jnp.int32)


def _final_kernel(x_ref, ya_ref, yb_ref, rg_ref, g_ref, o_ref):
    x = x_ref[...] + rg_ref[:, 0:1] * ya_ref[...] + rg_ref[:, 1:2] * yb_ref[...]
    o_ref[...] = _rms(x, g_ref[...])


def _final(x1, y2, gates, gain, tm):
    t, d = x1.shape
    nt = t // tm
    return pl.pallas_call(
        _final_kernel,
        grid=(nt,),
        in_specs=[
            pl.BlockSpec((tm, d), lambda i: (i, 0)),
            pl.BlockSpec((tm, d), lambda i: (i, 0)),
            pl.BlockSpec((tm, d), lambda i: (i + nt, 0)),
            pl.BlockSpec((tm, LANES), lambda i: (i, 0)),
            pl.BlockSpec((1, d), lambda i: (0, 0)),
        ],
        out_specs=pl.BlockSpec((tm, d), lambda i: (i, 0)),
        out_shape=jax.ShapeDtypeStruct((t, d), F32),
        compiler_params=_params("parallel"),
        name="final_norm",
    )(x1, y2, y2, gates, gain)


def _pad_cols(a, n):
    return jnp.pad(a, ((0, 0), (0, n - a.shape[1])))


def _rwkv_mix(p_rwkv, mu, w0, wdu, a0, wau, wgu, k_k, k_a, r_k, gn_w, gn_b):
    row = lambda a: a.reshape(1, -1)
    outs = _prep(p_rwkv, _pad_cols(row(mu), RWKV_COLS_PAD), row(w0), wdu, row(a0), wau,
                 jnp.pad(wgu, ((0, GATE_LORA_PAD - GATE_LORA), (0, 0))), row(k_k), row(k_a), row(r_k), tm=256)
    return _scan(*outs, row(gn_w), row(gn_b))


def _mix_and_route(x, y_rwkv, y_attn, w_out, ln2, w_rg, b_rg, w_re, b_re):
    row = lambda a: a.reshape(1, -1)
    wr = _pad_cols(jnp.concatenate([w_rg, w_re], axis=1), LANES)
    br = _pad_cols(row(jnp.concatenate([b_rg, b_re])), LANES)
    return _out_proj(x, y_rwkv, y_attn, w_out.astype(BF16), row(ln2), wr, br, tm=256)


def _experts(h2, route_ints, counts, w_gu, w_dn):
    t = h2.shape[0]
    tables, pad_start = _block_tables(counts[0, :N_EXPERTS], t * TOP_K_INNER)
    eid = route_ints[:, 0:TOP_K_INNER].reshape(-1)
    rank = route_ints[:, TOP_K_INNER:2 * TOP_K_INNER].reshape(-1)
    return _moe(h2, w_gu, w_dn, tables, pad_start, eid, rank)


def _layer(x, positions, ln1, w_in, mu, w0, wdu, a0, wau, wgu, k_k, k_a, r_k, gn_w, gn_b, sinks, w_out, ln2,
           w_rg, b_rg, w_re, b_re, w_gu, w_dn, ln_f):
    t = x.shape[0]
    row = lambda a: a.reshape(1, -1)
    w_rwkv, w_attn = _split_w_in(w_in, tk=256)
    p_rwkv = _in_proj(x, row(ln1), w_rwkv, tm=min(1024, t), tn=RWKV_COLS_PAD // 3, out_dtype=F32)
    qkv = _in_proj(x, row(ln1), w_attn, tm=min(1024, t), tn=ATTN_COLS, out_dtype=BF16)

    y_rwkv = _rwkv_mix(p_rwkv, mu, w0, wdu, a0, wau, wgu, k_k, k_a, r_k, gn_w, gn_b)

    inv_freq = ROPE_THETA ** (-jnp.arange(0, HEAD_DIM, 2, dtype=F32) / HEAD_DIM)
    freq = jnp.tile(inv_freq, 2 * LANES // HEAD_DIM).reshape(1, LANES)
    y_attn = _attn(qkv, positions.reshape(t, 1), freq, sinks)

    x1, h2, route_ints, gates, counts = _mix_and_route(x, y_rwkv, y_attn, w_out, ln2, w_rg, b_rg, w_re, b_re)
    y2 = _experts(h2, route_ints, counts, w_gu, w_dn)
    return _final(x1, y2, gates, row(ln_f), tm=256)


def kernel(x, positions, ln1, w_in, mu_shift, w_decay0, w_decay_up, a0, w_a_up, w_g_up, k_k, k_a, r_k, gn_w, gn_b,
           sinks, w_out, ln2, w_router_group, b_router_group, w_router_expert, b_router_expert, w_expert_gu,
           w_expert_down, ln_f):
    assert ln1.shape[0] == 1, "one trunk layer"
    outs = [
        _layer(x[i], positions[i], ln1[0], w_in[0], mu_shift[0], w_decay0[0], w_decay_up[0], a0[0], w_a_up[0],
               w_g_up[0], k_k[0], k_a[0], r_k[0], gn_w[0], gn_b[0], sinks[0], w_out[0], ln2[0], w_router_group[0],
               b_router_group[0], w_router_expert[0], b_router_expert[0], w_expert_gu[0], w_expert_down[0], ln_f)
        for i in range(x.shape[0])
    ]
    return jnp.stack(outs, axis=0)
```

```python
import math

import jax
import jax.numpy as jnp
from jax import lax
from jax.experimental import pallas as pl
from jax.experimental.pallas import tpu as pltpu

D_MODEL = 2048
HEAD_DIM = 64
RWKV_WIDTH = 1024
ATTN_WIDTH = 1024
ATTN_KV_HEADS = 4
ATTN_GROUP = 4
ATTN_KV_WIDTH = 256
WINDOW = 128
ROPE_THETA = 10000.0
DECAY_LORA = 64
AAA_LORA = 64
GATE_LORA = 160
GATE_LORA_PAD = 256
RWKV_COLS = 3 * RWKV_WIDTH + DECAY_LORA + AAA_LORA + GATE_LORA
RWKV_COLS_PAD = 3 * RWKV_WIDTH + DECAY_LORA + AAA_LORA + GATE_LORA_PAD
ATTN_COLS = ATTN_WIDTH + 2 * ATTN_KV_WIDTH
N_GROUPS = 8
EXPERTS_PER_GROUP = 8
N_EXPERTS = 64
TOP_K_INNER = 2
D_EXPERT = 768
MOE_BLOCK = 256
RMS_EPS = 1e-6
RWKV_GN_EPS = 64e-5

LANES = 128
SUBLANES = 8
CHUNK = 64
SCAN_CHUNKS = 8
EXPERT_SPLIT = 3
VMEM_LIMIT = 56 * 1024 * 1024

F32 = jnp.float32
BF16 = jnp.bfloat16


def _dot(a, b):
    return jnp.dot(a, b, preferred_element_type=F32)


def _split(x, parts):
    out = []
    for _ in range(parts - 1):
        hi = x.astype(BF16)
        out.append(hi)
        x = x - hi.astype(F32)
    out.append(x.astype(BF16))
    return out


def _dot_x3(a, b):
    ah, al = _split(a, 2)
    bh, bl = _split(b, 2)
    return _dot(ah, bh) + (_dot(ah, bl) + _dot(al, bh))


def _dot_sel(a, sel):
    ah, al = _split(a, 2)
    return _dot(ah, sel) + _dot(al, sel)


def _dot_nt(a, b):
    return lax.dot_general(a, b, (((1,), (1,)), ((), ())), preferred_element_type=F32)


def _dot_tn(a, b):
    return lax.dot_general(a, b, (((0,), (0,)), ((), ())), preferred_element_type=F32)


def _params(*sem):
    return pltpu.CompilerParams(dimension_semantics=sem, vmem_limit_bytes=VMEM_LIMIT)


def _head_sum_matrix():
    r = lax.broadcasted_iota(jnp.int32, (LANES, LANES), 0) // HEAD_DIM
    c = lax.broadcasted_iota(jnp.int32, (LANES, LANES), 1) // HEAD_DIM
    return (r == c).astype(BF16)


def _head_sums(x, bd):
    parts = [_dot_sel(x[:, j * LANES:(j + 1) * LANES], bd) for j in range(x.shape[1] // LANES)]
    return jnp.concatenate(parts, axis=1)


def _rms(x, gain):
    return x * lax.rsqrt(jnp.mean(x * x, axis=-1, keepdims=True) + RMS_EPS) * gain


def _in_proj_kernel(x_ref, g_ref, w_ref, o_ref, h_ref):
    @pl.when(pl.program_id(1) == 0)
    def _():
        h_ref[...] = _rms(x_ref[...], g_ref[...]).astype(BF16)

    o_ref[...] = _dot(h_ref[...], w_ref[...]).astype(o_ref.dtype)


def _in_proj(x, gain, w, tm, tn, n, out_dtype):
    t, d = x.shape
    return pl.pallas_call(
        _in_proj_kernel,
        grid=(t // tm, n // tn),
        in_specs=[
            pl.BlockSpec((tm, d), lambda i, j: (i, 0)),
            pl.BlockSpec((1, d), lambda i, j: (0, 0)),
            pl.BlockSpec((d, tn), lambda i, j: (0, j)),
        ],
        out_specs=pl.BlockSpec((tm, tn), lambda i, j: (i, j)),
        out_shape=jax.ShapeDtypeStruct((t, n), out_dtype),
        scratch_shapes=[pltpu.VMEM((tm, d), BF16)],
        compiler_params=_params("parallel", "arbitrary"),
        name="in_proj",
    )(x, gain, w)


def _prep_kernel(p_ref, prev_ref, mu_ref, w0_ref, wdu_ref, a0_ref, wau_ref, wgu_ref, kk_ref, ka_ref, rk_ref,
                 at_ref, rt_ref, bt_ref, kt_ref, bh_ref, kh_ref, v_ref, gam_ref, g_ref, bv_ref):
    i = pl.program_id(0)
    tm = p_ref.shape[0]
    w = RWKV_WIDTH
    p = p_ref[...]
    prev_row = jnp.where(i == 0, 0.0, prev_ref[7:8, :])
    row = lax.broadcasted_iota(jnp.int32, p.shape, 0)
    shifted = jnp.where(row == 0, prev_row, pltpu.roll(p, 1, 0))
    ps = p + (shifted - p) * mu_ref[...]

    r = ps[:, 0:w]
    k = ps[:, w:2 * w]
    v = ps[:, 2 * w:3 * w]
    wd = ps[:, 3 * w:3 * w + DECAY_LORA]
    ad = ps[:, 3 * w + DECAY_LORA:3 * w + DECAY_LORA + AAA_LORA]
    gd = ps[:, 3 * w + DECAY_LORA + AAA_LORA:]

    z = w0_ref[...] + _dot_x3(jnp.tanh(wd), wdu_ref[...])
    logw = -math.exp(-0.5) * jax.nn.sigmoid(z)
    alr = jax.nn.sigmoid(a0_ref[...] + _dot_x3(ad, wau_ref[...]))
    g_ref[...] = _dot_x3(jax.nn.sigmoid(gd), wgu_ref[...])

    bd = _head_sum_matrix()
    kk = k * kk_ref[...]
    kk = kk * lax.rsqrt(jnp.maximum(_head_sums(kk * kk, bd), 1e-24))
    kmod = k * (1.0 + (alr - 1.0) * ka_ref[...])
    b = kk * alr
    bv_ref[...] = _head_sums(r * kmod * rk_ref[...], bd) * v
    v_ref[...] = v.astype(BF16)

    rr = lax.broadcasted_iota(jnp.int32, (tm, tm), 0)
    cc = lax.broadcasted_iota(jnp.int32, (tm, tm), 1)
    tri = ((cc <= rr) & (cc // CHUNK == rr // CHUNK)).astype(BF16)
    l1, l2, l3 = _split(logw, 3)
    cum = _dot(tri, l1) + (_dot(tri, l2) + _dot(tri, l3))
    tot_rows = []
    for c in range(tm // CHUNK):
        last = cum[c * CHUNK + CHUNK - 1:c * CHUNK + CHUNK, :]
        gam_ref[c] = jnp.exp(last)
        tot_rows.append(jnp.broadcast_to(last, (CHUNK, w)))
    tot = jnp.concatenate(tot_rows, axis=0)

    e_neg = jnp.exp(-cum)
    e_rem = jnp.exp(tot - cum)
    at_ref[...] = (-kk * jnp.exp(cum - logw)).astype(BF16)
    rt_ref[...] = (r * jnp.exp(cum)).astype(BF16)
    bt_ref[...] = (b * e_neg).astype(BF16)
    kt_ref[...] = (kmod * e_neg).astype(BF16)
    bh_ref[...] = (b * e_rem).astype(BF16)
    kh_ref[...] = (kmod * e_rem).astype(BF16)


def _prep(p_rwkv, mu, w0, wdu, a0, wau, wgu, k_k, k_a, r_k, tm):
    t = p_rwkv.shape[0]
    w = RWKV_WIDTH
    cp = RWKV_COLS_PAD
    nc = tm // CHUNK
    row = lambda i: (i, 0)
    fixed = lambda i: (0, 0)
    vec = pl.BlockSpec((1, w), fixed)
    big_bf = jax.ShapeDtypeStruct((t, w), BF16)
    big_f32 = jax.ShapeDtypeStruct((t, w), F32)
    out_tile = pl.BlockSpec((tm, w), row)
    return pl.pallas_call(
        _prep_kernel,
        grid=(t // tm,),
        in_specs=[
            pl.BlockSpec((tm, cp), row),
            pl.BlockSpec((8, cp), lambda i: (jnp.maximum(i * (tm // 8) - 1, 0), 0)),
            pl.BlockSpec((1, cp), fixed),
            vec, pl.BlockSpec((DECAY_LORA, w), fixed),
            vec, pl.BlockSpec((AAA_LORA, w), fixed),
            pl.BlockSpec((GATE_LORA_PAD, w), fixed),
            vec, vec, vec,
        ],
        out_specs=[out_tile] * 7 + [pl.BlockSpec((nc, 1, w), lambda i: (i, 0, 0)), out_tile, out_tile],
        out_shape=[big_bf] * 7 + [jax.ShapeDtypeStruct((t // CHUNK, 1, w), F32), big_f32, big_f32],
        compiler_params=_params("parallel"),
        name="rwkv_prep",
    )(p_rwkv, p_rwkv, mu, w0, wdu, a0, wau, wgu, k_k, k_a, r_k)


def _chunk_factors(nchunks, own, strict, incl, eye, at_ref, rt_ref, bt_ref, kt_ref, bh_ref, kh_ref, v_ref, gam_ref,
                   tick):
    js = range(nchunks)
    n = 2 * CHUNK

    def stack(ref):
        xs = [ref[pl.ds(j * CHUNK, CHUNK), :] for j in js]
        return [jnp.where(own, jnp.concatenate([x, x], axis=0), jnp.zeros((), x.dtype)) for x in xs]

    a_s, r_s, b_s, k_s = stack(at_ref), stack(rt_ref), stack(bt_ref), stack(kt_ref)
    bh_s, kh_s, v_s = stack(bh_ref), stack(kh_ref), stack(v_ref)

    prod = [_dot_nt(jnp.concatenate([a_s[j], r_s[j]], axis=0), jnp.concatenate([b_s[j], k_s[j]], axis=0)) for j in js]
    a_ak = [jnp.where(strict, prod[j][:n, n:], 0.0).astype(BF16) for j in js]
    a_rb = [jnp.where(incl, prod[j][n:, :n], 0.0).astype(BF16) for j in js]
    a_rk = [jnp.where(incl, prod[j][n:, n:], 0.0).astype(BF16) for j in js]

    tick()
    pq = [jnp.concatenate([a_s[j].astype(F32), _dot(a_ak[j], v_s[j])], axis=1) for j in js]
    nk_b = [jnp.where(strict, prod[j][:n, :n], 0.0).astype(BF16) for j in js]
    span = 1
    while True:
        tick()
        pq = [pq[j] + _dot(nk_b[j], pq[j].astype(BF16)) for j in js]
        span *= 2
        if span >= CHUNK:
            break
        tick()
        nk_b = [_dot(nk_b[j], nk_b[j]).astype(BF16) for j in js]
    pq_b = [pq[j].astype(BF16) for j in js]

    zero = jnp.zeros((n, LANES), BF16)
    rhs = [jnp.concatenate([pq_b[j], jnp.concatenate([zero, v_s[j]], axis=1)], axis=0) for j in js]
    mn = [_dot_tn(jnp.concatenate([bh_s[j], kh_s[j]], axis=0), rhs[j]) for j in js]
    yy = [_dot(jnp.concatenate([a_rb[j], a_rk[j]], axis=1), rhs[j]) for j in js]
    out = []
    for j in js:
        m_mat = mn[j][:, :LANES] + jnp.where(eye, jnp.broadcast_to(gam_ref[j], (LANES, LANES)), 0.0)
        n_mat = mn[j][:, LANES:]
        y_c = r_s[j].astype(F32) + yy[j][:, :LANES]
        y_n = yy[j][:, LANES:]
        out.append((m_mat, n_mat, y_c, y_n))
    return out


def _scan_kernel(at_ref, rt_ref, bt_ref, kt_ref, bh_ref, kh_ref, v_ref, gam_ref, g_ref, bv_ref, gnw_ref, gnb_ref,
                 y_ref, h_ref, f_ref):
    @pl.when(pl.program_id(1) == 0)
    def _():
        h_ref[...] = jnp.zeros_like(h_ref)
        f_ref[...] = jnp.zeros_like(f_ref)

    chain = {"h": h_ref[...], "ys": []}

    def tick():
        j = len(chain["ys"])
        if j < SCAN_CHUNKS:
            h = chain["h"].astype(BF16)
            y_st = _dot(f_ref[j, 2].astype(BF16), h) + f_ref[j, 3]
            chain["h"] = _dot(f_ref[j, 0].astype(BF16), h) + f_ref[j, 1]
            chain["ys"].append(y_st[:CHUNK] + y_st[CHUNK:])

    n = 2 * CHUNK
    own = (lax.broadcasted_iota(jnp.int32, (n, LANES), 1) // HEAD_DIM
           == lax.broadcasted_iota(jnp.int32, (n, LANES), 0) // CHUNK)
    ti = lax.broadcasted_iota(jnp.int32, (n, n), 0)
    si = lax.broadcasted_iota(jnp.int32, (n, n), 1)
    same = (ti // CHUNK) == (si // CHUNK)
    strict = same & (si < ti)
    incl = same & (si <= ti)
    eye = ti == si

    factors = _chunk_factors(SCAN_CHUNKS, own, strict, incl, eye, at_ref, rt_ref, bt_ref, kt_ref, bh_ref, kh_ref,
                             v_ref, gam_ref, tick)
    while len(chain["ys"]) < SCAN_CHUNKS:
        tick()
    h_ref[...] = chain["h"]
    y = jnp.concatenate(chain["ys"], axis=0)

    low = lax.broadcasted_iota(jnp.int32, y.shape, 1) < HEAD_DIM

    def head_mean(x):
        s_low = jnp.sum(jnp.where(low, x, 0.0), axis=-1, keepdims=True)
        s_all = jnp.sum(x, axis=-1, keepdims=True)
        return jnp.where(low, s_low, s_all - s_low) * (1.0 / HEAD_DIM)

    yc = y - head_mean(y)
    var = head_mean(yc * yc)
    yn = yc * lax.rsqrt(var + RWKV_GN_EPS) * gnw_ref[...] + gnb_ref[...]
    y_ref[...] = ((yn + bv_ref[...]) * g_ref[...]).astype(BF16)

    for j, mats in enumerate(factors):
        for i, mat in enumerate(mats):
            f_ref[j, i] = mat


def _scan(at, rt, bt, kt, bh, kh, v, gam, g, bv, gn_w, gn_b):
    t = at.shape[0]
    rows = SCAN_CHUNKS * CHUNK
    steps = t // rows
    tile = pl.BlockSpec((rows, LANES), lambda p, s: (jnp.minimum(s, steps - 1), p))
    lagged = pl.BlockSpec((rows, LANES), lambda p, s: (jnp.maximum(s - 1, 0), p))
    vec = pl.BlockSpec((1, LANES), lambda p, s: (0, p))
    gam_spec = pl.BlockSpec((SCAN_CHUNKS, 1, LANES), lambda p, s: (jnp.minimum(s, steps - 1), 0, p))
    return pl.pallas_call(
        _scan_kernel,
        grid=(RWKV_WIDTH // LANES, steps + 1),
        in_specs=[tile] * 7 + [gam_spec, lagged, lagged, vec, vec],
        out_specs=lagged,
        out_shape=jax.ShapeDtypeStruct((t, RWKV_WIDTH), BF16),
        scratch_shapes=[pltpu.VMEM((LANES, LANES), F32), pltpu.VMEM((SCAN_CHUNKS, 4, 2 * CHUNK, LANES), F32)],
        compiler_params=_params("parallel", "arbitrary"),
        name="rwkv_scan",
    )(at, rt, bt, kt, bh, kh, v, gam, g, bv, gn_w, gn_b)


ATTN_BLOCKS = 1


def _attn_kernel(sink_ref, qkv_ref, pos_ref, freq_ref, o_ref, kprev_ref, vprev_ref):
    step = pl.program_id(0)
    rows = ATTN_BLOCKS * WINDOW

    @pl.when(step == 0)
    def _():
        kprev_ref[...] = jnp.zeros_like(kprev_ref)
        vprev_ref[...] = jnp.zeros_like(vprev_ref)

    ang = pos_ref[...].astype(F32) * freq_ref[...]
    cos = jnp.cos(ang)
    lane = lax.broadcasted_iota(jnp.int32, ang.shape, 1)
    first_half = lane % HEAD_DIM < HEAD_DIM // 2
    sin = jnp.where(first_half, -1.0, 1.0) * jnp.sin(ang)

    def rope(x):
        swapped = jnp.where(first_half, pltpu.roll(x, LANES - HEAD_DIM // 2, 1), pltpu.roll(x, HEAD_DIM // 2, 1))
        return x * cos + swapped * sin

    scale = HEAD_DIM ** -0.5
    qi = lax.broadcasted_iota(jnp.int32, (WINDOW, 2 * WINDOW), 0) + WINDOW
    ki = lax.broadcasted_iota(jnp.int32, (WINDOW, 2 * WINDOW), 1)
    rel = qi - ki
    local = (rel >= 0) & (rel < WINDOW)
    masks = [local & ((step > 0) | (ki >= WINDOW))] + [local] * (ATTN_BLOCKS - 1)

    k_cur = jnp.concatenate([rope(qkv_ref[:, ATTN_WIDTH + j * LANES:ATTN_WIDTH + (j + 1) * LANES].astype(F32))
                             for j in range(ATTN_KV_WIDTH // LANES)], axis=1)
    v_cur = qkv_ref[:, ATTN_WIDTH + ATTN_KV_WIDTH:].astype(F32)
    k_all = jnp.concatenate([kprev_ref[...], k_cur], axis=0)
    v_all = jnp.concatenate([vprev_ref[...], v_cur], axis=0)
    k_heads = [k_all[:, g * HEAD_DIM:(g + 1) * HEAD_DIM].astype(BF16) for g in range(ATTN_KV_HEADS)]
    v_heads = [v_all[:, g * HEAD_DIM:(g + 1) * HEAD_DIM].astype(BF16) for g in range(ATTN_KV_HEADS)]

    heads = range(ATTN_WIDTH // HEAD_DIM)
    per_group = LANES // HEAD_DIM
    units = [(h, u) for u in range(ATTN_BLOCKS) for h in heads]
    q_groups = [rope(qkv_ref[:, j * LANES:(j + 1) * LANES].astype(F32)) * scale for j in range(ATTN_WIDTH // LANES)]

    def q_of(h, u):
        cols = slice((h % per_group) * HEAD_DIM, (h % per_group + 1) * HEAD_DIM)
        return q_groups[h // per_group][u * WINDOW:(u + 1) * WINDOW, cols].astype(BF16)

    def band(xs, h, u):
        return xs[h // ATTN_GROUP][u * WINDOW:(u + 2) * WINDOW]

    s = {hu: jnp.where(masks[hu[1]], _dot_nt(q_of(*hu), band(k_heads, *hu)), -jnp.inf) for hu in units}
    probs = {}
    for h, u in units:
        sink = sink_ref[h]
        m = jnp.maximum(jnp.max(s[h, u], axis=-1, keepdims=True), sink)
        pexp = jnp.exp(s[h, u] - m)
        denom = jnp.sum(pexp, axis=-1, keepdims=True) + jnp.exp(sink - m)
        probs[h, u] = (pexp * (1.0 / denom)).astype(BF16)
    o = {hu: _dot(probs[hu], band(v_heads, *hu)) for hu in units}
    for u in range(ATTN_BLOCKS):
        for j in range(ATTN_WIDTH // LANES):
            o_ref[u * WINDOW:(u + 1) * WINDOW, j * LANES:(j + 1) * LANES] = jnp.concatenate(
                [o[h, u] for h in range(j * per_group, (j + 1) * per_group)], axis=1).astype(BF16)

    kprev_ref[...] = k_cur[rows - WINDOW:]
    vprev_ref[...] = v_cur[rows - WINDOW:]


def _attn(qkv, pos_col, freq, sinks):
    t = qkv.shape[0]
    rows = ATTN_BLOCKS * WINDOW
    return pl.pallas_call(
        _attn_kernel,
        grid_spec=pltpu.PrefetchScalarGridSpec(
            num_scalar_prefetch=1,
            grid=(t // rows,),
            in_specs=[
                pl.BlockSpec((rows, ATTN_COLS), lambda i, s: (i, 0)),
                pl.BlockSpec((rows, 1), lambda i, s: (i, 0)),
                pl.BlockSpec((1, LANES), lambda i, s: (0, 0)),
            ],
            out_specs=pl.BlockSpec((rows, ATTN_WIDTH), lambda i, s: (i, 0)),
            scratch_shapes=[pltpu.VMEM((WINDOW, ATTN_KV_WIDTH), F32), pltpu.VMEM((WINDOW, ATTN_KV_WIDTH), F32)],
        ),
        out_shape=jax.ShapeDtypeStruct((t, ATTN_WIDTH), BF16),
        compiler_params=_params("arbitrary"),
        name="swa_attn",
    )(sinks, qkv, pos_col, freq)


def _route_stages(lg, run, out):
    tm = lg.shape[0]
    lane = lax.broadcasted_iota(jnp.int32, lg.shape, 1)
    lane_f = lane.astype(F32)
    ninf = -jnp.inf

    def top(vals):
        best = jnp.max(vals, axis=-1, keepdims=True)
        idx = jnp.min(jnp.where(vals == best, lane_f, float(LANES)), axis=-1, keepdims=True)
        return best, idx

    gl = jnp.where(lane < N_GROUPS, lg, ninf)
    gmax, gidx = top(gl)
    pg_top = 1.0 / jnp.sum(jnp.exp(gl - gmax), axis=-1, keepdims=True)
    yield
    first = N_GROUPS + EXPERTS_PER_GROUP * gidx
    el = jnp.where((lane_f >= first) & (lane_f < first + EXPERTS_PER_GROUP), lg, ninf)
    v0, i0 = top(el)
    yield
    v1, i1 = top(jnp.where(lane_f == i0, ninf, el))
    ex = jnp.exp(v1 - v0)
    g0 = pg_top / (1.0 + ex)
    g1 = pg_top * ex / (1.0 + ex)
    e0 = i0 - N_GROUPS
    e1 = i1 - N_GROUPS
    yield

    hot0 = (lane_f == e0).astype(BF16)
    hot1 = (lane_f == e1).astype(BF16)
    rr = lax.broadcasted_iota(jnp.int32, (tm, tm), 0)
    cc = lax.broadcasted_iota(jnp.int32, (tm, tm), 1)
    before = (cc < rr).astype(BF16)
    cum = _dot(before, jnp.concatenate([hot0, hot1], axis=1))
    h0 = hot0.astype(F32)
    h1 = hot1.astype(F32)
    tot0 = jnp.sum(h0, axis=0, keepdims=True)
    tot1 = jnp.sum(h1, axis=0, keepdims=True)
    rank0 = jnp.sum(h0 * (run + cum[:, :LANES]), axis=-1, keepdims=True)
    rank1 = jnp.sum(h1 * (run + tot0 + cum[:, LANES:]), axis=-1, keepdims=True)
    ints = jnp.where(lane == 0, e0, jnp.where(lane == 1, e1, jnp.where(lane == 2, rank0, jnp.where(lane == 3, rank1, 0.0))))
    out["ints"] = ints.astype(jnp.int32)
    out["gates"] = jnp.where(lane == 0, g0, jnp.where(lane == 1, g1, 0.0))
    out["run"] = run + tot0 + tot1
    yield


OUT_PROJ_SPLIT = 4


def _out_proj_kernel(x_ref, ya_ref, yb_ref, wo_ref, g_ref, wr_ref, br_ref,
                     x1_ref, h2_ref, ri_ref, rg_ref, cnt_ref, run_ref, lg_ref):
    s = pl.program_id(0)

    @pl.when(s == 0)
    def _():
        run_ref[...] = jnp.zeros_like(run_ref)
        lg_ref[...] = jnp.zeros_like(lg_ref)

    routed = {}
    stages = _route_stages(lg_ref[...], run_ref[...], routed)
    ya = ya_ref[...]
    yb = yb_ref[...]
    width = D_MODEL // OUT_PROJ_SPLIT
    ssq = 0.0
    for c in range(OUT_PROJ_SPLIT):
        cols = slice(c * width, (c + 1) * width)
        x1 = x_ref[:, cols] + _dot(ya, wo_ref[:RWKV_WIDTH, cols]) + _dot(yb, wo_ref[RWKV_WIDTH:, cols])
        x1_ref[:, cols] = x1
        ssq = ssq + jnp.sum(x1 * x1, axis=-1, keepdims=True)
        next(stages)

    ri_ref[...] = routed["ints"]
    rg_ref[...] = routed["gates"]
    run = jnp.where(s > 0, routed["run"], run_ref[...])
    run_ref[...] = run
    cnt_ref[...] = run.astype(jnp.int32)

    h2 = x1_ref[...] * lax.rsqrt(ssq * (1.0 / D_MODEL) + RMS_EPS) * g_ref[...]
    h2_ref[...] = h2
    lg_ref[...] = _dot_x3(h2, wr_ref[...]) + br_ref[...]


def _out_proj(x, y_rwkv, y_attn, wo, ln2, wr, br, tm):
    t, d = x.shape
    steps = t // tm
    row = lambda i: (jnp.minimum(i, steps - 1), 0)
    lagged = lambda i: (jnp.maximum(i - 1, 0), 0)
    fixed = lambda i: (0, 0)
    return pl.pallas_call(
        _out_proj_kernel,
        grid=(steps + 1,),
        in_specs=[
            pl.BlockSpec((tm, d), row),
            pl.BlockSpec((tm, RWKV_WIDTH), row),
            pl.BlockSpec((tm, ATTN_WIDTH), row),
            pl.BlockSpec((d, d), fixed),
            pl.BlockSpec((1, d), fixed),
            pl.BlockSpec((d, LANES), fixed),
            pl.BlockSpec((1, LANES), fixed),
        ],
        out_specs=[pl.BlockSpec((tm, d), row), pl.BlockSpec((tm, d), row), pl.BlockSpec((tm, LANES), lagged),
                   pl.BlockSpec((tm, LANES), lagged), pl.BlockSpec((1, LANES), fixed)],
        out_shape=[jax.ShapeDtypeStruct((t, d), F32), jax.ShapeDtypeStruct((t, d), F32),
                   jax.ShapeDtypeStruct((t, LANES), jnp.int32), jax.ShapeDtypeStruct((t, LANES), F32),
                   jax.ShapeDtypeStruct((1, LANES), jnp.int32)],
        scratch_shapes=[pltpu.VMEM((1, LANES), F32), pltpu.VMEM((tm, LANES), F32)],
        compiler_params=_params("arbitrary"),
        name="out_proj_router",
    )(x, y_rwkv, y_attn, wo, ln2, wr, br)


def _for_rows(n, fn):
    full = lax.shift_right_logical(n, 3)

    def group(i, carry):
        for u in range(SUBLANES):
            fn(i, u)
        return carry

    def single(r, carry):
        fn(full, r)
        return carry

    lax.fori_loop(0, full, group, 0)
    lax.fori_loop(0, n - full * SUBLANES, single, 0)


def _moe_kernel(exp_ref, nvalid_ref, first_ref, wslot_ref, next_ref, start_ref, eid_ref, rank_ref,
                h_hbm, wgu_hbm, wdn_hbm, y_hbm, xbuf, obuf, wgu_buf, wdn_buf, slot_ref, gsem, ssem, wsem):
    b = pl.program_id(0)
    last = pl.num_programs(0) - 1
    cur = lax.rem(b, 2)
    nv = nvalid_ref[b]
    ws = wslot_ref[b]
    n_tok = h_hbm.shape[0]

    def weight_copies(e, buf):
        return (pltpu.make_async_copy(wgu_hbm.at[e], wgu_buf.at[buf], wsem.at[buf]),
                pltpu.make_async_copy(wdn_hbm.at[e], wdn_buf.at[buf], wsem.at[buf]))

    def gather_copy(blk, buf, i, u):
        a = slot_ref[blk * MOE_BLOCK + i * SUBLANES + u]
        tok = lax.shift_right_logical(a, 1)
        return pltpu.make_async_copy(h_hbm.at[pl.ds(tok, 1), :], xbuf.at[buf, i, pl.ds(u, 1), :], gsem.at[buf])

    def scatter_copy(blk, buf, i, u):
        a = slot_ref[blk * MOE_BLOCK + i * SUBLANES + u]
        dst = (a & 1) * n_tok + lax.shift_right_logical(a, 1)
        return pltpu.make_async_copy(obuf.at[buf, i, pl.ds(u, 1), :], y_hbm.at[pl.ds(dst, 1), :], ssem.at[buf])

    def gather_start(blk, buf):
        _for_rows(nvalid_ref[blk], lambda i, u: gather_copy(blk, buf, i, u).start())

    def gather_wait(blk, buf):
        _for_rows(nvalid_ref[blk], lambda i, u: gather_copy(blk, buf, i, u).wait())

    def scatter_start(blk, buf):
        _for_rows(nvalid_ref[blk], lambda i, u: scatter_copy(blk, buf, i, u).start())

    def scatter_wait(blk, buf):
        _for_rows(nvalid_ref[blk], lambda i, u: scatter_copy(blk, buf, i, u).wait())

    @pl.when(b == 0)
    def _():
        for c in weight_copies(exp_ref[0], 0):
            c.start()

    @pl.when((first_ref[b] == 1) & (next_ref[b] >= 0))
    def _():
        for c in weight_copies(next_ref[b], 1 - ws):
            c.start()

    @pl.when(b == 0)
    def _():
        xbuf[...] = jnp.zeros_like(xbuf)

        def place(i, carry):
            for u in range(SUBLANES):
                a = i * SUBLANES + u
                slot_ref[start_ref[eid_ref[a]] + rank_ref[a]] = a
            return carry

        lax.fori_loop(0, eid_ref.shape[0] // SUBLANES, place, 0)
        gather_start(0, 0)

    @pl.when(b < last)
    def _():
        gather_start(b + 1, 1 - cur)

    @pl.when(b >= 2)
    def _():
        scatter_wait(b - 2, cur)

    def experts(rows):
        tiles = rows // SUBLANES
        x = xbuf[cur, :tiles].reshape(rows, D_MODEL).astype(BF16)
        out = obuf.at[cur, :tiles]
        step = D_EXPERT // EXPERT_SPLIT
        for j in range(EXPERT_SPLIT):
            hg = _dot(x, wgu_buf[ws, :, j * step:(j + 1) * step].astype(BF16))
            hu = _dot(x, wgu_buf[ws, :, D_EXPERT + j * step:D_EXPERT + (j + 1) * step].astype(BF16))
            act = (hg * jax.nn.sigmoid(hg) * hu).astype(BF16)
            part = _dot(act, wdn_buf[ws, j * step:(j + 1) * step, :].astype(BF16))
            part = part.reshape(tiles, SUBLANES, D_MODEL)
            if j == 0:
                out[...] = part
            else:
                out[...] += part

    @pl.when(nv > 0)
    def _():
        gather_wait(b, cur)

        @pl.when(first_ref[b] == 1)
        def _():
            for c in weight_copies(exp_ref[b], ws):
                c.wait()

        @pl.when(nv > MOE_BLOCK // 2)
        def _():
            experts(MOE_BLOCK)

        @pl.when(nv <= MOE_BLOCK // 2)
        def _():
            experts(MOE_BLOCK // 2)

        scatter_start(b, cur)

    @pl.when(b == last)
    def _():
        @pl.when(b >= 1)
        def _():
            scatter_wait(b - 1, 1 - cur)

        scatter_wait(b, cur)


def _moe(h2, w_gu, w_dn, tables, pad_start, eid, rank):
    t, d = h2.shape
    n_blocks = tables[0].shape[0]
    hbm = pl.BlockSpec(memory_space=pl.ANY)
    return pl.pallas_call(
        _moe_kernel,
        grid_spec=pltpu.PrefetchScalarGridSpec(
            num_scalar_prefetch=len(tables) + 3,
            grid=(n_blocks,),
            in_specs=[hbm, hbm, hbm],
            out_specs=hbm,
            scratch_shapes=[
                pltpu.VMEM((2, MOE_BLOCK // SUBLANES, SUBLANES, d), F32),
                pltpu.VMEM((2, MOE_BLOCK // SUBLANES, SUBLANES, d), F32),
                pltpu.VMEM((2, d, 2 * D_EXPERT), F32),
                pltpu.VMEM((2, D_EXPERT, d), F32),
                pltpu.SMEM((n_blocks * MOE_BLOCK,), jnp.int32),
                pltpu.SemaphoreType.DMA((2,)),
                pltpu.SemaphoreType.DMA((2,)),
                pltpu.SemaphoreType.DMA((2,)),
            ],
        ),
        out_shape=jax.ShapeDtypeStruct((TOP_K_INNER * t, d), F32),
        compiler_params=_params("arbitrary"),
        name="moe_experts",
    )(*tables, pad_start, eid, rank, h2, w_gu, w_dn)


def _block_tables(counts, m_assign):
    padded = ((counts + MOE_BLOCK - 1) // MOE_BLOCK) * MOE_BLOCK
    pad_end = jnp.cumsum(padded)
    pad_start = pad_end - padded
    n_blocks = -(-m_assign // MOE_BLOCK) + N_EXPERTS
    blk_start = jnp.arange(n_blocks, dtype=jnp.int32) * MOE_BLOCK
    blk_exp = jnp.minimum(jnp.sum(pad_end[None, :] <= blk_start[:, None], axis=1), N_EXPERTS - 1).astype(jnp.int32)
    nvalid = jnp.clip(counts[blk_exp] - (blk_start - pad_start[blk_exp]), 0, MOE_BLOCK)
    nvalid = jnp.where(blk_start < pad_end[-1], nvalid, 0).astype(jnp.int32)
    prev_exp = jnp.concatenate([jnp.full((1,), -1, jnp.int32), blk_exp[:-1]])
    first = ((nvalid > 0) & (blk_exp != prev_exp)).astype(jnp.int32)
    wslot = jnp.maximum(jnp.cumsum(first) - 1, 0) % 2
    ids = jnp.arange(N_EXPERTS, dtype=jnp.int32)
    later = jnp.where((ids[None, :] > ids[:, None]) & (counts[None, :] > 0), ids[None, :], N_EXPERTS)
    next_of = jnp.min(later, axis=1)
    next_exp = jnp.where(next_of == N_EXPERTS, -1, next_of)[blk_exp]
    tables = tuple(a.astype(jnp.int32) for a in (blk_exp, nvalid, first, wslot, next_exp))
    return tables, pad_start.astype(jnp.int32)


def _final_kernel(x_ref, ya_ref, yb_ref, rg_ref, g_ref, o_ref):
    x = x_ref[...] + rg_ref[:, 0:1] * ya_ref[...] + rg_ref[:, 1:2] * yb_ref[...]
    o_ref[...] = _rms(x, g_ref[...])


def _final(x1, y2, gates, gain, tm):
    t, d = x1.shape
    nt = t // tm
    return pl.pallas_call(
        _final_kernel,
        grid=(nt,),
        in_specs=[
            pl.BlockSpec((tm, d), lambda i: (i, 0)),
            pl.BlockSpec((tm, d), lambda i: (i, 0)),
            pl.BlockSpec((tm, d), lambda i: (i + nt, 0)),
            pl.BlockSpec((tm, LANES), lambda i: (i, 0)),
            pl.BlockSpec((1, d), lambda i: (0, 0)),
        ],
        out_specs=pl.BlockSpec((tm, d), lambda i: (i, 0)),
        out_shape=jax.ShapeDtypeStruct((t, d), F32),
        compiler_params=_params("parallel"),
        name="final_norm",
    )(x1, y2, y2, gates, gain)


def _pad_cols(a, n):
    return jnp.pad(a, ((0, 0), (0, n - a.shape[1])))


def _rwkv_mix(p_rwkv, mu, w0, wdu, a0, wau, wgu, k_k, k_a, r_k, gn_w, gn_b):
    row = lambda a: a.reshape(1, -1)
    outs = _prep(p_rwkv, _pad_cols(row(mu), RWKV_COLS_PAD), row(w0), wdu, row(a0), wau,
                 jnp.pad(wgu, ((0, GATE_LORA_PAD - GATE_LORA), (0, 0))), row(k_k), row(k_a), row(r_k), tm=256)
    return _scan(*outs, row(gn_w), row(gn_b))


def _mix_and_route(x, y_rwkv, y_attn, w_out, ln2, w_rg, b_rg, w_re, b_re):
    row = lambda a: a.reshape(1, -1)
    wr = _pad_cols(jnp.concatenate([w_rg, w_re], axis=1), LANES)
    br = _pad_cols(row(jnp.concatenate([b_rg, b_re])), LANES)
    return _out_proj(x, y_rwkv, y_attn, w_out.astype(BF16), row(ln2), wr, br, tm=256)


def _experts(h2, route_ints, counts, w_gu, w_dn):
    t = h2.shape[0]
    tables, pad_start = _block_tables(counts[0, :N_EXPERTS], t * TOP_K_INNER)
    eid = route_ints[:, 0:TOP_K_INNER].reshape(-1)
    rank = route_ints[:, TOP_K_INNER:2 * TOP_K_INNER].reshape(-1)
    return _moe(h2, w_gu, w_dn, tables, pad_start, eid, rank)


def _layer(x, positions, ln1, w_in, mu, w0, wdu, a0, wau, wgu, k_k, k_a, r_k, gn_w, gn_b, sinks, w_out, ln2,
           w_rg, b_rg, w_re, b_re, w_gu, w_dn, ln_f):
    t = x.shape[0]
    row = lambda a: a.reshape(1, -1)
    w_bf = w_in.astype(BF16)
    p_rwkv = _in_proj(x, row(ln1), w_bf, tm=min(1024, t), tn=RWKV_COLS_PAD // 3, n=RWKV_COLS_PAD, out_dtype=F32)
    qkv = _in_proj(x, row(ln1), w_bf[:, RWKV_COLS:], tm=min(1024, t), tn=ATTN_COLS, n=ATTN_COLS, out_dtype=BF16)

    y_rwkv = _rwkv_mix(p_rwkv, mu, w0, wdu, a0, wau, wgu, k_k, k_a, r_k, gn_w, gn_b)

    inv_freq = ROPE_THETA ** (-jnp.arange(0, HEAD_DIM, 2, dtype=F32) / HEAD_DIM)
    freq = jnp.tile(inv_freq, 2 * LANES // HEAD_DIM).reshape(1, LANES)
    y_attn = _attn(qkv, positions.reshape(t, 1), freq, sinks)

    x1, h2, route_ints, gates, counts = _mix_and_route(x, y_rwkv, y_attn, w_out, ln2, w_rg, b_rg, w_re, b_re)
    y2 = _experts(h2, route_ints, counts, w_gu, w_dn)
    return _final(x1, y2, gates, row(ln_f), tm=256)


def kernel(x, positions, ln1, w_in, mu_shift, w_decay0, w_decay_up, a0, w_a_up, w_g_up, k_k, k_a, r_k, gn_w, gn_b,
           sinks, w_out, ln2, w_router_group, b_router_group, w_router_expert, b_router_expert, w_expert_gu,
           w_expert_down, ln_f):
    assert ln1.shape[0] == 1, "one trunk layer"
    outs = [
        _layer(x[i], positions[i], ln1[0], w_in[0], mu_shift[0], w_decay0[0], w_decay_up[0], a0[0], w_a_up[0],
               w_g_up[0], k_k[0], k_a[0], r_k[0], gn_w[0], gn_b[0], sinks[0], w_out[0], ln2[0], w_router_group[0],
               b_router_group[0], w_router_expert[0], b_router_expert[0], w_expert_gu[0], w_expert_down[0], ln_f)
        for i in range(x.shape[0])
    ]
    return jnp.stack(outs, axis=0)
```

```python
import math

import jax
import jax.numpy as jnp
from jax import lax
from jax.experimental import pallas as pl
from jax.experimental.pallas import tpu as pltpu

D_MODEL = 2048
HEAD_DIM = 64
RWKV_WIDTH = 1024
ATTN_WIDTH = 1024
ATTN_KV_HEADS = 4
ATTN_GROUP = 4
ATTN_KV_WIDTH = 256
WINDOW = 128
ROPE_THETA = 10000.0
DECAY_LORA = 64
AAA_LORA = 64
GATE_LORA = 160
GATE_LORA_PAD = 256
RWKV_COLS = 3 * RWKV_WIDTH + DECAY_LORA + AAA_LORA + GATE_LORA
RWKV_COLS_PAD = 3 * RWKV_WIDTH + DECAY_LORA + AAA_LORA + GATE_LORA_PAD
ATTN_COLS = ATTN_WIDTH + 2 * ATTN_KV_WIDTH
N_GROUPS = 8
EXPERTS_PER_GROUP = 8
N_EXPERTS = 64
TOP_K_INNER = 2
D_EXPERT = 768
MOE_BLOCK = 256
RMS_EPS = 1e-6
RWKV_GN_EPS = 64e-5

LANES = 128
SUBLANES = 8
CHUNK = 64
SCAN_CHUNKS = 8
EXPERT_SPLIT = 3
VMEM_LIMIT = 56 * 1024 * 1024

F32 = jnp.float32
BF16 = jnp.bfloat16


def _dot(a, b):
    return jnp.dot(a, b, preferred_element_type=F32)


def _split(x, parts):
    out = []
    for _ in range(parts - 1):
        hi = x.astype(BF16)
        out.append(hi)
        x = x - hi.astype(F32)
    out.append(x.astype(BF16))
    return out


def _dot_x3(a, b):
    ah, al = _split(a, 2)
    bh, bl = _split(b, 2)
    return _dot(ah, bh) + (_dot(ah, bl) + _dot(al, bh))


def _dot_sel(a, sel):
    ah, al = _split(a, 2)
    return _dot(ah, sel) + _dot(al, sel)


def _dot_nt(a, b):
    return lax.dot_general(a, b, (((1,), (1,)), ((), ())), preferred_element_type=F32)


def _dot_tn(a, b):
    return lax.dot_general(a, b, (((0,), (0,)), ((), ())), preferred_element_type=F32)


def _params(*sem):
    return pltpu.CompilerParams(dimension_semantics=sem, vmem_limit_bytes=VMEM_LIMIT)


def _head_sum_matrix():
    r = lax.broadcasted_iota(jnp.int32, (LANES, LANES), 0) // HEAD_DIM
    c = lax.broadcasted_iota(jnp.int32, (LANES, LANES), 1) // HEAD_DIM
    return (r == c).astype(BF16)


def _head_sums(x, bd):
    parts = [_dot_sel(x[:, j * LANES:(j + 1) * LANES], bd) for j in range(x.shape[1] // LANES)]
    return jnp.concatenate(parts, axis=1)


def _rms(x, gain):
    return x * lax.rsqrt(jnp.mean(x * x, axis=-1, keepdims=True) + RMS_EPS) * gain


def _in_proj_kernel(x_ref, g_ref, w_ref, o_ref, h_ref):
    @pl.when(pl.program_id(1) == 0)
    def _():
        h_ref[...] = _rms(x_ref[...], g_ref[...]).astype(BF16)

    o_ref[...] = _dot(h_ref[...], w_ref[...]).astype(o_ref.dtype)


def _in_proj(x, gain, w, tm, tn, n, out_dtype):
    t, d = x.shape
    return pl.pallas_call(
        _in_proj_kernel,
        grid=(t // tm, n // tn),
        in_specs=[
            pl.BlockSpec((tm, d), lambda i, j: (i, 0)),
            pl.BlockSpec((1, d), lambda i, j: (0, 0)),
            pl.BlockSpec((d, tn), lambda i, j: (0, j)),
        ],
        out_specs=[pl.BlockSpec((tm, tn), lambda i, j: (i, j)), pl.BlockSpec((tm, d), lambda i, j: (i, 0))],
        out_shape=[jax.ShapeDtypeStruct((t, n), out_dtype), jax.ShapeDtypeStruct((t, d), BF16)],
        compiler_params=_params("parallel", "arbitrary"),
        name="in_proj",
    )(x, gain, w)


def _proj_kernel(h_ref, w_ref, o_ref):
    o_ref[...] = _dot(h_ref[...], w_ref[...]).astype(o_ref.dtype)


def _proj(h, w, tm, out_dtype):
    t, d = h.shape
    n = w.shape[1]
    return pl.pallas_call(
        _proj_kernel,
        grid=(t // tm,),
        in_specs=[pl.BlockSpec((tm, d), lambda i: (i, 0)), pl.BlockSpec((d, n), lambda i: (0, 0))],
        out_specs=pl.BlockSpec((tm, n), lambda i: (i, 0)),
        out_shape=jax.ShapeDtypeStruct((t, n), out_dtype),
        compiler_params=_params("parallel"),
        name="attn_proj",
    )(h, w)


def _prep_kernel(p_ref, prev_ref, mu_ref, w0_ref, wdu_ref, a0_ref, wau_ref, wgu_ref, kk_ref, ka_ref, rk_ref,
                 at_ref, rt_ref, bt_ref, kt_ref, bh_ref, kh_ref, v_ref, gam_ref, g_ref, bv_ref):
    i = pl.program_id(0)
    tm = p_ref.shape[0]
    w = RWKV_WIDTH
    p = p_ref[...]
    prev_row = jnp.where(i == 0, 0.0, prev_ref[7:8, :])
    row = lax.broadcasted_iota(jnp.int32, p.shape, 0)
    shifted = jnp.where(row == 0, prev_row, pltpu.roll(p, 1, 0))
    ps = p + (shifted - p) * mu_ref[...]

    r = ps[:, 0:w]
    k = ps[:, w:2 * w]
    v = ps[:, 2 * w:3 * w]
    wd = ps[:, 3 * w:3 * w + DECAY_LORA]
    ad = ps[:, 3 * w + DECAY_LORA:3 * w + DECAY_LORA + AAA_LORA]
    gd = ps[:, 3 * w + DECAY_LORA + AAA_LORA:]

    z = w0_ref[...] + _dot_x3(jnp.tanh(wd), wdu_ref[...])
    logw = -math.exp(-0.5) * jax.nn.sigmoid(z)
    alr = jax.nn.sigmoid(a0_ref[...] + _dot_x3(ad, wau_ref[...]))
    g_ref[...] = _dot_x3(jax.nn.sigmoid(gd), wgu_ref[...])

    bd = _head_sum_matrix()
    kk = k * kk_ref[...]
    kk = kk * lax.rsqrt(jnp.maximum(_head_sums(kk * kk, bd), 1e-24))
    kmod = k * (1.0 + (alr - 1.0) * ka_ref[...])
    b = kk * alr
    bv_ref[...] = _head_sums(r * kmod * rk_ref[...], bd) * v
    v_ref[...] = v.astype(BF16)

    rr = lax.broadcasted_iota(jnp.int32, (tm, tm), 0)
    cc = lax.broadcasted_iota(jnp.int32, (tm, tm), 1)
    tri = ((cc <= rr) & (cc // CHUNK == rr // CHUNK)).astype(BF16)
    l1, l2, l3 = _split(logw, 3)
    cum = _dot(tri, l1) + (_dot(tri, l2) + _dot(tri, l3))
    tot_rows = []
    for c in range(tm // CHUNK):
        last = cum[c * CHUNK + CHUNK - 1:c * CHUNK + CHUNK, :]
        gam_ref[c] = jnp.exp(last)
        tot_rows.append(jnp.broadcast_to(last, (CHUNK, w)))
    tot = jnp.concatenate(tot_rows, axis=0)

    e_neg = jnp.exp(-cum)
    e_rem = jnp.exp(tot - cum)
    at_ref[...] = (-kk * jnp.exp(cum - logw)).astype(BF16)
    rt_ref[...] = (r * jnp.exp(cum)).astype(BF16)
    bt_ref[...] = (b * e_neg).astype(BF16)
    kt_ref[...] = (kmod * e_neg).astype(BF16)
    bh_ref[...] = (b * e_rem).astype(BF16)
    kh_ref[...] = (kmod * e_rem).astype(BF16)


def _prep(p_rwkv, mu, w0, wdu, a0, wau, wgu, k_k, k_a, r_k, tm):
    t = p_rwkv.shape[0]
    w = RWKV_WIDTH
    cp = RWKV_COLS_PAD
    nc = tm // CHUNK
    row = lambda i: (i, 0)
    fixed = lambda i: (0, 0)
    vec = pl.BlockSpec((1, w), fixed)
    big_bf = jax.ShapeDtypeStruct((t, w), BF16)
    big_f32 = jax.ShapeDtypeStruct((t, w), F32)
    out_tile = pl.BlockSpec((tm, w), row)
    return pl.pallas_call(
        _prep_kernel,
        grid=(t // tm,),
        in_specs=[
            pl.BlockSpec((tm, cp), row),
            pl.BlockSpec((8, cp), lambda i: (jnp.maximum(i * (tm // 8) - 1, 0), 0)),
            pl.BlockSpec((1, cp), fixed),
            vec, pl.BlockSpec((DECAY_LORA, w), fixed),
            vec, pl.BlockSpec((AAA_LORA, w), fixed),
            pl.BlockSpec((GATE_LORA_PAD, w), fixed),
            vec, vec, vec,
        ],
        out_specs=[out_tile] * 7 + [pl.BlockSpec((nc, 1, w), lambda i: (i, 0, 0)), out_tile, out_tile],
        out_shape=[big_bf] * 7 + [jax.ShapeDtypeStruct((t // CHUNK, 1, w), F32), big_f32, big_f32],
        compiler_params=_params("parallel"),
        name="rwkv_prep",
    )(p_rwkv, p_rwkv, mu, w0, wdu, a0, wau, wgu, k_k, k_a, r_k)


def _chunk_factors(nchunks, own, strict, incl, eye, at_ref, rt_ref, bt_ref, kt_ref, bh_ref, kh_ref, v_ref, gam_ref,
                   tick):
    js = range(nchunks)
    n = 2 * CHUNK

    def stack(ref):
        xs = [ref[pl.ds(j * CHUNK, CHUNK), :] for j in js]
        return [jnp.where(own, jnp.concatenate([x, x], axis=0), jnp.zeros((), x.dtype)) for x in xs]

    a_s, r_s, b_s, k_s = stack(at_ref), stack(rt_ref), stack(bt_ref), stack(kt_ref)
    bh_s, kh_s, v_s = stack(bh_ref), stack(kh_ref), stack(v_ref)

    prod = [_dot_nt(jnp.concatenate([a_s[j], r_s[j]], axis=0), jnp.concatenate([b_s[j], k_s[j]], axis=0)) for j in js]
    a_ak = [jnp.where(strict, prod[j][:n, n:], 0.0).astype(BF16) for j in js]
    a_rb = [jnp.where(incl, prod[j][n:, :n], 0.0).astype(BF16) for j in js]
    a_rk = [jnp.where(incl, prod[j][n:, n:], 0.0).astype(BF16) for j in js]

    tick()
    pq = [jnp.concatenate([a_s[j].astype(F32), _dot(a_ak[j], v_s[j])], axis=1) for j in js]
    nk_b = [jnp.where(strict, prod[j][:n, :n], 0.0).astype(BF16) for j in js]
    span = 1
    while True:
        tick()
        pq = [pq[j] + _dot(nk_b[j], pq[j].astype(BF16)) for j in js]
        span *= 2
        if span >= CHUNK:
            break
        tick()
        nk_b = [_dot(nk_b[j], nk_b[j]).astype(BF16) for j in js]
    pq_b = [pq[j].astype(BF16) for j in js]

    zero = jnp.zeros((n, LANES), BF16)
    rhs = [jnp.concatenate([pq_b[j], jnp.concatenate([zero, v_s[j]], axis=1)], axis=0) for j in js]
    mn = [_dot_tn(jnp.concatenate([bh_s[j], kh_s[j]], axis=0), rhs[j]) for j in js]
    yy = [_dot(jnp.concatenate([a_rb[j], a_rk[j]], axis=1), rhs[j]) for j in js]
    out = []
    for j in js:
        m_mat = mn[j][:, :LANES] + jnp.where(eye, jnp.broadcast_to(gam_ref[j], (LANES, LANES)), 0.0)
        n_mat = mn[j][:, LANES:]
        y_c = r_s[j].astype(F32) + yy[j][:, :LANES]
        y_n = yy[j][:, LANES:]
        out.append((m_mat, n_mat, y_c, y_n))
    return out


def _scan_kernel(at_ref, rt_ref, bt_ref, kt_ref, bh_ref, kh_ref, v_ref, gam_ref, g_ref, bv_ref, gnw_ref, gnb_ref,
                 y_ref, h_ref, f_ref):
    @pl.when(pl.program_id(1) == 0)
    def _():
        h_ref[...] = jnp.zeros_like(h_ref)
        f_ref[...] = jnp.zeros_like(f_ref)

    chain = {"h": h_ref[...], "ys": []}

    def tick():
        j = len(chain["ys"])
        if j < SCAN_CHUNKS:
            h = chain["h"].astype(BF16)
            y_st = _dot(f_ref[j, 2].astype(BF16), h) + f_ref[j, 3]
            chain["h"] = _dot(f_ref[j, 0].astype(BF16), h) + f_ref[j, 1]
            chain["ys"].append(y_st[:CHUNK] + y_st[CHUNK:])

    n = 2 * CHUNK
    own = (lax.broadcasted_iota(jnp.int32, (n, LANES), 1) // HEAD_DIM
           == lax.broadcasted_iota(jnp.int32, (n, LANES), 0) // CHUNK)
    ti = lax.broadcasted_iota(jnp.int32, (n, n), 0)
    si = lax.broadcasted_iota(jnp.int32, (n, n), 1)
    same = (ti // CHUNK) == (si // CHUNK)
    strict = same & (si < ti)
    incl = same & (si <= ti)
    eye = ti == si

    factors = _chunk_factors(SCAN_CHUNKS, own, strict, incl, eye, at_ref, rt_ref, bt_ref, kt_ref, bh_ref, kh_ref,
                             v_ref, gam_ref, tick)
    while len(chain["ys"]) < SCAN_CHUNKS:
        tick()
    h_ref[...] = chain["h"]
    y = jnp.concatenate(chain["ys"], axis=0)

    low = lax.broadcasted_iota(jnp.int32, y.shape, 1) < HEAD_DIM

    def head_mean(x):
        s_low = jnp.sum(jnp.where(low, x, 0.0), axis=-1, keepdims=True)
        s_all = jnp.sum(x, axis=-1, keepdims=True)
        return jnp.where(low, s_low, s_all - s_low) * (1.0 / HEAD_DIM)

    yc = y - head_mean(y)
    var = head_mean(yc * yc)
    yn = yc * lax.rsqrt(var + RWKV_GN_EPS) * gnw_ref[...] + gnb_ref[...]
    y_ref[...] = ((yn + bv_ref[...]) * g_ref[...]).astype(BF16)

    for j, mats in enumerate(factors):
        for i, mat in enumerate(mats):
            f_ref[j, i] = mat


def _scan(at, rt, bt, kt, bh, kh, v, gam, g, bv, gn_w, gn_b):
    t = at.shape[0]
    rows = SCAN_CHUNKS * CHUNK
    steps = t // rows
    tile = pl.BlockSpec((rows, LANES), lambda p, s: (jnp.minimum(s, steps - 1), p))
    lagged = pl.BlockSpec((rows, LANES), lambda p, s: (jnp.maximum(s - 1, 0), p))
    vec = pl.BlockSpec((1, LANES), lambda p, s: (0, p))
    gam_spec = pl.BlockSpec((SCAN_CHUNKS, 1, LANES), lambda p, s: (jnp.minimum(s, steps - 1), 0, p))
    return pl.pallas_call(
        _scan_kernel,
        grid=(RWKV_WIDTH // LANES, steps + 1),
        in_specs=[tile] * 7 + [gam_spec, lagged, lagged, vec, vec],
        out_specs=lagged,
        out_shape=jax.ShapeDtypeStruct((t, RWKV_WIDTH), BF16),
        scratch_shapes=[pltpu.VMEM((LANES, LANES), F32), pltpu.VMEM((SCAN_CHUNKS, 4, 2 * CHUNK, LANES), F32)],
        compiler_params=_params("parallel", "arbitrary"),
        name="rwkv_scan",
    )(at, rt, bt, kt, bh, kh, v, gam, g, bv, gn_w, gn_b)


ATTN_BLOCKS = 1


def _attn_kernel(sink_ref, qkv_ref, pos_ref, freq_ref, o_ref, kprev_ref, vprev_ref):
    step = pl.program_id(0)
    rows = ATTN_BLOCKS * WINDOW

    @pl.when(step == 0)
    def _():
        kprev_ref[...] = jnp.zeros_like(kprev_ref)
        vprev_ref[...] = jnp.zeros_like(vprev_ref)

    ang = pos_ref[...].astype(F32) * freq_ref[...]
    cos = jnp.cos(ang)
    lane = lax.broadcasted_iota(jnp.int32, ang.shape, 1)
    first_half = lane % HEAD_DIM < HEAD_DIM // 2
    sin = jnp.where(first_half, -1.0, 1.0) * jnp.sin(ang)

    def rope(x):
        swapped = jnp.where(first_half, pltpu.roll(x, LANES - HEAD_DIM // 2, 1), pltpu.roll(x, HEAD_DIM // 2, 1))
        return x * cos + swapped * sin

    scale = HEAD_DIM ** -0.5
    qi = lax.broadcasted_iota(jnp.int32, (WINDOW, 2 * WINDOW), 0) + WINDOW
    ki = lax.broadcasted_iota(jnp.int32, (WINDOW, 2 * WINDOW), 1)
    rel = qi - ki
    local = (rel >= 0) & (rel < WINDOW)
    masks = [local & ((step > 0) | (ki >= WINDOW))] + [local] * (ATTN_BLOCKS - 1)

    k_cur = jnp.concatenate([rope(qkv_ref[:, ATTN_WIDTH + j * LANES:ATTN_WIDTH + (j + 1) * LANES].astype(F32))
                             for j in range(ATTN_KV_WIDTH // LANES)], axis=1)
    v_cur = qkv_ref[:, ATTN_WIDTH + ATTN_KV_WIDTH:].astype(F32)
    k_all = jnp.concatenate([kprev_ref[...], k_cur], axis=0)
    v_all = jnp.concatenate([vprev_ref[...], v_cur], axis=0)
    k_heads = [k_all[:, g * HEAD_DIM:(g + 1) * HEAD_DIM].astype(BF16) for g in range(ATTN_KV_HEADS)]
    v_heads = [v_all[:, g * HEAD_DIM:(g + 1) * HEAD_DIM].astype(BF16) for g in range(ATTN_KV_HEADS)]

    heads = range(ATTN_WIDTH // HEAD_DIM)
    per_group = LANES // HEAD_DIM
    units = [(h, u) for u in range(ATTN_BLOCKS) for h in heads]
    q_groups = [rope(qkv_ref[:, j * LANES:(j + 1) * LANES].astype(F32)) * scale for j in range(ATTN_WIDTH // LANES)]

    def q_of(h, u):
        cols = slice((h % per_group) * HEAD_DIM, (h % per_group + 1) * HEAD_DIM)
        return q_groups[h // per_group][u * WINDOW:(u + 1) * WINDOW, cols].astype(BF16)

    def band(xs, h, u):
        return xs[h // ATTN_GROUP][u * WINDOW:(u + 2) * WINDOW]

    s = {hu: jnp.where(masks[hu[1]], _dot_nt(q_of(*hu), band(k_heads, *hu)), -jnp.inf) for hu in units}
    probs = {}
    for h, u in units:
        sink = sink_ref[h]
        m = jnp.maximum(jnp.max(s[h, u], axis=-1, keepdims=True), sink)
        pexp = jnp.exp(s[h, u] - m)
        denom = jnp.sum(pexp, axis=-1, keepdims=True) + jnp.exp(sink - m)
        probs[h, u] = (pexp * (1.0 / denom)).astype(BF16)
    o = {hu: _dot(probs[hu], band(v_heads, *hu)) for hu in units}
    for u in range(ATTN_BLOCKS):
        for j in range(ATTN_WIDTH // LANES):
            o_ref[u * WINDOW:(u + 1) * WINDOW, j * LANES:(j + 1) * LANES] = jnp.concatenate(
                [o[h, u] for h in range(j * per_group, (j + 1) * per_group)], axis=1).astype(BF16)

    kprev_ref[...] = k_cur[rows - WINDOW:]
    vprev_ref[...] = v_cur[rows - WINDOW:]


def _attn(qkv, pos_col, freq, sinks):
    t = qkv.shape[0]
    rows = ATTN_BLOCKS * WINDOW
    return pl.pallas_call(
        _attn_kernel,
        grid_spec=pltpu.PrefetchScalarGridSpec(
            num_scalar_prefetch=1,
            grid=(t // rows,),
            in_specs=[
                pl.BlockSpec((rows, ATTN_COLS), lambda i, s: (i, 0)),
                pl.BlockSpec((rows, 1), lambda i, s: (i, 0)),
                pl.BlockSpec((1, LANES), lambda i, s: (0, 0)),
            ],
            out_specs=pl.BlockSpec((rows, ATTN_WIDTH), lambda i, s: (i, 0)),
            scratch_shapes=[pltpu.VMEM((WINDOW, ATTN_KV_WIDTH), F32), pltpu.VMEM((WINDOW, ATTN_KV_WIDTH), F32)],
        ),
        out_shape=jax.ShapeDtypeStruct((t, ATTN_WIDTH), BF16),
        compiler_params=_params("arbitrary"),
        name="swa_attn",
    )(sinks, qkv, pos_col, freq)


def _route_stages(lg, run, out):
    tm = lg.shape[0]
    lane = lax.broadcasted_iota(jnp.int32, lg.shape, 1)
    lane_f = lane.astype(F32)
    ninf = -jnp.inf

    def top(vals):
        best = jnp.max(vals, axis=-1, keepdims=True)
        idx = jnp.min(jnp.where(vals == best, lane_f, float(LANES)), axis=-1, keepdims=True)
        return best, idx

    gl = jnp.where(lane < N_GROUPS, lg, ninf)
    gmax, gidx = top(gl)
    pg_top = 1.0 / jnp.sum(jnp.exp(gl - gmax), axis=-1, keepdims=True)
    yield
    first = N_GROUPS + EXPERTS_PER_GROUP * gidx
    el = jnp.where((lane_f >= first) & (lane_f < first + EXPERTS_PER_GROUP), lg, ninf)
    v0, i0 = top(el)
    yield
    v1, i1 = top(jnp.where(lane_f == i0, ninf, el))
    ex = jnp.exp(v1 - v0)
    g0 = pg_top / (1.0 + ex)
    g1 = pg_top * ex / (1.0 + ex)
    e0 = i0 - N_GROUPS
    e1 = i1 - N_GROUPS
    yield

    hot0 = (lane_f == e0).astype(BF16)
    hot1 = (lane_f == e1).astype(BF16)
    rr = lax.broadcasted_iota(jnp.int32, (tm, tm), 0)
    cc = lax.broadcasted_iota(jnp.int32, (tm, tm), 1)
    before = (cc < rr).astype(BF16)
    cum = _dot(before, jnp.concatenate([hot0, hot1], axis=1))
    h0 = hot0.astype(F32)
    h1 = hot1.astype(F32)
    tot0 = jnp.sum(h0, axis=0, keepdims=True)
    tot1 = jnp.sum(h1, axis=0, keepdims=True)
    rank0 = jnp.sum(h0 * (run + cum[:, :LANES]), axis=-1, keepdims=True)
    rank1 = jnp.sum(h1 * (run + tot0 + cum[:, LANES:]), axis=-1, keepdims=True)
    ints = jnp.where(lane == 0, e0, jnp.where(lane == 1, e1, jnp.where(lane == 2, rank0, jnp.where(lane == 3, rank1, 0.0))))
    out["ints"] = ints.astype(jnp.int32)
    out["gates"] = jnp.where(lane == 0, g0, jnp.where(lane == 1, g1, 0.0))
    out["run"] = run + tot0 + tot1
    yield


OUT_PROJ_SPLIT = 4


def _out_proj_kernel(x_ref, ya_ref, yb_ref, wo_ref, g_ref, wr_ref, br_ref,
                     x1_ref, h2_ref, ri_ref, rg_ref, cnt_ref, run_ref, lg_ref):
    s = pl.program_id(0)

    @pl.when(s == 0)
    def _():
        run_ref[...] = jnp.zeros_like(run_ref)
        lg_ref[...] = jnp.zeros_like(lg_ref)

    routed = {}
    stages = _route_stages(lg_ref[...], run_ref[...], routed)
    ya = ya_ref[...]
    yb = yb_ref[...]
    width = D_MODEL // OUT_PROJ_SPLIT
    ssq = 0.0
    for c in range(OUT_PROJ_SPLIT):
        cols = slice(c * width, (c + 1) * width)
        x1 = x_ref[:, cols] + _dot(ya, wo_ref[:RWKV_WIDTH, cols]) + _dot(yb, wo_ref[RWKV_WIDTH:, cols])
        x1_ref[:, cols] = x1
        ssq = ssq + jnp.sum(x1 * x1, axis=-1, keepdims=True)
        next(stages)

    ri_ref[...] = routed["ints"]
    rg_ref[...] = routed["gates"]
    run = jnp.where(s > 0, routed["run"], run_ref[...])
    run_ref[...] = run
    cnt_ref[...] = run.astype(jnp.int32)

    h2 = x1_ref[...] * lax.rsqrt(ssq * (1.0 / D_MODEL) + RMS_EPS) * g_ref[...]
    h2_ref[...] = h2
    lg_ref[...] = _dot_x3(h2, wr_ref[...]) + br_ref[...]


def _out_proj(x, y_rwkv, y_attn, wo, ln2, wr, br, tm):
    t, d = x.shape
    steps = t // tm
    row = lambda i: (jnp.minimum(i, steps - 1), 0)
    lagged = lambda i: (jnp.maximum(i - 1, 0), 0)
    fixed = lambda i: (0, 0)
    return pl.pallas_call(
        _out_proj_kernel,
        grid=(steps + 1,),
        in_specs=[
            pl.BlockSpec((tm, d), row),
            pl.BlockSpec((tm, RWKV_WIDTH), row),
            pl.BlockSpec((tm, ATTN_WIDTH), row),
            pl.BlockSpec((d, d), fixed),
            pl.BlockSpec((1, d), fixed),
            pl.BlockSpec((d, LANES), fixed),
            pl.BlockSpec((1, LANES), fixed),
        ],
        out_specs=[pl.BlockSpec((tm, d), row), pl.BlockSpec((tm, d), row), pl.BlockSpec((tm, LANES), lagged),
                   pl.BlockSpec((tm, LANES), lagged), pl.BlockSpec((1, LANES), fixed)],
        out_shape=[jax.ShapeDtypeStruct((t, d), F32), jax.ShapeDtypeStruct((t, d), F32),
                   jax.ShapeDtypeStruct((t, LANES), jnp.int32), jax.ShapeDtypeStruct((t, LANES), F32),
                   jax.ShapeDtypeStruct((1, LANES), jnp.int32)],
        scratch_shapes=[pltpu.VMEM((1, LANES), F32), pltpu.VMEM((tm, LANES), F32)],
        compiler_params=_params("arbitrary"),
        name="out_proj_router",
    )(x, y_rwkv, y_attn, wo, ln2, wr, br)


def _for_rows(n, fn):
    full = lax.shift_right_logical(n, 3)

    def group(i, carry):
        for u in range(SUBLANES):
            fn(i, u)
        return carry

    def single(r, carry):
        fn(full, r)
        return carry

    lax.fori_loop(0, full, group, 0)
    lax.fori_loop(0, n - full * SUBLANES, single, 0)


def _moe_kernel(exp_ref, nvalid_ref, first_ref, wslot_ref, next_ref, start_ref, eid_ref, rank_ref,
                h_hbm, wgu_hbm, wdn_hbm, y_hbm, xbuf, obuf, wgu_buf, wdn_buf, tok_ref, dst_ref, gsem, ssem, wsem):
    b = pl.program_id(0)
    last = pl.num_programs(0) - 1
    cur = lax.rem(b, 2)
    nv = nvalid_ref[b]
    ws = wslot_ref[b]
    n_tok = h_hbm.shape[0]

    def weight_copies(e, buf):
        return (pltpu.make_async_copy(wgu_hbm.at[e], wgu_buf.at[buf], wsem.at[buf]),
                pltpu.make_async_copy(wdn_hbm.at[e], wdn_buf.at[buf], wsem.at[buf]))

    def gather_copy(blk, buf, i, u):
        tok = tok_ref[blk * MOE_BLOCK + i * SUBLANES + u]
        return pltpu.make_async_copy(h_hbm.at[pl.ds(tok, 1), :], xbuf.at[buf, i, pl.ds(u, 1), :], gsem.at[buf])

    def scatter_copy(blk, buf, i, u):
        dst = dst_ref[blk * MOE_BLOCK + i * SUBLANES + u]
        return pltpu.make_async_copy(obuf.at[buf, i, pl.ds(u, 1), :], y_hbm.at[pl.ds(dst, 1), :], ssem.at[buf])

    def gather_start(blk, buf):
        _for_rows(nvalid_ref[blk], lambda i, u: gather_copy(blk, buf, i, u).start())

    def gather_wait(blk, buf):
        _for_rows(nvalid_ref[blk], lambda i, u: gather_copy(blk, buf, i, u).wait())

    def scatter_start(blk, buf):
        _for_rows(nvalid_ref[blk], lambda i, u: scatter_copy(blk, buf, i, u).start())

    def scatter_wait(blk, buf):
        _for_rows(nvalid_ref[blk], lambda i, u: scatter_copy(blk, buf, i, u).wait())

    @pl.when(b == 0)
    def _():
        for c in weight_copies(exp_ref[0], 0):
            c.start()

    @pl.when((first_ref[b] == 1) & (next_ref[b] >= 0))
    def _():
        for c in weight_copies(next_ref[b], 1 - ws):
            c.start()

    @pl.when(b == 0)
    def _():
        xbuf[...] = jnp.zeros_like(xbuf)

        def place(i, carry):
            for u in range(SUBLANES):
                a = i * SUBLANES + u
                row = start_ref[eid_ref[a]] + rank_ref[a]
                tok_ref[row] = i * (SUBLANES // TOP_K_INNER) + u // TOP_K_INNER
                dst_ref[row] = (u % TOP_K_INNER) * n_tok + i * (SUBLANES // TOP_K_INNER) + u // TOP_K_INNER
            return carry

        lax.fori_loop(0, eid_ref.shape[0] // SUBLANES, place, 0)
        gather_start(0, 0)

    @pl.when(b < last)
    def _():
        gather_start(b + 1, 1 - cur)

    @pl.when(b >= 2)
    def _():
        scatter_wait(b - 2, cur)

    def experts(rows):
        tiles = rows // SUBLANES
        x = xbuf[cur, :tiles].reshape(rows, D_MODEL).astype(BF16)
        out = obuf.at[cur, :tiles]
        step = D_EXPERT // EXPERT_SPLIT
        for j in range(EXPERT_SPLIT):
            hg = _dot(x, wgu_buf[ws, :, j * step:(j + 1) * step].astype(BF16))
            hu = _dot(x, wgu_buf[ws, :, D_EXPERT + j * step:D_EXPERT + (j + 1) * step].astype(BF16))
            act = (hg * jax.nn.sigmoid(hg) * hu).astype(BF16)
            part = _dot(act, wdn_buf[ws, j * step:(j + 1) * step, :].astype(BF16))
            part = part.reshape(tiles, SUBLANES, D_MODEL)
            if j == 0:
                out[...] = part
            else:
                out[...] += part

    @pl.when(nv > 0)
    def _():
        gather_wait(b, cur)

        @pl.when(first_ref[b] == 1)
        def _():
            for c in weight_copies(exp_ref[b], ws):
                c.wait()

        @pl.when(nv > MOE_BLOCK // 2)
        def _():
            experts(MOE_BLOCK)

        @pl.when(nv <= MOE_BLOCK // 2)
        def _():
            experts(MOE_BLOCK // 2)

        scatter_start(b, cur)

    @pl.when(b == last)
    def _():
        @pl.when(b >= 1)
        def _():
            scatter_wait(b - 1, 1 - cur)

        scatter_wait(b, cur)


def _moe(h2, w_gu, w_dn, tables, pad_start, eid, rank):
    t, d = h2.shape
    n_blocks = tables[0].shape[0]
    hbm = pl.BlockSpec(memory_space=pl.ANY)
    return pl.pallas_call(
        _moe_kernel,
        grid_spec=pltpu.PrefetchScalarGridSpec(
            num_scalar_prefetch=len(tables) + 3,
            grid=(n_blocks,),
            in_specs=[hbm, hbm, hbm],
            out_specs=hbm,
            scratch_shapes=[
                pltpu.VMEM((2, MOE_BLOCK // SUBLANES, SUBLANES, d), F32),
                pltpu.VMEM((2, MOE_BLOCK // SUBLANES, SUBLANES, d), F32),
                pltpu.VMEM((2, d, 2 * D_EXPERT), F32),
                pltpu.VMEM((2, D_EXPERT, d), F32),
                pltpu.SMEM((n_blocks * MOE_BLOCK,), jnp.int32),
                pltpu.SMEM((n_blocks * MOE_BLOCK,), jnp.int32),
                pltpu.SemaphoreType.DMA((2,)),
                pltpu.SemaphoreType.DMA((2,)),
                pltpu.SemaphoreType.DMA((2,)),
            ],
        ),
        out_shape=jax.ShapeDtypeStruct((TOP_K_INNER * t, d), F32),
        compiler_params=_params("arbitrary"),
        name="moe_experts",
    )(*tables, pad_start, eid, rank, h2, w_gu, w_dn)


def _block_tables(counts, m_assign):
    padded = ((counts + MOE_BLOCK - 1) // MOE_BLOCK) * MOE_BLOCK
    pad_end = jnp.cumsum(padded)
    pad_start = pad_end - padded
    n_blocks = -(-m_assign // MOE_BLOCK) + N_EXPERTS
    blk_start = jnp.arange(n_blocks, dtype=jnp.int32) * MOE_BLOCK
    blk_exp = jnp.minimum(jnp.sum(pad_end[None, :] <= blk_start[:, None], axis=1), N_EXPERTS - 1).astype(jnp.int32)
    nvalid = jnp.clip(counts[blk_exp] - (blk_start - pad_start[blk_exp]), 0, MOE_BLOCK)
    nvalid = jnp.where(blk_start < pad_end[-1], nvalid, 0).astype(jnp.int32)
    prev_exp = jnp.concatenate([jnp.full((1,), -1, jnp.int32), blk_exp[:-1]])
    first = ((nvalid > 0) & (blk_exp != prev_exp)).astype(jnp.int32)
    wslot = jnp.maximum(jnp.cumsum(first) - 1, 0) % 2
    ids = jnp.arange(N_EXPERTS, dtype=jnp.int32)
    later = jnp.where((ids[None, :] > ids[:, None]) & (counts[None, :] > 0), ids[None, :], N_EXPERTS)
    next_of = jnp.min(later, axis=1)
    next_exp = jnp.where(next_of == N_EXPERTS, -1, next_of)[blk_exp]
    tables = tuple(a.astype(jnp.int32) for a in (blk_exp, nvalid, first, wslot, next_exp))
    return tables, pad_start.astype(jnp.int32)


def _final_kernel(x_ref, ya_ref, yb_ref, rg_ref, g_ref, o_ref):
    x = x_ref[...] + rg_ref[:, 0:1] * ya_ref[...] + rg_ref[:, 1:2] * yb_ref[...]
    o_ref[...] = _rms(x, g_ref[...])


def _final(x1, y2, gates, gain, tm):
    t, d = x1.shape
    nt = t // tm
    return pl.pallas_call(
        _final_kernel,
        grid=(nt,),
        in_specs=[
            pl.BlockSpec((tm, d), lambda i: (i, 0)),
            pl.BlockSpec((tm, d), lambda i: (i, 0)),
            pl.BlockSpec((tm, d), lambda i: (i + nt, 0)),
            pl.BlockSpec((tm, LANES), lambda i: (i, 0)),
            pl.BlockSpec((1, d), lambda i: (0, 0)),
        ],
        out_specs=pl.BlockSpec((tm, d), lambda i: (i, 0)),
        out_shape=jax.ShapeDtypeStruct((t, d), F32),
        compiler_params=_params("parallel"),
        name="final_norm",
    )(x1, y2, y2, gates, gain)


def _pad_cols(a, n):
    return jnp.pad(a, ((0, 0), (0, n - a.shape[1])))


def _rwkv_mix(p_rwkv, mu, w0, wdu, a0, wau, wgu, k_k, k_a, r_k, gn_w, gn_b):
    row = lambda a: a.reshape(1, -1)
    outs = _prep(p_rwkv, _pad_cols(row(mu), RWKV_COLS_PAD), row(w0), wdu, row(a0), wau,
                 jnp.pad(wgu, ((0, GATE_LORA_PAD - GATE_LORA), (0, 0))), row(k_k), row(k_a), row(r_k), tm=256)
    return _scan(*outs, row(gn_w), row(gn_b))


def _mix_and_route(x, y_rwkv, y_attn, w_out, ln2, w_rg, b_rg, w_re, b_re):
    row = lambda a: a.reshape(1, -1)
    wr = _pad_cols(jnp.concatenate([w_rg, w_re], axis=1), LANES)
    br = _pad_cols(row(jnp.concatenate([b_rg, b_re])), LANES)
    return _out_proj(x, y_rwkv, y_attn, w_out.astype(BF16), row(ln2), wr, br, tm=256)


def _experts(h2, route_ints, counts, w_gu, w_dn):
    t = h2.shape[0]
    tables, pad_start = _block_tables(counts[0, :N_EXPERTS], t * TOP_K_INNER)
    eid = route_ints[:, 0:TOP_K_INNER].reshape(-1)
    rank = route_ints[:, TOP_K_INNER:2 * TOP_K_INNER].reshape(-1)
    return _moe(h2, w_gu, w_dn, tables, pad_start, eid, rank)


def _layer(x, positions, ln1, w_in, mu, w0, wdu, a0, wau, wgu, k_k, k_a, r_k, gn_w, gn_b, sinks, w_out, ln2,
           w_rg, b_rg, w_re, b_re, w_gu, w_dn, ln_f):
    t = x.shape[0]
    row = lambda a: a.reshape(1, -1)
    w_bf = w_in.astype(BF16)
    p_rwkv, h1 = _in_proj(x, row(ln1), w_bf, tm=min(1024, t), tn=RWKV_COLS_PAD // 3, n=RWKV_COLS_PAD, out_dtype=F32)
    qkv = _proj(h1, w_bf[:, RWKV_COLS:], tm=min(1024, t), out_dtype=BF16)

    y_rwkv = _rwkv_mix(p_rwkv, mu, w0, wdu, a0, wau, wgu, k_k, k_a, r_k, gn_w, gn_b)

    inv_freq = ROPE_THETA ** (-jnp.arange(0, HEAD_DIM, 2, dtype=F32) / HEAD_DIM)
    freq = jnp.tile(inv_freq, 2 * LANES // HEAD_DIM).reshape(1, LANES)
    y_attn = _attn(qkv, positions.reshape(t, 1), freq, sinks)

    x1, h2, route_ints, gates, counts = _mix_and_route(x, y_rwkv, y_attn, w_out, ln2, w_rg, b_rg, w_re, b_re)
    y2 = _experts(h2, route_ints, counts, w_gu, w_dn)
    return _final(x1, y2, gates, row(ln_f), tm=min(512, t))


def kernel(x, positions, ln1, w_in, mu_shift, w_decay0, w_decay_up, a0, w_a_up, w_g_up, k_k, k_a, r_k, gn_w, gn_b,
           sinks, w_out, ln2, w_router_group, b_router_group, w_router_expert, b_router_expert, w_expert_gu,
           w_expert_down, ln_f):
    assert ln1.shape[0] == 1, "one trunk layer"
    outs = [
        _layer(x[i], positions[i], ln1[0], w_in[0], mu_shift[0], w_decay0[0], w_decay_up[0], a0[0], w_a_up[0],
               w_g_up[0], k_k[0], k_a[0], r_k[0], gn_w[0], gn_b[0], sinks[0], w_out[0], ln2[0], w_router_group[0],
               b_router_group[0], w_router_expert[0], b_router_expert[0], w_expert_gu[0], w_expert_down[0], ln_f)
        for i in range(x.shape[0])
    ]
    return jnp.stack(outs, axis=0)
```

```python
import math

import jax
import jax.numpy as jnp
from jax import lax
from jax.experimental import pallas as pl
from jax.experimental.pallas import tpu as pltpu

D_MODEL = 2048
HEAD_DIM = 64
RWKV_WIDTH = 1024
ATTN_WIDTH = 1024
ATTN_KV_HEADS = 4
ATTN_GROUP = 4
ATTN_KV_WIDTH = 256
WINDOW = 128
ROPE_THETA = 10000.0
DECAY_LORA = 64
AAA_LORA = 64
GATE_LORA = 160
GATE_LORA_PAD = 256
RWKV_COLS = 3 * RWKV_WIDTH + DECAY_LORA + AAA_LORA + GATE_LORA
RWKV_COLS_PAD = 3 * RWKV_WIDTH + DECAY_LORA + AAA_LORA + GATE_LORA_PAD
ATTN_COLS = ATTN_WIDTH + 2 * ATTN_KV_WIDTH
N_GROUPS = 8
EXPERTS_PER_GROUP = 8
N_EXPERTS = 64
TOP_K_INNER = 2
D_EXPERT = 768
MOE_BLOCK = 256
RMS_EPS = 1e-6
RWKV_GN_EPS = 64e-5

LANES = 128
SUBLANES = 8
CHUNK = 64
SCAN_CHUNKS = 8
EXPERT_SPLIT = 3
VMEM_LIMIT = 56 * 1024 * 1024

F32 = jnp.float32
BF16 = jnp.bfloat16


def _dot(a, b):
    return jnp.dot(a, b, preferred_element_type=F32)


def _split(x, parts):
    out = []
    for _ in range(parts - 1):
        hi = x.astype(BF16)
        out.append(hi)
        x = x - hi.astype(F32)
    out.append(x.astype(BF16))
    return out


def _dot_x3(a, b):
    ah, al = _split(a, 2)
    bh, bl = _split(b, 2)
    return _dot(ah, bh) + (_dot(ah, bl) + _dot(al, bh))


def _dot_x3_pre(a, b_hi, b_lo):
    ah, al = _split(a, 2)
    return _dot(ah, b_hi) + (_dot(ah, b_lo) + _dot(al, b_hi))


def _dot_sel(a, sel):
    ah, al = _split(a, 2)
    return _dot(ah, sel) + _dot(al, sel)


def _dot_sel_lhs(sel, b):
    bh, bl = _split(b, 2)
    return _dot(sel, bh) + _dot(sel, bl)


def _dot_nt(a, b):
    return lax.dot_general(a, b, (((1,), (1,)), ((), ())), preferred_element_type=F32)


def _dot_tn(a, b):
    return lax.dot_general(a, b, (((0,), (0,)), ((), ())), preferred_element_type=F32)


def _params(*sem):
    return pltpu.CompilerParams(dimension_semantics=sem, vmem_limit_bytes=VMEM_LIMIT)


def _head_sum_matrix():
    r = lax.broadcasted_iota(jnp.int32, (LANES, LANES), 0) // HEAD_DIM
    c = lax.broadcasted_iota(jnp.int32, (LANES, LANES), 1) // HEAD_DIM
    return (r == c).astype(BF16)


def _head_sums(x, bd):
    parts = [_dot_sel(x[:, j * LANES:(j + 1) * LANES], bd) for j in range(x.shape[1] // LANES)]
    return jnp.concatenate(parts, axis=1)


def _rms(x, gain):
    return x * lax.rsqrt(jnp.mean(x * x, axis=-1, keepdims=True) + RMS_EPS) * gain


def _in_proj_kernel(x_ref, g_ref, w_ref, o_ref, h_ref):
    @pl.when(pl.program_id(1) == 0)
    def _():
        h_ref[...] = _rms(x_ref[...], g_ref[...]).astype(BF16)

    o_ref[...] = _dot(h_ref[...], w_ref[...]).astype(o_ref.dtype)


def _in_proj(x, gain, w, tm, tn, n, out_dtype):
    t, d = x.shape
    return pl.pallas_call(
        _in_proj_kernel,
        grid=(t // tm, n // tn),
        in_specs=[
            pl.BlockSpec((tm, d), lambda i, j: (i, 0)),
            pl.BlockSpec((1, d), lambda i, j: (0, 0)),
            pl.BlockSpec((d, tn), lambda i, j: (0, j)),
        ],
        out_specs=pl.BlockSpec((tm, tn), lambda i, j: (i, j)),
        out_shape=jax.ShapeDtypeStruct((t, n), out_dtype),
        scratch_shapes=[pltpu.VMEM((tm, d), BF16)],
        compiler_params=_params("parallel", "arbitrary"),
        name="in_proj",
    )(x, gain, w)


def _prep_kernel(p_ref, prev_ref, mu_ref, w0_ref, wdu_ref, wdl_ref, a0_ref, wau_ref, wal_ref, wgu_ref, wgl_ref,
                 kk_ref, ka_ref, rk_ref,
                 at_ref, rt_ref, bt_ref, kt_ref, bh_ref, kh_ref, v_ref, gam_ref, g_ref, bv_ref):
    i = pl.program_id(0)
    tm = p_ref.shape[0]
    w = RWKV_WIDTH
    p = p_ref[...]
    prev_row = jnp.where(i == 0, 0.0, prev_ref[7:8, :])
    rolled = pltpu.roll(p, 1, 0)
    row = lax.broadcasted_iota(jnp.int32, (SUBLANES, p.shape[1]), 0)
    shifted = jnp.concatenate([jnp.where(row == 0, prev_row, rolled[:SUBLANES]), rolled[SUBLANES:]], axis=0)
    ps = p + (shifted - p) * mu_ref[...]

    r = ps[:, 0:w]
    k = ps[:, w:2 * w]
    v = ps[:, 2 * w:3 * w]
    wd = ps[:, 3 * w:3 * w + DECAY_LORA]
    ad = ps[:, 3 * w + DECAY_LORA:3 * w + DECAY_LORA + AAA_LORA]
    gd = ps[:, 3 * w + DECAY_LORA + AAA_LORA:]

    z = w0_ref[...] + _dot_x3_pre(jnp.tanh(wd), wdu_ref[...], wdl_ref[...])
    logw = -math.exp(-0.5) * jax.nn.sigmoid(z)
    alr = jax.nn.sigmoid(a0_ref[...] + _dot_x3_pre(ad, wau_ref[...], wal_ref[...]))
    g_ref[...] = _dot_x3_pre(jax.nn.sigmoid(gd), wgu_ref[...], wgl_ref[...])

    bd = _head_sum_matrix()
    kk = k * kk_ref[...]
    kk = kk * lax.rsqrt(jnp.maximum(_head_sums(kk * kk, bd), 1e-24))
    kmod = k * (1.0 + (alr - 1.0) * ka_ref[...])
    b = kk * alr
    bv_ref[...] = _head_sums(r * kmod * rk_ref[...], bd) * v
    v_ref[...] = v.astype(BF16)

    rr = lax.broadcasted_iota(jnp.int32, (tm, tm), 0)
    cc = lax.broadcasted_iota(jnp.int32, (tm, tm), 1)
    tri = ((cc <= rr) & (cc // CHUNK == rr // CHUNK)).astype(BF16)
    cum = _dot_sel_lhs(tri, logw)
    tot_rows = []
    for c in range(tm // CHUNK):
        last = cum[c * CHUNK + CHUNK - 1:c * CHUNK + CHUNK, :]
        gam_ref[c] = jnp.exp(last)
        tot_rows.append(jnp.broadcast_to(last, (CHUNK, w)))
    tot = jnp.concatenate(tot_rows, axis=0)

    e_neg = jnp.exp(-cum)
    e_rem = jnp.exp(tot - cum)
    at_ref[...] = (-kk * jnp.exp(cum - logw)).astype(BF16)
    rt_ref[...] = (r * jnp.exp(cum)).astype(BF16)
    bt_ref[...] = (b * e_neg).astype(BF16)
    kt_ref[...] = (kmod * e_neg).astype(BF16)
    bh_ref[...] = (b * e_rem).astype(BF16)
    kh_ref[...] = (kmod * e_rem).astype(BF16)


def _hi_lo(w):
    hi = w.astype(BF16)
    return hi, (w - hi.astype(F32)).astype(BF16)


def _prep(p_rwkv, mu, w0, wdu, a0, wau, wgu, k_k, k_a, r_k, tm):
    t = p_rwkv.shape[0]
    w = RWKV_WIDTH
    cp = RWKV_COLS_PAD
    nc = tm // CHUNK
    row = lambda i: (i, 0)
    fixed = lambda i: (0, 0)
    vec = pl.BlockSpec((1, w), fixed)
    big_bf = jax.ShapeDtypeStruct((t, w), BF16)
    big_f32 = jax.ShapeDtypeStruct((t, w), F32)
    out_tile = pl.BlockSpec((tm, w), row)
    return pl.pallas_call(
        _prep_kernel,
        grid=(t // tm,),
        in_specs=[
            pl.BlockSpec((tm, cp), row),
            pl.BlockSpec((8, cp), lambda i: (jnp.maximum(i * (tm // 8) - 1, 0), 0)),
            pl.BlockSpec((1, cp), fixed),
            vec, pl.BlockSpec((DECAY_LORA, w), fixed), pl.BlockSpec((DECAY_LORA, w), fixed),
            vec, pl.BlockSpec((AAA_LORA, w), fixed), pl.BlockSpec((AAA_LORA, w), fixed),
            pl.BlockSpec((GATE_LORA_PAD, w), fixed), pl.BlockSpec((GATE_LORA_PAD, w), fixed),
            vec, vec, vec,
        ],
        out_specs=[out_tile] * 7 + [pl.BlockSpec((nc, 1, w), lambda i: (i, 0, 0)), out_tile, out_tile],
        out_shape=[big_bf] * 7 + [jax.ShapeDtypeStruct((t // CHUNK, 1, w), F32), big_f32, big_f32],
        compiler_params=_params("parallel"),
        name="rwkv_prep",
    )(p_rwkv, p_rwkv, mu, w0, *_hi_lo(wdu), a0, *_hi_lo(wau), *_hi_lo(wgu), k_k, k_a, r_k)


def _chunk_factors(nchunks, own, strict, incl, eye, at_ref, rt_ref, bt_ref, kt_ref, bh_ref, kh_ref, v_ref, gam_ref,
                   tick):
    js = range(nchunks)
    n = 2 * CHUNK

    def stack(ref):
        xs = [ref[pl.ds(j * CHUNK, CHUNK), :] for j in js]
        return [jnp.where(own, jnp.concatenate([x, x], axis=0), jnp.zeros((), x.dtype)) for x in xs]

    a_s, r_s, b_s, k_s = stack(at_ref), stack(rt_ref), stack(bt_ref), stack(kt_ref)
    bh_s, kh_s, v_s = stack(bh_ref), stack(kh_ref), stack(v_ref)

    prod = [_dot_nt(jnp.concatenate([a_s[j], r_s[j]], axis=0), jnp.concatenate([b_s[j], k_s[j]], axis=0)) for j in js]
    a_ak = [jnp.where(strict, prod[j][:n, n:], 0.0).astype(BF16) for j in js]
    a_rb = [jnp.where(incl, prod[j][n:, :n], 0.0).astype(BF16) for j in js]
    a_rk = [jnp.where(incl, prod[j][n:, n:], 0.0).astype(BF16) for j in js]

    tick()
    pq = [jnp.concatenate([a_s[j].astype(F32), _dot(a_ak[j], v_s[j])], axis=1) for j in js]
    nk_b = [jnp.where(strict, prod[j][:n, :n], 0.0).astype(BF16) for j in js]
    span = 1
    while True:
        tick()
        pq = [pq[j] + _dot(nk_b[j], pq[j].astype(BF16)) for j in js]
        span *= 2
        if span >= CHUNK:
            break
        tick()
        nk_b = [_dot(nk_b[j], nk_b[j]).astype(BF16) for j in js]
    pq_b = [pq[j].astype(BF16) for j in js]

    zero = jnp.zeros((n, LANES), BF16)
    rhs = [jnp.concatenate([pq_b[j], jnp.concatenate([zero, v_s[j]], axis=1)], axis=0) for j in js]
    mn = [_dot_tn(jnp.concatenate([bh_s[j], kh_s[j]], axis=0), rhs[j]) for j in js]
    yy = [_dot(jnp.concatenate([a_rb[j], a_rk[j]], axis=1), rhs[j]) for j in js]
    out = []
    for j in js:
        m_mat = mn[j][:, :LANES] + jnp.where(eye, jnp.broadcast_to(gam_ref[j], (LANES, LANES)), 0.0)
        n_mat = mn[j][:, LANES:]
        y_c = r_s[j].astype(F32) + yy[j][:, :LANES]
        y_n = yy[j][:, LANES:]
        out.append((m_mat, n_mat, y_c, y_n))
    return out


def _scan_kernel(at_ref, rt_ref, bt_ref, kt_ref, bh_ref, kh_ref, v_ref, gam_ref, g_ref, bv_ref, gnw_ref, gnb_ref,
                 y_ref, h_ref, f_ref):
    @pl.when(pl.program_id(1) == 0)
    def _():
        h_ref[...] = jnp.zeros_like(h_ref)
        f_ref[...] = jnp.zeros_like(f_ref)

    chain = {"h": h_ref[...], "ys": []}

    def tick():
        j = len(chain["ys"])
        if j < SCAN_CHUNKS:
            h = chain["h"].astype(BF16)
            y_st = _dot(f_ref[j, 2].astype(BF16), h) + f_ref[j, 3]
            chain["h"] = _dot(f_ref[j, 0].astype(BF16), h) + f_ref[j, 1]
            chain["ys"].append(y_st[:CHUNK] + y_st[CHUNK:])

    n = 2 * CHUNK
    own = (lax.broadcasted_iota(jnp.int32, (n, LANES), 1) // HEAD_DIM
           == lax.broadcasted_iota(jnp.int32, (n, LANES), 0) // CHUNK)
    ti = lax.broadcasted_iota(jnp.int32, (n, n), 0)
    si = lax.broadcasted_iota(jnp.int32, (n, n), 1)
    same = (ti // CHUNK) == (si // CHUNK)
    strict = same & (si < ti)
    incl = same & (si <= ti)
    eye = ti == si

    factors = _chunk_factors(SCAN_CHUNKS, own, strict, incl, eye, at_ref, rt_ref, bt_ref, kt_ref, bh_ref, kh_ref,
                             v_ref, gam_ref, tick)
    while len(chain["ys"]) < SCAN_CHUNKS:
        tick()
    h_ref[...] = chain["h"]
    y = jnp.concatenate(chain["ys"], axis=0)

    low = lax.broadcasted_iota(jnp.int32, y.shape, 1) < HEAD_DIM

    def head_mean(x):
        s_low = jnp.sum(jnp.where(low, x, 0.0), axis=-1, keepdims=True)
        s_all = jnp.sum(x, axis=-1, keepdims=True)
        return jnp.where(low, s_low, s_all - s_low) * (1.0 / HEAD_DIM)

    yc = y - head_mean(y)
    var = head_mean(yc * yc)
    yn = yc * lax.rsqrt(var + RWKV_GN_EPS) * gnw_ref[...] + gnb_ref[...]
    y_ref[...] = ((yn + bv_ref[...]) * g_ref[...]).astype(BF16)

    for j, mats in enumerate(factors):
        for i, mat in enumerate(mats):
            f_ref[j, i] = mat


def _scan(at, rt, bt, kt, bh, kh, v, gam, g, bv, gn_w, gn_b):
    t = at.shape[0]
    rows = SCAN_CHUNKS * CHUNK
    steps = t // rows
    tile = pl.BlockSpec((rows, LANES), lambda p, s: (jnp.minimum(s, steps - 1), p))
    lagged = pl.BlockSpec((rows, LANES), lambda p, s: (jnp.maximum(s - 1, 0), p))
    vec = pl.BlockSpec((1, LANES), lambda p, s: (0, p))
    gam_spec = pl.BlockSpec((SCAN_CHUNKS, 1, LANES), lambda p, s: (jnp.minimum(s, steps - 1), 0, p))
    return pl.pallas_call(
        _scan_kernel,
        grid=(RWKV_WIDTH // LANES, steps + 1),
        in_specs=[tile] * 7 + [gam_spec, lagged, lagged, vec, vec],
        out_specs=lagged,
        out_shape=jax.ShapeDtypeStruct((t, RWKV_WIDTH), BF16),
        scratch_shapes=[pltpu.VMEM((LANES, LANES), F32), pltpu.VMEM((SCAN_CHUNKS, 4, 2 * CHUNK, LANES), F32)],
        compiler_params=_params("parallel", "arbitrary"),
        name="rwkv_scan",
    )(at, rt, bt, kt, bh, kh, v, gam, g, bv, gn_w, gn_b)


ATTN_BLOCKS = 1


def _attn_kernel(sink_ref, qkv_ref, pos_ref, freq_ref, o_ref, kprev_ref, vprev_ref):
    step = pl.program_id(0)
    rows = ATTN_BLOCKS * WINDOW

    @pl.when(step == 0)
    def _():
        kprev_ref[...] = jnp.zeros_like(kprev_ref)
        vprev_ref[...] = jnp.zeros_like(vprev_ref)

    ang = pos_ref[...].astype(F32) * freq_ref[...]
    cos = jnp.cos(ang)
    lane = lax.broadcasted_iota(jnp.int32, ang.shape, 1)
    first_half = lane % HEAD_DIM < HEAD_DIM // 2
    sin = jnp.where(first_half, -1.0, 1.0) * jnp.sin(ang)

    def rope(x):
        swapped = jnp.where(first_half, pltpu.roll(x, LANES - HEAD_DIM // 2, 1), pltpu.roll(x, HEAD_DIM // 2, 1))
        return x * cos + swapped * sin

    scale = HEAD_DIM ** -0.5
    qi = lax.broadcasted_iota(jnp.int32, (WINDOW, 2 * WINDOW), 0) + WINDOW
    ki = lax.broadcasted_iota(jnp.int32, (WINDOW, 2 * WINDOW), 1)
    rel = qi - ki
    local = (rel >= 0) & (rel < WINDOW)
    masks = [local & ((step > 0) | (ki >= WINDOW))] + [local] * (ATTN_BLOCKS - 1)

    k_cur = jnp.concatenate([rope(qkv_ref[:, ATTN_WIDTH + j * LANES:ATTN_WIDTH + (j + 1) * LANES].astype(F32))
                             for j in range(ATTN_KV_WIDTH // LANES)], axis=1)
    v_cur = qkv_ref[:, ATTN_WIDTH + ATTN_KV_WIDTH:].astype(F32)
    k_all = jnp.concatenate([kprev_ref[...], k_cur], axis=0)
    v_all = jnp.concatenate([vprev_ref[...], v_cur], axis=0)
    k_heads = [k_all[:, g * HEAD_DIM:(g + 1) * HEAD_DIM].astype(BF16) for g in range(ATTN_KV_HEADS)]
    v_heads = [v_all[:, g * HEAD_DIM:(g + 1) * HEAD_DIM].astype(BF16) for g in range(ATTN_KV_HEADS)]

    heads = range(ATTN_WIDTH // HEAD_DIM)
    per_group = LANES // HEAD_DIM
    units = [(h, u) for u in range(ATTN_BLOCKS) for h in heads]
    q_groups = [rope(qkv_ref[:, j * LANES:(j + 1) * LANES].astype(F32)) * scale for j in range(ATTN_WIDTH // LANES)]

    def q_of(h, u):
        cols = slice((h % per_group) * HEAD_DIM, (h % per_group + 1) * HEAD_DIM)
        return q_groups[h // per_group][u * WINDOW:(u + 1) * WINDOW, cols].astype(BF16)

    def band(xs, h, u):
        return xs[h // ATTN_GROUP][u * WINDOW:(u + 2) * WINDOW]

    s = {hu: jnp.where(masks[hu[1]], _dot_nt(q_of(*hu), band(k_heads, *hu)), -jnp.inf) for hu in units}
    probs = {}
    for h, u in units:
        sink = sink_ref[h]
        m = jnp.maximum(jnp.max(s[h, u], axis=-1, keepdims=True), sink)
        pexp = jnp.exp(s[h, u] - m)
        denom = jnp.sum(pexp, axis=-1, keepdims=True) + jnp.exp(sink - m)
        probs[h, u] = (pexp * (1.0 / denom)).astype(BF16)
    o = {hu: _dot(probs[hu], band(v_heads, *hu)) for hu in units}
    for u in range(ATTN_BLOCKS):
        for j in range(ATTN_WIDTH // LANES):
            o_ref[u * WINDOW:(u + 1) * WINDOW, j * LANES:(j + 1) * LANES] = jnp.concatenate(
                [o[h, u] for h in range(j * per_group, (j + 1) * per_group)], axis=1).astype(BF16)

    kprev_ref[...] = k_cur[rows - WINDOW:]
    vprev_ref[...] = v_cur[rows - WINDOW:]


def _attn(qkv, pos_col, freq, sinks):
    t = qkv.shape[0]
    rows = ATTN_BLOCKS * WINDOW
    return pl.pallas_call(
        _attn_kernel,
        grid_spec=pltpu.PrefetchScalarGridSpec(
            num_scalar_prefetch=1,
            grid=(t // rows,),
            in_specs=[
                pl.BlockSpec((rows, ATTN_COLS), lambda i, s: (i, 0)),
                pl.BlockSpec((rows, 1), lambda i, s: (i, 0)),
                pl.BlockSpec((1, LANES), lambda i, s: (0, 0)),
            ],
            out_specs=pl.BlockSpec((rows, ATTN_WIDTH), lambda i, s: (i, 0)),
            scratch_shapes=[pltpu.VMEM((WINDOW, ATTN_KV_WIDTH), F32), pltpu.VMEM((WINDOW, ATTN_KV_WIDTH), F32)],
        ),
        out_shape=jax.ShapeDtypeStruct((t, ATTN_WIDTH), BF16),
        compiler_params=_params("arbitrary"),
        name="swa_attn",
    )(sinks, qkv, pos_col, freq)


def _route_stages(lg, run, out):
    tm = lg.shape[0]
    lane = lax.broadcasted_iota(jnp.int32, lg.shape, 1)
    lane_f = lane.astype(F32)
    ninf = -jnp.inf

    def top(vals):
        best = jnp.max(vals, axis=-1, keepdims=True)
        idx = jnp.min(jnp.where(vals == best, lane_f, float(LANES)), axis=-1, keepdims=True)
        return best, idx

    gl = jnp.where(lane < N_GROUPS, lg, ninf)
    gmax, gidx = top(gl)
    pg_top = 1.0 / jnp.sum(jnp.exp(gl - gmax), axis=-1, keepdims=True)
    yield
    first = N_GROUPS + EXPERTS_PER_GROUP * gidx
    el = jnp.where((lane_f >= first) & (lane_f < first + EXPERTS_PER_GROUP), lg, ninf)
    v0, i0 = top(el)
    yield
    v1, i1 = top(jnp.where(lane_f == i0, ninf, el))
    ex = jnp.exp(v1 - v0)
    g0 = pg_top / (1.0 + ex)
    g1 = pg_top * ex / (1.0 + ex)
    e0 = i0 - N_GROUPS
    e1 = i1 - N_GROUPS
    yield

    hot0 = (lane_f == e0).astype(BF16)
    hot1 = (lane_f == e1).astype(BF16)
    rr = lax.broadcasted_iota(jnp.int32, (tm, tm), 0)
    cc = lax.broadcasted_iota(jnp.int32, (tm, tm), 1)
    before = (cc < rr).astype(BF16)
    cum = _dot(before, jnp.concatenate([hot0, hot1], axis=1))
    h0 = hot0.astype(F32)
    h1 = hot1.astype(F32)
    tot0 = jnp.sum(h0, axis=0, keepdims=True)
    tot1 = jnp.sum(h1, axis=0, keepdims=True)
    rank0 = jnp.sum(h0 * (run + cum[:, :LANES]), axis=-1, keepdims=True)
    rank1 = jnp.sum(h1 * (run + tot0 + cum[:, LANES:]), axis=-1, keepdims=True)
    ints = jnp.where(lane == 0, e0, jnp.where(lane == 1, e1, jnp.where(lane == 2, rank0, jnp.where(lane == 3, rank1, 0.0))))
    out["ints"] = ints.astype(jnp.int32)
    out["gates"] = jnp.where(lane == 0, g0, jnp.where(lane == 1, g1, 0.0))
    out["run"] = run + tot0 + tot1
    yield


OUT_PROJ_SPLIT = 4


def _out_proj_kernel(x_ref, ya_ref, yb_ref, wo_ref, g_ref, wr_ref, br_ref,
                     x1_ref, h2_ref, ri_ref, rg_ref, cnt_ref, run_ref, lg_ref):
    s = pl.program_id(0)

    @pl.when(s == 0)
    def _():
        run_ref[...] = jnp.zeros_like(run_ref)
        lg_ref[...] = jnp.zeros_like(lg_ref)

    routed = {}
    stages = _route_stages(lg_ref[...], run_ref[...], routed)
    ya = ya_ref[...]
    yb = yb_ref[...]
    width = D_MODEL // OUT_PROJ_SPLIT
    ssq = 0.0
    for c in range(OUT_PROJ_SPLIT):
        cols = slice(c * width, (c + 1) * width)
        x1 = x_ref[:, cols] + _dot(ya, wo_ref[:RWKV_WIDTH, cols]) + _dot(yb, wo_ref[RWKV_WIDTH:, cols])
        x1_ref[:, cols] = x1
        ssq = ssq + jnp.sum(x1 * x1, axis=-1, keepdims=True)
        next(stages)

    ri_ref[...] = routed["ints"]
    rg_ref[...] = routed["gates"]
    run = jnp.where(s > 0, routed["run"], run_ref[...])
    run_ref[...] = run
    cnt_ref[...] = run.astype(jnp.int32)

    h2 = x1_ref[...] * lax.rsqrt(ssq * (1.0 / D_MODEL) + RMS_EPS) * g_ref[...]
    h2_ref[...] = h2
    lg_ref[...] = _dot_x3(h2, wr_ref[...]) + br_ref[...]


def _out_proj(x, y_rwkv, y_attn, wo, ln2, wr, br, tm):
    t, d = x.shape
    steps = t // tm
    row = lambda i: (jnp.minimum(i, steps - 1), 0)
    lagged = lambda i: (jnp.maximum(i - 1, 0), 0)
    fixed = lambda i: (0, 0)
    return pl.pallas_call(
        _out_proj_kernel,
        grid=(steps + 1,),
        in_specs=[
            pl.BlockSpec((tm, d), row),
            pl.BlockSpec((tm, RWKV_WIDTH), row),
            pl.BlockSpec((tm, ATTN_WIDTH), row),
            pl.BlockSpec((d, d), fixed),
            pl.BlockSpec((1, d), fixed),
            pl.BlockSpec((d, LANES), fixed),
            pl.BlockSpec((1, LANES), fixed),
        ],
        out_specs=[pl.BlockSpec((tm, d), row), pl.BlockSpec((tm, d), row), pl.BlockSpec((tm, LANES), lagged),
                   pl.BlockSpec((tm, LANES), lagged), pl.BlockSpec((1, LANES), fixed)],
        out_shape=[jax.ShapeDtypeStruct((t, d), F32), jax.ShapeDtypeStruct((t, d), F32),
                   jax.ShapeDtypeStruct((t, LANES), jnp.int32), jax.ShapeDtypeStruct((t, LANES), F32),
                   jax.ShapeDtypeStruct((1, LANES), jnp.int32)],
        scratch_shapes=[pltpu.VMEM((1, LANES), F32), pltpu.VMEM((tm, LANES), F32)],
        compiler_params=_params("arbitrary"),
        name="out_proj_router",
    )(x, y_rwkv, y_attn, wo, ln2, wr, br)


def _for_rows(n, fn):
    full = lax.shift_right_logical(n, 3)

    def group(i, carry):
        for u in range(SUBLANES):
            fn(i, u)
        return carry

    def single(r, carry):
        fn(full, r)
        return carry

    lax.fori_loop(0, full, group, 0)
    lax.fori_loop(0, n - full * SUBLANES, single, 0)


def _moe_kernel(exp_ref, nvalid_ref, first_ref, wslot_ref, next_ref, start_ref, eid_ref, rank_ref,
                h_hbm, wgu_hbm, wdn_hbm, y_hbm, xbuf, obuf, wgu_buf, wdn_buf, tok_ref, dst_ref, gsem, ssem, wsem):
    b = pl.program_id(0)
    last = pl.num_programs(0) - 1
    cur = lax.rem(b, 2)
    nv = nvalid_ref[b]
    ws = wslot_ref[b]
    n_tok = h_hbm.shape[0]

    def weight_copies(e, buf):
        return (pltpu.make_async_copy(wgu_hbm.at[e], wgu_buf.at[buf], wsem.at[buf]),
                pltpu.make_async_copy(wdn_hbm.at[e], wdn_buf.at[buf], wsem.at[buf]))

    def gather_copy(blk, buf, i, u):
        tok = tok_ref[blk * MOE_BLOCK + i * SUBLANES + u]
        return pltpu.make_async_copy(h_hbm.at[pl.ds(tok, 1), :], xbuf.at[buf, i, pl.ds(u, 1), :], gsem.at[buf])

    def scatter_copy(blk, buf, i, u):
        dst = dst_ref[blk * MOE_BLOCK + i * SUBLANES + u]
        return pltpu.make_async_copy(obuf.at[buf, i, pl.ds(u, 1), :], y_hbm.at[pl.ds(dst, 1), :], ssem.at[buf])

    def gather_start(blk, buf):
        _for_rows(nvalid_ref[blk], lambda i, u: gather_copy(blk, buf, i, u).start())

    def gather_wait(blk, buf):
        _for_rows(nvalid_ref[blk], lambda i, u: gather_copy(blk, buf, i, u).wait())

    def scatter_start(blk, buf):
        _for_rows(nvalid_ref[blk], lambda i, u: scatter_copy(blk, buf, i, u).start())

    def scatter_wait(blk, buf):
        _for_rows(nvalid_ref[blk], lambda i, u: scatter_copy(blk, buf, i, u).wait())

    @pl.when(b == 0)
    def _():
        for c in weight_copies(exp_ref[0], 0):
            c.start()

    @pl.when((first_ref[b] == 1) & (next_ref[b] >= 0))
    def _():
        for c in weight_copies(next_ref[b], 1 - ws):
            c.start()

    @pl.when(b == 0)
    def _():
        xbuf[...] = jnp.zeros_like(xbuf)

        def place(i, carry):
            for u in range(SUBLANES):
                a = i * SUBLANES + u
                row = start_ref[eid_ref[a]] + rank_ref[a]
                tok_ref[row] = i * (SUBLANES // TOP_K_INNER) + u // TOP_K_INNER
                dst_ref[row] = (u % TOP_K_INNER) * n_tok + i * (SUBLANES // TOP_K_INNER) + u // TOP_K_INNER
            return carry

        lax.fori_loop(0, eid_ref.shape[0] // SUBLANES, place, 0)
        gather_start(0, 0)

    @pl.when(b < last)
    def _():
        gather_start(b + 1, 1 - cur)

    @pl.when(b >= 2)
    def _():
        scatter_wait(b - 2, cur)

    def experts(rows):
        tiles = rows // SUBLANES
        x = xbuf[cur, :tiles].reshape(rows, D_MODEL).astype(BF16)
        out = obuf.at[cur, :tiles]
        step = D_EXPERT // EXPERT_SPLIT
        for j in range(EXPERT_SPLIT):
            hg = _dot(x, wgu_buf[ws, :, j * step:(j + 1) * step].astype(BF16))
            hu = _dot(x, wgu_buf[ws, :, D_EXPERT + j * step:D_EXPERT + (j + 1) * step].astype(BF16))
            act = (hg * jax.nn.sigmoid(hg) * hu).astype(BF16)
            part = _dot(act, wdn_buf[ws, j * step:(j + 1) * step, :].astype(BF16))
            part = part.reshape(tiles, SUBLANES, D_MODEL)
            if j == 0:
                out[...] = part
            else:
                out[...] += part

    @pl.when(nv > 0)
    def _():
        gather_wait(b, cur)

        @pl.when(first_ref[b] == 1)
        def _():
            for c in weight_copies(exp_ref[b], ws):
                c.wait()

        @pl.when(nv > MOE_BLOCK // 2)
        def _():
            experts(MOE_BLOCK)

        @pl.when(nv <= MOE_BLOCK // 2)
        def _():
            experts(MOE_BLOCK // 2)

        scatter_start(b, cur)

    @pl.when(b == last)
    def _():
        @pl.when(b >= 1)
        def _():
            scatter_wait(b - 1, 1 - cur)

        scatter_wait(b, cur)


def _moe(h2, w_gu, w_dn, tables, pad_start, eid, rank):
    t, d = h2.shape
    n_blocks = tables[0].shape[0]
    hbm = pl.BlockSpec(memory_space=pl.ANY)
    return pl.pallas_call(
        _moe_kernel,
        grid_spec=pltpu.PrefetchScalarGridSpec(
            num_scalar_prefetch=len(tables) + 3,
            grid=(n_blocks,),
            in_specs=[hbm, hbm, hbm],
            out_specs=hbm,
            scratch_shapes=[
                pltpu.VMEM((2, MOE_BLOCK // SUBLANES, SUBLANES, d), F32),
                pltpu.VMEM((2, MOE_BLOCK // SUBLANES, SUBLANES, d), F32),
                pltpu.VMEM((2, d, 2 * D_EXPERT), F32),
                pltpu.VMEM((2, D_EXPERT, d), F32),
                pltpu.SMEM((n_blocks * MOE_BLOCK,), jnp.int32),
                pltpu.SMEM((n_blocks * MOE_BLOCK,), jnp.int32),
                pltpu.SemaphoreType.DMA((2,)),
                pltpu.SemaphoreType.DMA((2,)),
                pltpu.SemaphoreType.DMA((2,)),
            ],
        ),
        out_shape=jax.ShapeDtypeStruct((TOP_K_INNER * t, d), F32),
        compiler_params=_params("arbitrary"),
        name="moe_experts",
    )(*tables, pad_start, eid, rank, h2, w_gu, w_dn)


def _block_tables(counts, m_assign):
    padded = ((counts + MOE_BLOCK - 1) // MOE_BLOCK) * MOE_BLOCK
    pad_end = jnp.cumsum(padded)
    pad_start = pad_end - padded
    n_blocks = -(-m_assign // MOE_BLOCK) + N_EXPERTS
    blk_start = jnp.arange(n_blocks, dtype=jnp.int32) * MOE_BLOCK
    blk_exp = jnp.minimum(jnp.sum(pad_end[None, :] <= blk_start[:, None], axis=1), N_EXPERTS - 1).astype(jnp.int32)
    nvalid = jnp.clip(counts[blk_exp] - (blk_start - pad_start[blk_exp]), 0, MOE_BLOCK)
    nvalid = jnp.where(blk_start < pad_end[-1], nvalid, 0).astype(jnp.int32)
    prev_exp = jnp.concatenate([jnp.full((1,), -1, jnp.int32), blk_exp[:-1]])
    first = ((nvalid > 0) & (blk_exp != prev_exp)).astype(jnp.int32)
    wslot = jnp.maximum(jnp.cumsum(first) - 1, 0) % 2
    ids = jnp.arange(N_EXPERTS, dtype=jnp.int32)
    later = jnp.where((ids[None, :] > ids[:, None]) & (counts[None, :] > 0), ids[None, :], N_EXPERTS)
    next_of = jnp.min(later, axis=1)
    next_exp = jnp.where(next_of == N_EXPERTS, -1, next_of)[blk_exp]
    tables = tuple(a.astype(jnp.int32) for a in (blk_exp, nvalid, first, wslot, next_exp))
    return tables, pad_start.astype(jnp.int32)


def _final_kernel(x_ref, ya_ref, yb_ref, rg_ref, g_ref, o_ref):
    x = x_ref[...] + rg_ref[:, 0:1] * ya_ref[...] + rg_ref[:, 1:2] * yb_ref[...]
    o_ref[...] = _rms(x, g_ref[...])


def _final(x1, y2, gates, gain, tm):
    t, d = x1.shape
    nt = t // tm
    return pl.pallas_call(
        _final_kernel,
        grid=(nt,),
        in_specs=[
            pl.BlockSpec((tm, d), lambda i: (i, 0)),
            pl.BlockSpec((tm, d), lambda i: (i, 0)),
            pl.BlockSpec((tm, d), lambda i: (i + nt, 0)),
            pl.BlockSpec((tm, LANES), lambda i: (i, 0)),
            pl.BlockSpec((1, d), lambda i: (0, 0)),
        ],
        out_specs=pl.BlockSpec((tm, d), lambda i: (i, 0)),
        out_shape=jax.ShapeDtypeStruct((t, d), F32),
        compiler_params=_params("parallel"),
        name="final_norm",
    )(x1, y2, y2, gates, gain)


def _pad_cols(a, n):
    return jnp.pad(a, ((0, 0), (0, n - a.shape[1])))


def _rwkv_mix(p_rwkv, mu, w0, wdu, a0, wau, wgu, k_k, k_a, r_k, gn_w, gn_b):
    row = lambda a: a.reshape(1, -1)
    outs = _prep(p_rwkv, _pad_cols(row(mu), RWKV_COLS_PAD), row(w0), wdu, row(a0), wau,
                 jnp.pad(wgu, ((0, GATE_LORA_PAD - GATE_LORA), (0, 0))), row(k_k), row(k_a), row(r_k), tm=256)
    return _scan(*outs, row(gn_w), row(gn_b))


def _mix_and_route(x, y_rwkv, y_attn, w_out, ln2, w_rg, b_rg, w_re, b_re):
    row = lambda a: a.reshape(1, -1)
    wr = _pad_cols(jnp.concatenate([w_rg, w_re], axis=1), LANES)
    br = _pad_cols(row(jnp.concatenate([b_rg, b_re])), LANES)
    return _out_proj(x, y_rwkv, y_attn, w_out.astype(BF16), row(ln2), wr, br, tm=256)


def _experts(h2, route_ints, counts, w_gu, w_dn):
    t = h2.shape[0]
    tables, pad_start = _block_tables(counts[0, :N_EXPERTS], t * TOP_K_INNER)
    eid = route_ints[:, 0:TOP_K_INNER].reshape(-1)
    rank = route_ints[:, TOP_K_INNER:2 * TOP_K_INNER].reshape(-1)
    return _moe(h2, w_gu, w_dn, tables, pad_start, eid, rank)


def _layer(x, positions, ln1, w_in, mu, w0, wdu, a0, wau, wgu, k_k, k_a, r_k, gn_w, gn_b, sinks, w_out, ln2,
           w_rg, b_rg, w_re, b_re, w_gu, w_dn, ln_f):
    t = x.shape[0]
    row = lambda a: a.reshape(1, -1)
    w_bf = w_in.astype(BF16)
    p_rwkv = _in_proj(x, row(ln1), w_bf, tm=min(1024, t), tn=RWKV_COLS_PAD // 3, n=RWKV_COLS_PAD, out_dtype=F32)
    qkv = _in_proj(x, row(ln1), w_bf[:, RWKV_COLS:], tm=min(1024, t), tn=ATTN_COLS, n=ATTN_COLS, out_dtype=BF16)

    y_rwkv = _rwkv_mix(p_rwkv, mu, w0, wdu, a0, wau, wgu, k_k, k_a, r_k, gn_w, gn_b)

    inv_freq = ROPE_THETA ** (-jnp.arange(0, HEAD_DIM, 2, dtype=F32) / HEAD_DIM)
    freq = jnp.tile(inv_freq, 2 * LANES // HEAD_DIM).reshape(1, LANES)
    y_attn = _attn(qkv, positions.reshape(t, 1), freq, sinks)

    x1, h2, route_ints, gates, counts = _mix_and_route(x, y_rwkv, y_attn, w_out, ln2, w_rg, b_rg, w_re, b_re)
    y2 = _experts(h2, route_ints, counts, w_gu, w_dn)
    return _final(x1, y2, gates, row(ln_f), tm=min(512, t))


def kernel(x, positions, ln1, w_in, mu_shift, w_decay0, w_decay_up, a0, w_a_up, w_g_up, k_k, k_a, r_k, gn_w, gn_b,
           sinks, w_out, ln2, w_router_group, b_router_group, w_router_expert, b_router_expert, w_expert_gu,
           w_expert_down, ln_f):
    assert ln1.shape[0] == 1, "one trunk layer"
    outs = [
        _layer(x[i], positions[i], ln1[0], w_in[0], mu_shift[0], w_decay0[0], w_decay_up[0], a0[0], w_a_up[0],
               w_g_up[0], k_k[0], k_a[0], r_k[0], gn_w[0], gn_b[0], sinks[0], w_out[0], ln2[0], w_router_group[0],
               b_router_group[0], w_router_expert[0], b_router_expert[0], w_expert_gu[0], w_expert_down[0], ln_f)
        for i in range(x.shape[0])
    ]
    return jnp.stack(outs, axis=0)
```

```python
import math

import jax
import jax.numpy as jnp
from jax import lax
from jax.experimental import pallas as pl
from jax.experimental.pallas import tpu as pltpu

D_MODEL = 2048
HEAD_DIM = 64
RWKV_WIDTH = 1024
ATTN_WIDTH = 1024
ATTN_KV_HEADS = 4
ATTN_GROUP = 4
ATTN_KV_WIDTH = 256
WINDOW = 128
ROPE_THETA = 10000.0
DECAY_LORA = 64
AAA_LORA = 64
GATE_LORA = 160
GATE_LORA_PAD = 256
RWKV_COLS = 3 * RWKV_WIDTH + DECAY_LORA + AAA_LORA + GATE_LORA
RWKV_COLS_PAD = 3 * RWKV_WIDTH + DECAY_LORA + AAA_LORA + GATE_LORA_PAD
ATTN_COLS = ATTN_WIDTH + 2 * ATTN_KV_WIDTH
N_GROUPS = 8
EXPERTS_PER_GROUP = 8
N_EXPERTS = 64
TOP_K_INNER = 2
D_EXPERT = 768
MOE_BLOCK = 256
RMS_EPS = 1e-6
RWKV_GN_EPS = 64e-5

LANES = 128
SUBLANES = 8
CHUNK = 64
SCAN_CHUNKS = 8
EXPERT_SPLIT = 3
VMEM_LIMIT = 56 * 1024 * 1024

F32 = jnp.float32
BF16 = jnp.bfloat16


def _dot(a, b):
    return jnp.dot(a, b, preferred_element_type=F32)


def _split(x, parts):
    out = []
    for _ in range(parts - 1):
        hi = x.astype(BF16)
        out.append(hi)
        x = x - hi.astype(F32)
    out.append(x.astype(BF16))
    return out


def _dot_x3(a, b):
    ah, al = _split(a, 2)
    bh, bl = _split(b, 2)
    return _dot(ah, bh) + (_dot(ah, bl) + _dot(al, bh))


def _dot_x3_pre(a, b_hi, b_lo):
    ah, al = _split(a, 2)
    return _dot(ah, b_hi) + (_dot(ah, b_lo) + _dot(al, b_hi))


def _dot_sel(a, sel):
    ah, al = _split(a, 2)
    return _dot(ah, sel) + _dot(al, sel)


def _dot_sel_lhs(sel, b):
    bh, bl = _split(b, 2)
    return _dot(sel, bh) + _dot(sel, bl)


def _dot_nt(a, b):
    return lax.dot_general(a, b, (((1,), (1,)), ((), ())), preferred_element_type=F32)


def _dot_tn(a, b):
    return lax.dot_general(a, b, (((0,), (0,)), ((), ())), preferred_element_type=F32)


def _params(*sem):
    return pltpu.CompilerParams(dimension_semantics=sem, vmem_limit_bytes=VMEM_LIMIT)


def _head_sum_matrix():
    r = lax.broadcasted_iota(jnp.int32, (LANES, LANES), 0) // HEAD_DIM
    c = lax.broadcasted_iota(jnp.int32, (LANES, LANES), 1) // HEAD_DIM
    return (r == c).astype(BF16)


def _head_sums(x, bd):
    xb = x.astype(BF16)
    parts = [_dot(xb[:, j * LANES:(j + 1) * LANES], bd) for j in range(x.shape[1] // LANES)]
    return jnp.concatenate(parts, axis=1)


def _rms(x, gain):
    return x * lax.rsqrt(jnp.mean(x * x, axis=-1, keepdims=True) + RMS_EPS) * gain


IN_PROJ_ROW_CHUNKS = 4


def _in_proj_kernel(x_ref, g_ref, w_ref, o_ref, h_ref):
    j = pl.program_id(1)
    rows = x_ref.shape[0] // IN_PROJ_ROW_CHUNKS

    @pl.when(j == 0)
    def _():
        for c in range(IN_PROJ_ROW_CHUNKS):
            sl = pl.ds(c * rows, rows)
            h = _rms(x_ref[sl, :], g_ref[...]).astype(BF16)
            h_ref[sl, :] = h
            o_ref[sl, :] = _dot(h, w_ref[...]).astype(o_ref.dtype)

    @pl.when(j > 0)
    def _():
        o_ref[...] = _dot(h_ref[...], w_ref[...]).astype(o_ref.dtype)


def _in_proj(x, gain, w, tm, tn, n, out_dtype):
    t, d = x.shape
    return pl.pallas_call(
        _in_proj_kernel,
        grid=(t // tm, n // tn),
        in_specs=[
            pl.BlockSpec((tm, d), lambda i, j: (i, 0)),
            pl.BlockSpec((1, d), lambda i, j: (0, 0)),
            pl.BlockSpec((d, tn), lambda i, j: (0, j)),
        ],
        out_specs=pl.BlockSpec((tm, tn), lambda i, j: (i, j)),
        out_shape=jax.ShapeDtypeStruct((t, n), out_dtype),
        scratch_shapes=[pltpu.VMEM((tm, d), BF16)],
        compiler_params=_params("parallel", "arbitrary"),
        name="in_proj",
    )(x, gain, w)


def _prep_kernel(p_ref, prev_ref, mu_ref, w0_ref, wdu_ref, wdl_ref, a0_ref, wau_ref, wal_ref, wgu_ref, wgl_ref,
                 kk_ref, ka_ref, rk_ref,
                 at_ref, rt_ref, bt_ref, kt_ref, bh_ref, kh_ref, v_ref, gam_ref, g_ref, bv_ref):
    i = pl.program_id(0)
    tm = p_ref.shape[0]
    w = RWKV_WIDTH
    p = p_ref[...]
    prev_row = jnp.where(i == 0, 0.0, prev_ref[7:8, :])
    rolled = pltpu.roll(p, 1, 0)
    row = lax.broadcasted_iota(jnp.int32, (SUBLANES, p.shape[1]), 0)
    shifted = jnp.concatenate([jnp.where(row == 0, prev_row, rolled[:SUBLANES]), rolled[SUBLANES:]], axis=0)
    ps = p + (shifted - p) * mu_ref[...]

    r = ps[:, 0:w]
    k = ps[:, w:2 * w]
    v = ps[:, 2 * w:3 * w]
    wd = ps[:, 3 * w:3 * w + DECAY_LORA]
    ad = ps[:, 3 * w + DECAY_LORA:3 * w + DECAY_LORA + AAA_LORA]
    gd = ps[:, 3 * w + DECAY_LORA + AAA_LORA:]

    z = w0_ref[...] + _dot_x3_pre(jnp.tanh(wd), wdu_ref[...], wdl_ref[...])
    logw = -math.exp(-0.5) * jax.nn.sigmoid(z)
    alr = jax.nn.sigmoid(a0_ref[...] + _dot_x3_pre(ad, wau_ref[...], wal_ref[...]))
    g_ref[...] = _dot_x3_pre(jax.nn.sigmoid(gd), wgu_ref[...], wgl_ref[...])

    bd = _head_sum_matrix()
    kk = k * kk_ref[...]
    kk = kk * lax.rsqrt(jnp.maximum(_head_sums(kk * kk, bd), 1e-24))
    kmod = k * (1.0 + (alr - 1.0) * ka_ref[...])
    b = kk * alr
    bv_ref[...] = _head_sums(r * kmod * rk_ref[...], bd) * v
    v_ref[...] = v.astype(BF16)

    rr = lax.broadcasted_iota(jnp.int32, (tm, tm), 0)
    cc = lax.broadcasted_iota(jnp.int32, (tm, tm), 1)
    tri = ((cc <= rr) & (cc // CHUNK == rr // CHUNK)).astype(BF16)
    cum = _dot_sel_lhs(tri, logw)
    tot_rows = []
    for c in range(tm // CHUNK):
        last = cum[c * CHUNK + CHUNK - 1:c * CHUNK + CHUNK, :]
        gam_ref[c] = jnp.exp(last)
        tot_rows.append(jnp.broadcast_to(last, (CHUNK, w)))
    tot = jnp.concatenate(tot_rows, axis=0)

    e_neg = jnp.exp(-cum)
    e_rem = jnp.exp(tot - cum)
    at_ref[...] = (-kk * jnp.exp(cum - logw)).astype(BF16)
    rt_ref[...] = (r * jnp.exp(cum)).astype(BF16)
    bt_ref[...] = (b * e_neg).astype(BF16)
    kt_ref[...] = (kmod * e_neg).astype(BF16)
    bh_ref[...] = (b * e_rem).astype(BF16)
    kh_ref[...] = (kmod * e_rem).astype(BF16)


def _hi_lo(w):
    hi = w.astype(BF16)
    return hi, (w - hi.astype(F32)).astype(BF16)


def _prep(p_rwkv, mu, w0, wdu, a0, wau, wgu, k_k, k_a, r_k, tm):
    t = p_rwkv.shape[0]
    w = RWKV_WIDTH
    cp = RWKV_COLS_PAD
    nc = tm // CHUNK
    row = lambda i: (i, 0)
    fixed = lambda i: (0, 0)
    vec = pl.BlockSpec((1, w), fixed)
    big_bf = jax.ShapeDtypeStruct((t, w), BF16)
    big_f32 = jax.ShapeDtypeStruct((t, w), F32)
    out_tile = pl.BlockSpec((tm, w), row)
    return pl.pallas_call(
        _prep_kernel,
        grid=(t // tm,),
        in_specs=[
            pl.BlockSpec((tm, cp), row),
            pl.BlockSpec((8, cp), lambda i: (jnp.maximum(i * (tm // 8) - 1, 0), 0)),
            pl.BlockSpec((1, cp), fixed),
            vec, pl.BlockSpec((DECAY_LORA, w), fixed), pl.BlockSpec((DECAY_LORA, w), fixed),
            vec, pl.BlockSpec((AAA_LORA, w), fixed), pl.BlockSpec((AAA_LORA, w), fixed),
            pl.BlockSpec((GATE_LORA_PAD, w), fixed), pl.BlockSpec((GATE_LORA_PAD, w), fixed),
            vec, vec, vec,
        ],
        out_specs=[out_tile] * 7 + [pl.BlockSpec((nc, 1, w), lambda i: (i, 0, 0)), out_tile, out_tile],
        out_shape=[big_bf] * 7 + [jax.ShapeDtypeStruct((t // CHUNK, 1, w), F32), big_f32, big_f32],
        compiler_params=_params("parallel"),
        name="rwkv_prep",
    )(p_rwkv, p_rwkv, mu, w0, *_hi_lo(wdu), a0, *_hi_lo(wau), *_hi_lo(wgu), k_k, k_a, r_k)


def _chunk_factors(nchunks, own, strict, incl, eye, at_ref, rt_ref, bt_ref, kt_ref, bh_ref, kh_ref, v_ref, gam_ref,
                   tick):
    js = range(nchunks)
    n = 2 * CHUNK

    def stack(ref):
        xs = [ref[pl.ds(j * CHUNK, CHUNK), :] for j in js]
        return [jnp.where(own, jnp.concatenate([x, x], axis=0), jnp.zeros((), x.dtype)) for x in xs]

    a_s, r_s, b_s, k_s = stack(at_ref), stack(rt_ref), stack(bt_ref), stack(kt_ref)
    bh_s, kh_s, v_s = stack(bh_ref), stack(kh_ref), stack(v_ref)

    prod = [_dot_nt(jnp.concatenate([a_s[j], r_s[j]], axis=0), jnp.concatenate([b_s[j], k_s[j]], axis=0)) for j in js]
    a_ak = [jnp.where(strict, prod[j][:n, n:], 0.0).astype(BF16) for j in js]
    a_rb = [jnp.where(incl, prod[j][n:, :n], 0.0).astype(BF16) for j in js]
    a_rk = [jnp.where(incl, prod[j][n:, n:], 0.0).astype(BF16) for j in js]

    tick()
    pq = [jnp.concatenate([a_s[j].astype(F32), _dot(a_ak[j], v_s[j])], axis=1) for j in js]
    nk_b = [jnp.where(strict, prod[j][:n, :n], 0.0).astype(BF16) for j in js]
    span = 1
    while True:
        tick()
        pq = [pq[j] + _dot(nk_b[j], pq[j].astype(BF16)) for j in js]
        span *= 2
        if span >= CHUNK:
            break
        tick()
        nk_b = [_dot(nk_b[j], nk_b[j]).astype(BF16) for j in js]
    pq_b = [pq[j].astype(BF16) for j in js]

    zero = jnp.zeros((n, LANES), BF16)
    rhs = [jnp.concatenate([pq_b[j], jnp.concatenate([zero, v_s[j]], axis=1)], axis=0) for j in js]
    mn = [_dot_tn(jnp.concatenate([bh_s[j], kh_s[j]], axis=0), rhs[j]) for j in js]
    yy = [_dot(jnp.concatenate([a_rb[j], a_rk[j]], axis=1), rhs[j]) for j in js]
    out = []
    for j in js:
        m_mat = mn[j][:, :LANES] + jnp.where(eye, jnp.broadcast_to(gam_ref[j], (LANES, LANES)), 0.0)
        n_mat = mn[j][:, LANES:]
        y_c = r_s[j].astype(F32) + yy[j][:, :LANES]
        y_n = yy[j][:, LANES:]
        out.append((m_mat, n_mat, y_c, y_n))
    return out


def _scan_kernel(at_ref, rt_ref, bt_ref, kt_ref, bh_ref, kh_ref, v_ref, gam_ref, g_ref, bv_ref, gnw_ref, gnb_ref,
                 y_ref, h_ref, f_ref):
    @pl.when(pl.program_id(1) == 0)
    def _():
        h_ref[...] = jnp.zeros_like(h_ref)
        f_ref[...] = jnp.zeros_like(f_ref)

    chain = {"h": h_ref[...], "ys": []}

    def tick():
        j = len(chain["ys"])
        if j < SCAN_CHUNKS:
            h = chain["h"].astype(BF16)
            both = _dot(jnp.concatenate([f_ref[j, 2], f_ref[j, 0]], axis=0).astype(BF16), h)
            y_st = both[:2 * CHUNK] + f_ref[j, 3]
            chain["h"] = both[2 * CHUNK:] + f_ref[j, 1]
            chain["ys"].append(y_st[:CHUNK] + y_st[CHUNK:])

    n = 2 * CHUNK
    own = (lax.broadcasted_iota(jnp.int32, (n, LANES), 1) // HEAD_DIM
           == lax.broadcasted_iota(jnp.int32, (n, LANES), 0) // CHUNK)
    ti = lax.broadcasted_iota(jnp.int32, (n, n), 0)
    si = lax.broadcasted_iota(jnp.int32, (n, n), 1)
    same = (ti // CHUNK) == (si // CHUNK)
    strict = same & (si < ti)
    incl = same & (si <= ti)
    eye = ti == si

    factors = _chunk_factors(SCAN_CHUNKS, own, strict, incl, eye, at_ref, rt_ref, bt_ref, kt_ref, bh_ref, kh_ref,
                             v_ref, gam_ref, tick)
    while len(chain["ys"]) < SCAN_CHUNKS:
        tick()
    h_ref[...] = chain["h"]
    y = jnp.concatenate(chain["ys"], axis=0)

    low = lax.broadcasted_iota(jnp.int32, y.shape, 1) < HEAD_DIM

    def head_mean(x):
        s_low = jnp.sum(jnp.where(low, x, 0.0), axis=-1, keepdims=True)
        s_all = jnp.sum(x, axis=-1, keepdims=True)
        return jnp.where(low, s_low, s_all - s_low) * (1.0 / HEAD_DIM)

    yc = y - head_mean(y)
    var = head_mean(yc * yc)
    yn = yc * lax.rsqrt(var + RWKV_GN_EPS) * gnw_ref[...] + gnb_ref[...]
    y_ref[...] = ((yn + bv_ref[...]) * g_ref[...]).astype(BF16)

    for j, mats in enumerate(factors):
        for i, mat in enumerate(mats):
            f_ref[j, i] = mat


def _scan(at, rt, bt, kt, bh, kh, v, gam, g, bv, gn_w, gn_b):
    t = at.shape[0]
    rows = SCAN_CHUNKS * CHUNK
    steps = t // rows
    tile = pl.BlockSpec((rows, LANES), lambda p, s: (jnp.minimum(s, steps - 1), p))
    lagged = pl.BlockSpec((rows, LANES), lambda p, s: (jnp.maximum(s - 1, 0), p))
    vec = pl.BlockSpec((1, LANES), lambda p, s: (0, p))
    gam_spec = pl.BlockSpec((SCAN_CHUNKS, 1, LANES), lambda p, s: (jnp.minimum(s, steps - 1), 0, p))
    return pl.pallas_call(
        _scan_kernel,
        grid=(RWKV_WIDTH // LANES, steps + 1),
        in_specs=[tile] * 7 + [gam_spec, lagged, lagged, vec, vec],
        out_specs=lagged,
        out_shape=jax.ShapeDtypeStruct((t, RWKV_WIDTH), BF16),
        scratch_shapes=[pltpu.VMEM((LANES, LANES), F32), pltpu.VMEM((SCAN_CHUNKS, 4, 2 * CHUNK, LANES), F32)],
        compiler_params=_params("parallel", "arbitrary"),
        name="rwkv_scan",
    )(at, rt, bt, kt, bh, kh, v, gam, g, bv, gn_w, gn_b)


ATTN_BLOCKS = 1


def _attn_kernel(sink_ref, qkv_ref, pos_ref, freq_ref, o_ref, kprev_ref, vprev_ref):
    step = pl.program_id(0)
    rows = ATTN_BLOCKS * WINDOW

    @pl.when(step == 0)
    def _():
        kprev_ref[...] = jnp.zeros_like(kprev_ref)
        vprev_ref[...] = jnp.zeros_like(vprev_ref)

    quarter = HEAD_DIM // 2
    groups = LANES // quarter
    prow = rows // groups
    pos = pos_ref[...].astype(F32)
    plane = lax.broadcasted_iota(jnp.int32, (prow, LANES), 1) // quarter
    pos_packed = pos[(groups - 1) * prow:, :]
    for b in range(groups - 2, -1, -1):
        pos_packed = jnp.where(plane == b, pos[b * prow:(b + 1) * prow, :], pos_packed)
    ang = pos_packed * freq_ref[...]

    def unpack(packed):
        rot = [packed] + [pltpu.roll(packed, quarter * j, 1) for j in range(1, groups)]
        blocks = []
        for b in range(groups):
            out = rot[(groups - 1 - b) % groups]
            for k in range(groups - 2, -1, -1):
                out = jnp.where(plane == k, rot[(k - b) % groups], out)
            blocks.append(out)
        return jnp.concatenate(blocks, axis=0)

    cos = unpack(jnp.cos(ang))
    lane = lax.broadcasted_iota(jnp.int32, (rows, LANES), 1)
    first_half = lane % HEAD_DIM < HEAD_DIM // 2
    sin = jnp.where(first_half, -1.0, 1.0) * unpack(jnp.sin(ang))

    def rope(x):
        swapped = jnp.where(first_half, pltpu.roll(x, LANES - HEAD_DIM // 2, 1), pltpu.roll(x, HEAD_DIM // 2, 1))
        return x * cos + swapped * sin

    scale = HEAD_DIM ** -0.5
    qi = lax.broadcasted_iota(jnp.int32, (WINDOW, 2 * WINDOW), 0) + WINDOW
    ki = lax.broadcasted_iota(jnp.int32, (WINDOW, 2 * WINDOW), 1)
    rel = qi - ki
    local = (rel >= 0) & (rel < WINDOW)
    masks = [local & ((step > 0) | (ki >= WINDOW))] + [local] * (ATTN_BLOCKS - 1)

    k_cur = jnp.concatenate([rope(qkv_ref[:, ATTN_WIDTH + j * LANES:ATTN_WIDTH + (j + 1) * LANES].astype(F32))
                             for j in range(ATTN_KV_WIDTH // LANES)], axis=1)
    v_cur = qkv_ref[:, ATTN_WIDTH + ATTN_KV_WIDTH:].astype(F32)
    k_all = jnp.concatenate([kprev_ref[...], k_cur], axis=0)
    v_all = jnp.concatenate([vprev_ref[...], v_cur], axis=0)
    k_heads = [k_all[:, g * HEAD_DIM:(g + 1) * HEAD_DIM].astype(BF16) for g in range(ATTN_KV_HEADS)]
    ones = jnp.ones((v_all.shape[0], HEAD_DIM), BF16)
    v_ones = [jnp.concatenate([v_all[:, g * HEAD_DIM:(g + 1) * HEAD_DIM].astype(BF16), ones], axis=1)
              for g in range(ATTN_KV_HEADS)]

    heads = range(ATTN_WIDTH // HEAD_DIM)
    per_group = LANES // HEAD_DIM
    units = [(h, u) for u in range(ATTN_BLOCKS) for h in heads]
    q_groups = [rope(qkv_ref[:, j * LANES:(j + 1) * LANES].astype(F32)) * scale for j in range(ATTN_WIDTH // LANES)]

    def q_of(h, u):
        cols = slice((h % per_group) * HEAD_DIM, (h % per_group + 1) * HEAD_DIM)
        return q_groups[h // per_group][u * WINDOW:(u + 1) * WINDOW, cols].astype(BF16)

    def band(xs, h, u):
        return xs[h // ATTN_GROUP][u * WINDOW:(u + 2) * WINDOW]

    s = {hu: jnp.where(masks[hu[1]], _dot_nt(q_of(*hu), band(k_heads, *hu)), -jnp.inf) for hu in units}
    pexp, shift = {}, {}
    for h, u in units:
        sink = sink_ref[h]
        m = jnp.maximum(jnp.max(s[h, u], axis=-1, keepdims=True), sink)
        pexp[h, u] = jnp.exp(s[h, u] - m).astype(BF16)
        shift[h, u] = jnp.exp(sink - m)
    ov = {hu: _dot(pexp[hu], band(v_ones, *hu)) for hu in units}
    o = {hu: ov[hu][:, :HEAD_DIM] * (1.0 / (ov[hu][:, HEAD_DIM:] + shift[hu])) for hu in units}
    for u in range(ATTN_BLOCKS):
        for j in range(ATTN_WIDTH // LANES):
            o_ref[u * WINDOW:(u + 1) * WINDOW, j * LANES:(j + 1) * LANES] = jnp.concatenate(
                [o[h, u] for h in range(j * per_group, (j + 1) * per_group)], axis=1).astype(BF16)

    kprev_ref[...] = k_cur[rows - WINDOW:]
    vprev_ref[...] = v_cur[rows - WINDOW:]


def _attn(qkv, pos_col, freq, sinks):
    t = qkv.shape[0]
    rows = ATTN_BLOCKS * WINDOW
    return pl.pallas_call(
        _attn_kernel,
        grid_spec=pltpu.PrefetchScalarGridSpec(
            num_scalar_prefetch=1,
            grid=(t // rows,),
            in_specs=[
                pl.BlockSpec((rows, ATTN_COLS), lambda i, s: (i, 0)),
                pl.BlockSpec((rows, 1), lambda i, s: (i, 0)),
                pl.BlockSpec((1, LANES), lambda i, s: (0, 0)),
            ],
            out_specs=pl.BlockSpec((rows, ATTN_WIDTH), lambda i, s: (i, 0)),
            scratch_shapes=[pltpu.VMEM((WINDOW, ATTN_KV_WIDTH), F32), pltpu.VMEM((WINDOW, ATTN_KV_WIDTH), F32)],
        ),
        out_shape=jax.ShapeDtypeStruct((t, ATTN_WIDTH), BF16),
        compiler_params=_params("arbitrary"),
        name="swa_attn",
    )(sinks, qkv, pos_col, freq)


def _route_stages(lg, run, out):
    tm = lg.shape[0]
    lane = lax.broadcasted_iota(jnp.int32, lg.shape, 1)
    lane_f = lane.astype(F32)
    ninf = -jnp.inf

    def top(vals):
        best = jnp.max(vals, axis=-1, keepdims=True)
        idx = jnp.min(jnp.where(vals == best, lane_f, float(LANES)), axis=-1, keepdims=True)
        return best, idx

    gl = jnp.where(lane < N_GROUPS, lg, ninf)
    gmax, gidx = top(gl)
    pg_top = 1.0 / jnp.sum(jnp.exp(gl - gmax), axis=-1, keepdims=True)
    yield
    first = N_GROUPS + EXPERTS_PER_GROUP * gidx
    el = jnp.where((lane_f >= first) & (lane_f < first + EXPERTS_PER_GROUP), lg, ninf)
    v0, i0 = top(el)
    yield
    v1, i1 = top(jnp.where(lane_f == i0, ninf, el))
    ex = jnp.exp(v1 - v0)
    g0 = pg_top / (1.0 + ex)
    g1 = pg_top * ex / (1.0 + ex)
    e0 = i0 - N_GROUPS
    e1 = i1 - N_GROUPS
    yield

    hot0 = (lane_f == e0).astype(BF16)
    hot1 = (lane_f == e1).astype(BF16)
    rr = lax.broadcasted_iota(jnp.int32, (tm, tm), 0)
    cc = lax.broadcasted_iota(jnp.int32, (tm, tm), 1)
    before = (cc < rr).astype(BF16)
    cum = _dot(before, jnp.concatenate([hot0, hot1], axis=1))
    h0 = hot0.astype(F32)
    h1 = hot1.astype(F32)
    tot0 = jnp.sum(h0, axis=0, keepdims=True)
    tot1 = jnp.sum(h1, axis=0, keepdims=True)
    rank0 = jnp.sum(h0 * (run + cum[:, :LANES]), axis=-1, keepdims=True)
    rank1 = jnp.sum(h1 * (run + tot0 + cum[:, LANES:]), axis=-1, keepdims=True)
    ints = jnp.where(lane == 0, e0, jnp.where(lane == 1, e1, jnp.where(lane == 2, rank0, jnp.where(lane == 3, rank1, 0.0))))
    out["ints"] = ints.astype(jnp.int32)
    out["gates"] = jnp.where(lane == 0, g0, jnp.where(lane == 1, g1, 0.0))
    out["run"] = run + tot0 + tot1
    yield


OUT_PROJ_SPLIT = 4


def _out_proj_kernel(x_ref, ya_ref, yb_ref, wo_ref, g_ref, wr_ref, br_ref,
                     x1_ref, h2_ref, ri_ref, rg_ref, cnt_ref, run_ref, lg_ref):
    s = pl.program_id(0)

    @pl.when(s == 0)
    def _():
        run_ref[...] = jnp.zeros_like(run_ref)
        lg_ref[...] = jnp.zeros_like(lg_ref)

    routed = {}
    stages = _route_stages(lg_ref[...], run_ref[...], routed)
    ya = ya_ref[...]
    yb = yb_ref[...]
    width = D_MODEL // OUT_PROJ_SPLIT
    ssq = 0.0
    for c in range(OUT_PROJ_SPLIT):
        cols = slice(c * width, (c + 1) * width)
        x1 = x_ref[:, cols] + _dot(ya, wo_ref[:RWKV_WIDTH, cols]) + _dot(yb, wo_ref[RWKV_WIDTH:, cols])
        x1_ref[:, cols] = x1
        ssq = ssq + jnp.sum(x1 * x1, axis=-1, keepdims=True)
        next(stages)

    ri_ref[...] = routed["ints"]
    rg_ref[...] = routed["gates"]
    run = jnp.where(s > 0, routed["run"], run_ref[...])
    run_ref[...] = run
    cnt_ref[...] = run.astype(jnp.int32)

    h2 = x1_ref[...] * lax.rsqrt(ssq * (1.0 / D_MODEL) + RMS_EPS) * g_ref[...]
    h2_ref[...] = h2
    lg_ref[...] = _dot_x3(h2, wr_ref[...]) + br_ref[...]


def _out_proj(x, y_rwkv, y_attn, wo, ln2, wr, br, tm):
    t, d = x.shape
    steps = t // tm
    row = lambda i: (jnp.minimum(i, steps - 1), 0)
    lagged = lambda i: (jnp.maximum(i - 1, 0), 0)
    fixed = lambda i: (0, 0)
    return pl.pallas_call(
        _out_proj_kernel,
        grid=(steps + 1,),
        in_specs=[
            pl.BlockSpec((tm, d), row),
            pl.BlockSpec((tm, RWKV_WIDTH), row),
            pl.BlockSpec((tm, ATTN_WIDTH), row),
            pl.BlockSpec((d, d), fixed),
            pl.BlockSpec((1, d), fixed),
            pl.BlockSpec((d, LANES), fixed),
            pl.BlockSpec((1, LANES), fixed),
        ],
        out_specs=[pl.BlockSpec((tm, d), row), pl.BlockSpec((tm, d), row), pl.BlockSpec((tm, LANES), lagged),
                   pl.BlockSpec((tm, LANES), lagged), pl.BlockSpec((1, LANES), fixed)],
        out_shape=[jax.ShapeDtypeStruct((t, d), F32), jax.ShapeDtypeStruct((t, d), F32),
                   jax.ShapeDtypeStruct((t, LANES), jnp.int32), jax.ShapeDtypeStruct((t, LANES), F32),
                   jax.ShapeDtypeStruct((1, LANES), jnp.int32)],
        scratch_shapes=[pltpu.VMEM((1, LANES), F32), pltpu.VMEM((tm, LANES), F32)],
        compiler_params=_params("arbitrary"),
        name="out_proj_router",
    )(x, y_rwkv, y_attn, wo, ln2, wr, br)


def _for_rows(n, fn):
    full = lax.shift_right_logical(n, 3)

    def group(i, carry):
        for u in range(SUBLANES):
            fn(i, u)
        return carry

    def single(r, carry):
        fn(full, r)
        return carry

    lax.fori_loop(0, full, group, 0)
    lax.fori_loop(0, n - full * SUBLANES, single, 0)


def _moe_kernel(exp_ref, nvalid_ref, first_ref, wslot_ref, next_ref, start_ref, eid_ref, rank_ref,
                h_hbm, wgu_hbm, wdn_hbm, y_hbm, xbuf, obuf, wgu_buf, wdn_buf, tok_ref, dst_ref, gsem, ssem, wsem):
    b = pl.program_id(0)
    last = pl.num_programs(0) - 1
    cur = lax.rem(b, 2)
    nv = nvalid_ref[b]
    ws = wslot_ref[b]
    n_tok = h_hbm.shape[0]

    def weight_copies(e, buf):
        return (pltpu.make_async_copy(wgu_hbm.at[e], wgu_buf.at[buf], wsem.at[buf]),
                pltpu.make_async_copy(wdn_hbm.at[e], wdn_buf.at[buf], wsem.at[buf]))

    def gather_copy(blk, buf, i, u):
        tok = tok_ref[blk * MOE_BLOCK + i * SUBLANES + u]
        return pltpu.make_async_copy(h_hbm.at[pl.ds(tok, 1), :], xbuf.at[buf, i, pl.ds(u, 1), :], gsem.at[buf])

    def scatter_copy(blk, buf, i, u):
        dst = dst_ref[blk * MOE_BLOCK + i * SUBLANES + u]
        return pltpu.make_async_copy(obuf.at[buf, i, pl.ds(u, 1), :], y_hbm.at[pl.ds(dst, 1), :], ssem.at[buf])

    def gather_start(blk, buf):
        _for_rows(nvalid_ref[blk], lambda i, u: gather_copy(blk, buf, i, u).start())

    def gather_wait(blk, buf):
        _for_rows(nvalid_ref[blk], lambda i, u: gather_copy(blk, buf, i, u).wait())

    def scatter_start(blk, buf):
        _for_rows(nvalid_ref[blk], lambda i, u: scatter_copy(blk, buf, i, u).start())

    def scatter_wait(blk, buf):
        _for_rows(nvalid_ref[blk], lambda i, u: scatter_copy(blk, buf, i, u).wait())

    @pl.when(b == 0)
    def _():
        for c in weight_copies(exp_ref[0], 0):
            c.start()

    @pl.when((first_ref[b] == 1) & (next_ref[b] >= 0))
    def _():
        for c in weight_copies(next_ref[b], 1 - ws):
            c.start()

    @pl.when(b == 0)
    def _():
        xbuf[...] = jnp.zeros_like(xbuf)

        def place(i, carry):
            for u in range(SUBLANES):
                a = i * SUBLANES + u
                row = start_ref[eid_ref[a]] + rank_ref[a]
                tok_ref[row] = i * (SUBLANES // TOP_K_INNER) + u // TOP_K_INNER
                dst_ref[row] = (u % TOP_K_INNER) * n_tok + i * (SUBLANES // TOP_K_INNER) + u // TOP_K_INNER
            return carry

        lax.fori_loop(0, eid_ref.shape[0] // SUBLANES, place, 0)
        gather_start(0, 0)

    @pl.when(b < last)
    def _():
        gather_start(b + 1, 1 - cur)

    @pl.when(b >= 2)
    def _():
        scatter_wait(b - 2, cur)

    def experts(rows):
        tiles = rows // SUBLANES
        x = xbuf[cur, :tiles].reshape(rows, D_MODEL).astype(BF16)
        out = obuf.at[cur, :tiles]
        step = D_EXPERT // EXPERT_SPLIT
        for j in range(EXPERT_SPLIT):
            hg = _dot(x, wgu_buf[ws, :, j * step:(j + 1) * step].astype(BF16))
            hu = _dot(x, wgu_buf[ws, :, D_EXPERT + j * step:D_EXPERT + (j + 1) * step].astype(BF16))
            act = (hg * jax.nn.sigmoid(hg) * hu).astype(BF16)
            part = _dot(act, wdn_buf[ws, j * step:(j + 1) * step, :].astype(BF16))
            part = part.reshape(tiles, SUBLANES, D_MODEL)
            if j == 0:
                out[...] = part
            else:
                out[...] += part

    @pl.when(nv > 0)
    def _():
        gather_wait(b, cur)

        @pl.when(first_ref[b] == 1)
        def _():
            for c in weight_copies(exp_ref[b], ws):
                c.wait()

        @pl.when(nv > MOE_BLOCK // 2)
        def _():
            experts(MOE_BLOCK)

        @pl.when((nv <= MOE_BLOCK // 2) & (nv > MOE_BLOCK // 4))
        def _():
            experts(MOE_BLOCK // 2)

        @pl.when(nv <= MOE_BLOCK // 4)
        def _():
            experts(MOE_BLOCK // 4)

        scatter_start(b, cur)

    @pl.when(b == last)
    def _():
        @pl.when(b >= 1)
        def _():
            scatter_wait(b - 1, 1 - cur)

        scatter_wait(b, cur)


def _moe(h2, w_gu, w_dn, tables, pad_start, eid, rank):
    t, d = h2.shape
    n_blocks = tables[0].shape[0]
    hbm = pl.BlockSpec(memory_space=pl.ANY)
    return pl.pallas_call(
        _moe_kernel,
        grid_spec=pltpu.PrefetchScalarGridSpec(
            num_scalar_prefetch=len(tables) + 3,
            grid=(n_blocks,),
            in_specs=[hbm, hbm, hbm],
            out_specs=hbm,
            scratch_shapes=[
                pltpu.VMEM((2, MOE_BLOCK // SUBLANES, SUBLANES, d), F32),
                pltpu.VMEM((2, MOE_BLOCK // SUBLANES, SUBLANES, d), F32),
                pltpu.VMEM((2, d, 2 * D_EXPERT), F32),
                pltpu.VMEM((2, D_EXPERT, d), F32),
                pltpu.SMEM((n_blocks * MOE_BLOCK,), jnp.int32),
                pltpu.SMEM((n_blocks * MOE_BLOCK,), jnp.int32),
                pltpu.SemaphoreType.DMA((2,)),
                pltpu.SemaphoreType.DMA((2,)),
                pltpu.SemaphoreType.DMA((2,)),
            ],
        ),
        out_shape=jax.ShapeDtypeStruct((TOP_K_INNER * t, d), F32),
        compiler_params=_params("arbitrary"),
        name="moe_experts",
    )(*tables, pad_start, eid, rank, h2, w_gu, w_dn)


def _block_tables(counts, m_assign):
    padded = ((counts + MOE_BLOCK - 1) // MOE_BLOCK) * MOE_BLOCK
    pad_end = jnp.cumsum(padded)
    pad_start = pad_end - padded
    n_blocks = -(-m_assign // MOE_BLOCK) + N_EXPERTS
    blk_start = jnp.arange(n_blocks, dtype=jnp.int32) * MOE_BLOCK
    blk_exp = jnp.minimum(jnp.sum(pad_end[None, :] <= blk_start[:, None], axis=1), N_EXPERTS - 1).astype(jnp.int32)
    nvalid = jnp.clip(counts[blk_exp] - (blk_start - pad_start[blk_exp]), 0, MOE_BLOCK)
    nvalid = jnp.where(blk_start < pad_end[-1], nvalid, 0).astype(jnp.int32)
    prev_exp = jnp.concatenate([jnp.full((1,), -1, jnp.int32), blk_exp[:-1]])
    first = ((nvalid > 0) & (blk_exp != prev_exp)).astype(jnp.int32)
    wslot = jnp.maximum(jnp.cumsum(first) - 1, 0) % 2
    ids = jnp.arange(N_EXPERTS, dtype=jnp.int32)
    later = jnp.where((ids[None, :] > ids[:, None]) & (counts[None, :] > 0), ids[None, :], N_EXPERTS)
    next_of = jnp.min(later, axis=1)
    next_exp = jnp.where(next_of == N_EXPERTS, -1, next_of)[blk_exp]
    tables = tuple(a.astype(jnp.int32) for a in (blk_exp, nvalid, first, wslot, next_exp))
    return tables, pad_start.astype(jnp.int32)


def _final_kernel(x_ref, ya_ref, yb_ref, rg_ref, g_ref, o_ref):
    x = x_ref[...] + rg_ref[:, 0:1] * ya_ref[...] + rg_ref[:, 1:2] * yb_ref[...]
    o_ref[...] = _rms(x, g_ref[...])


def _final(x1, y2, gates, gain, tm):
    t, d = x1.shape
    nt = t // tm
    return pl.pallas_call(
        _final_kernel,
        grid=(nt,),
        in_specs=[
            pl.BlockSpec((tm, d), lambda i: (i, 0)),
            pl.BlockSpec((tm, d), lambda i: (i, 0)),
            pl.BlockSpec((tm, d), lambda i: (i + nt, 0)),
            pl.BlockSpec((tm, LANES), lambda i: (i, 0)),
            pl.BlockSpec((1, d), lambda i: (0, 0)),
        ],
        out_specs=pl.BlockSpec((tm, d), lambda i: (i, 0)),
        out_shape=jax.ShapeDtypeStruct((t, d), F32),
        compiler_params=_params("parallel"),
        name="final_norm",
    )(x1, y2, y2, gates, gain)


def _pad_cols(a, n):
    return jnp.pad(a, ((0, 0), (0, n - a.shape[1])))


def _rwkv_mix(p_rwkv, mu, w0, wdu, a0, wau, wgu, k_k, k_a, r_k, gn_w, gn_b):
    row = lambda a: a.reshape(1, -1)
    outs = _prep(p_rwkv, _pad_cols(row(mu), RWKV_COLS_PAD), row(w0), wdu, row(a0), wau,
                 jnp.pad(wgu, ((0, GATE_LORA_PAD - GATE_LORA), (0, 0))), row(k_k), row(k_a), row(r_k), tm=256)
    return _scan(*outs, row(gn_w), row(gn_b))


def _mix_and_route(x, y_rwkv, y_attn, w_out, ln2, w_rg, b_rg, w_re, b_re):
    row = lambda a: a.reshape(1, -1)
    wr = _pad_cols(jnp.concatenate([w_rg, w_re], axis=1), LANES)
    br = _pad_cols(row(jnp.concatenate([b_rg, b_re])), LANES)
    return _out_proj(x, y_rwkv, y_attn, w_out.astype(BF16), row(ln2), wr, br, tm=256)


def _experts(h2, route_ints, counts, w_gu, w_dn):
    t = h2.shape[0]
    tables, pad_start = _block_tables(counts[0, :N_EXPERTS], t * TOP_K_INNER)
    eid = route_ints[:, 0:TOP_K_INNER].reshape(-1)
    rank = route_ints[:, TOP_K_INNER:2 * TOP_K_INNER].reshape(-1)
    return _moe(h2, w_gu, w_dn, tables, pad_start, eid, rank)


def _layer(x, positions, ln1, w_in, mu, w0, wdu, a0, wau, wgu, k_k, k_a, r_k, gn_w, gn_b, sinks, w_out, ln2,
           w_rg, b_rg, w_re, b_re, w_gu, w_dn, ln_f):
    t = x.shape[0]
    row = lambda a: a.reshape(1, -1)
    w_bf = w_in.astype(BF16)
    p_rwkv = _in_proj(x, row(ln1), w_bf, tm=min(1024, t), tn=RWKV_COLS_PAD // 3, n=RWKV_COLS_PAD, out_dtype=F32)
    qkv = _in_proj(x, row(ln1), w_bf[:, RWKV_COLS:], tm=min(1024, t), tn=ATTN_COLS, n=ATTN_COLS, out_dtype=BF16)

    y_rwkv = _rwkv_mix(p_rwkv, mu, w0, wdu, a0, wau, wgu, k_k, k_a, r_k, gn_w, gn_b)

    inv_freq = ROPE_THETA ** (-jnp.arange(0, HEAD_DIM, 2, dtype=F32) / HEAD_DIM)
    freq = jnp.tile(inv_freq, 2 * LANES // HEAD_DIM).reshape(1, LANES)
    y_attn = _attn(qkv, positions.reshape(t, 1), freq, sinks)

    x1, h2, route_ints, gates, counts = _mix_and_route(x, y_rwkv, y_attn, w_out, ln2, w_rg, b_rg, w_re, b_re)
    y2 = _experts(h2, route_ints, counts, w_gu, w_dn)
    return _final(x1, y2, gates, row(ln_f), tm=min(512, t))


def kernel(x, positions, ln1, w_in, mu_shift, w_decay0, w_decay_up, a0, w_a_up, w_g_up, k_k, k_a, r_k, gn_w, gn_b,
           sinks, w_out, ln2, w_router_group, b_router_group, w_router_expert, b_router_expert, w_expert_gu,
           w_expert_down, ln_f):
    assert ln1.shape[0] == 1, "one trunk layer"
    outs = [
        _layer(x[i], positions[i], ln1[0], w_in[0], mu_shift[0], w_decay0[0], w_decay_up[0], a0[0], w_a_up[0],
               w_g_up[0], k_k[0], k_a[0], r_k[0], gn_w[0], gn_b[0], sinks[0], w_out[0], ln2[0], w_router_group[0],
               b_router_group[0], w_router_expert[0], b_router_expert[0], w_expert_gu[0], w_expert_down[0], ln_f)
        for i in range(x.shape[0])
    ]
    return jnp.stack(outs, axis=0)
```

```python
import functools
import math

import jax
import jax.numpy as jnp
from jax import lax
from jax.experimental import pallas as pl
from jax.experimental.pallas import tpu as pltpu

D_MODEL = 2048
HEAD_DIM = 64
RWKV_WIDTH = 1024
ATTN_WIDTH = 1024
ATTN_KV_HEADS = 4
ATTN_GROUP = 4
ATTN_KV_WIDTH = 256
WINDOW = 128
ROPE_THETA = 10000.0
DECAY_LORA = 64
AAA_LORA = 64
GATE_LORA = 160
GATE_LORA_PAD = 256
RWKV_COLS = 3 * RWKV_WIDTH + DECAY_LORA + AAA_LORA + GATE_LORA
RWKV_COLS_PAD = 3 * RWKV_WIDTH + DECAY_LORA + AAA_LORA + GATE_LORA_PAD
ATTN_COLS = ATTN_WIDTH + 2 * ATTN_KV_WIDTH
N_GROUPS = 8
EXPERTS_PER_GROUP = 8
N_EXPERTS = 64
TOP_K_INNER = 2
D_EXPERT = 768
MOE_BLOCK = 256
RMS_EPS = 1e-6
RWKV_GN_EPS = 64e-5

LANES = 128
SUBLANES = 8
CHUNK = 64
SCAN_CHUNKS = 8
EXPERT_SPLIT = 3
VMEM_LIMIT = 56 * 1024 * 1024

F32 = jnp.float32
BF16 = jnp.bfloat16


def _dot(a, b):
    return jnp.dot(a, b, preferred_element_type=F32)


def _split(x, parts):
    out = []
    for _ in range(parts - 1):
        hi = x.astype(BF16)
        out.append(hi)
        x = x - hi.astype(F32)
    out.append(x.astype(BF16))
    return out


def _dot_x3(a, b):
    ah, al = _split(a, 2)
    bh, bl = _split(b, 2)
    return _dot(ah, bh) + (_dot(ah, bl) + _dot(al, bh))


def _dot_x3_pre(a, b_hi, b_lo):
    ah, al = _split(a, 2)
    return _dot(ah, b_hi) + (_dot(ah, b_lo) + _dot(al, b_hi))


def _dot_sel(a, sel):
    ah, al = _split(a, 2)
    return _dot(ah, sel) + _dot(al, sel)


def _dot_sel_lhs(sel, b):
    bh, bl = _split(b, 2)
    return _dot(sel, bh) + _dot(sel, bl)


def _dot_nt(a, b):
    return lax.dot_general(a, b, (((1,), (1,)), ((), ())), preferred_element_type=F32)


def _dot_tn(a, b):
    return lax.dot_general(a, b, (((0,), (0,)), ((), ())), preferred_element_type=F32)


def _params(*sem):
    return pltpu.CompilerParams(dimension_semantics=sem, vmem_limit_bytes=VMEM_LIMIT)


def _head_sum_matrix():
    r = lax.broadcasted_iota(jnp.int32, (LANES, LANES), 0) // HEAD_DIM
    c = lax.broadcasted_iota(jnp.int32, (LANES, LANES), 1) // HEAD_DIM
    return (r == c).astype(BF16)


def _head_sums(x, bd):
    xb = x.astype(BF16)
    parts = [_dot(xb[:, j * LANES:(j + 1) * LANES], bd) for j in range(x.shape[1] // LANES)]
    return jnp.concatenate(parts, axis=1)


def _rms(x, gain):
    return x * lax.rsqrt(jnp.mean(x * x, axis=-1, keepdims=True) + RMS_EPS) * gain


IN_PROJ_ROW_CHUNKS = 4


def _in_proj_kernel(x_ref, g_ref, w_ref, o_ref, h_ref):
    j = pl.program_id(1)
    rows = x_ref.shape[0] // IN_PROJ_ROW_CHUNKS

    @pl.when(j == 0)
    def _():
        for c in range(IN_PROJ_ROW_CHUNKS):
            sl = pl.ds(c * rows, rows)
            h = _rms(x_ref[sl, :], g_ref[...]).astype(BF16)
            h_ref[sl, :] = h
            o_ref[sl, :] = _dot(h, w_ref[...]).astype(o_ref.dtype)

    @pl.when(j > 0)
    def _():
        o_ref[...] = _dot(h_ref[...], w_ref[...]).astype(o_ref.dtype)


def _in_proj(x, gain, w, tm, tn, n, out_dtype):
    t, d = x.shape
    return pl.pallas_call(
        _in_proj_kernel,
        grid=(t // tm, n // tn),
        in_specs=[
            pl.BlockSpec((tm, d), lambda i, j: (i, 0)),
            pl.BlockSpec((1, d), lambda i, j: (0, 0)),
            pl.BlockSpec((d, tn), lambda i, j: (0, j)),
        ],
        out_specs=pl.BlockSpec((tm, tn), lambda i, j: (i, j)),
        out_shape=jax.ShapeDtypeStruct((t, n), out_dtype),
        scratch_shapes=[pltpu.VMEM((tm, d), BF16)],
        compiler_params=_params("parallel", "arbitrary"),
        name="in_proj",
    )(x, gain, w)


def _prep_kernel(p_ref, prev_ref, mu_ref, w0_ref, wdu_ref, wdl_ref, a0_ref, wau_ref, wal_ref, wgu_ref, wgl_ref,
                 kk_ref, ka_ref, rk_ref,
                 at_ref, rt_ref, bt_ref, kt_ref, bh_ref, kh_ref, v_ref, gam_ref, g_ref, bv_ref):
    i = pl.program_id(0)
    tm = p_ref.shape[0]
    w = RWKV_WIDTH
    p = p_ref[...]
    prev_row = jnp.where(i == 0, 0.0, prev_ref[7:8, :])
    rolled = pltpu.roll(p, 1, 0)
    row = lax.broadcasted_iota(jnp.int32, (SUBLANES, p.shape[1]), 0)
    shifted = jnp.concatenate([jnp.where(row == 0, prev_row, rolled[:SUBLANES]), rolled[SUBLANES:]], axis=0)
    ps = p + (shifted - p) * mu_ref[...]

    r = ps[:, 0:w]
    k = ps[:, w:2 * w]
    v = ps[:, 2 * w:3 * w]
    wd = ps[:, 3 * w:3 * w + DECAY_LORA]
    ad = ps[:, 3 * w + DECAY_LORA:3 * w + DECAY_LORA + AAA_LORA]
    gd = ps[:, 3 * w + DECAY_LORA + AAA_LORA:]

    z = w0_ref[...] + _dot_x3_pre(jnp.tanh(wd), wdu_ref[...], wdl_ref[...])
    logw = -math.exp(-0.5) * jax.nn.sigmoid(z)
    alr = jax.nn.sigmoid(a0_ref[...] + _dot_x3_pre(ad, wau_ref[...], wal_ref[...]))
    g_ref[...] = _dot_x3_pre(jax.nn.sigmoid(gd), wgu_ref[...], wgl_ref[...])

    bd = _head_sum_matrix()
    kk = k * kk_ref[...]
    kk = kk * lax.rsqrt(jnp.maximum(_head_sums(kk * kk, bd), 1e-24))
    kmod = k * (1.0 + (alr - 1.0) * ka_ref[...])
    b = kk * alr
    bv_ref[...] = _head_sums(r * kmod * rk_ref[...], bd) * v
    v_ref[...] = v.astype(BF16)

    rr = lax.broadcasted_iota(jnp.int32, (tm, tm), 0)
    cc = lax.broadcasted_iota(jnp.int32, (tm, tm), 1)
    tri = ((cc <= rr) & (cc // CHUNK == rr // CHUNK)).astype(BF16)
    cum = _dot_sel_lhs(tri, logw)
    tot_rows = []
    for c in range(tm // CHUNK):
        last = cum[c * CHUNK + CHUNK - 1:c * CHUNK + CHUNK, :]
        gam_ref[c] = jnp.exp(last)
        tot_rows.append(jnp.broadcast_to(last, (CHUNK, w)))
    tot = jnp.concatenate(tot_rows, axis=0)

    e_neg = jnp.exp(-cum)
    e_rem = jnp.exp(tot - cum)
    at_ref[...] = (-kk * jnp.exp(cum - logw)).astype(BF16)
    rt_ref[...] = (r * jnp.exp(cum)).astype(BF16)
    bt_ref[...] = (b * e_neg).astype(BF16)
    kt_ref[...] = (kmod * e_neg).astype(BF16)
    bh_ref[...] = (b * e_rem).astype(BF16)
    kh_ref[...] = (kmod * e_rem).astype(BF16)


def _hi_lo(w):
    hi = w.astype(BF16)
    return hi, (w - hi.astype(F32)).astype(BF16)


def _prep(p_rwkv, mu, w0, wdu, a0, wau, wgu, k_k, k_a, r_k, tm):
    t = p_rwkv.shape[0]
    w = RWKV_WIDTH
    cp = RWKV_COLS_PAD
    nc = tm // CHUNK
    row = lambda i: (i, 0)
    fixed = lambda i: (0, 0)
    vec = pl.BlockSpec((1, w), fixed)
    big_bf = jax.ShapeDtypeStruct((t, w), BF16)
    big_f32 = jax.ShapeDtypeStruct((t, w), F32)
    out_tile = pl.BlockSpec((tm, w), row)
    return pl.pallas_call(
        _prep_kernel,
        grid=(t // tm,),
        in_specs=[
            pl.BlockSpec((tm, cp), row),
            pl.BlockSpec((8, cp), lambda i: (jnp.maximum(i * (tm // 8) - 1, 0), 0)),
            pl.BlockSpec((1, cp), fixed),
            vec, pl.BlockSpec((DECAY_LORA, w), fixed), pl.BlockSpec((DECAY_LORA, w), fixed),
            vec, pl.BlockSpec((AAA_LORA, w), fixed), pl.BlockSpec((AAA_LORA, w), fixed),
            pl.BlockSpec((GATE_LORA_PAD, w), fixed), pl.BlockSpec((GATE_LORA_PAD, w), fixed),
            vec, vec, vec,
        ],
        out_specs=[out_tile] * 7 + [pl.BlockSpec((nc, 1, w), lambda i: (i, 0, 0)), out_tile, out_tile],
        out_shape=[big_bf] * 7 + [jax.ShapeDtypeStruct((t // CHUNK, 1, w), F32), big_f32, big_f32],
        compiler_params=_params("parallel"),
        name="rwkv_prep",
    )(p_rwkv, p_rwkv, mu, w0, *_hi_lo(wdu), a0, *_hi_lo(wau), *_hi_lo(wgu), k_k, k_a, r_k)


def _chunk_factors(nchunks, own, strict, incl, eye, at_ref, rt_ref, bt_ref, kt_ref, bh_ref, kh_ref, v_ref, gam_ref,
                   tick):
    js = range(nchunks)
    n = 2 * CHUNK

    def stack(ref):
        xs = [ref[pl.ds(j * CHUNK, CHUNK), :] for j in js]
        return [jnp.where(own, jnp.concatenate([x, x], axis=0), jnp.zeros((), x.dtype)) for x in xs]

    a_s, r_s, b_s, k_s = stack(at_ref), stack(rt_ref), stack(bt_ref), stack(kt_ref)
    bh_s, kh_s, v_s = stack(bh_ref), stack(kh_ref), stack(v_ref)

    prod = [_dot_nt(jnp.concatenate([a_s[j], r_s[j]], axis=0), jnp.concatenate([b_s[j], k_s[j]], axis=0)) for j in js]
    a_ak = [jnp.where(strict, prod[j][:n, n:], 0.0).astype(BF16) for j in js]
    a_rb = [jnp.where(incl, prod[j][n:, :n], 0.0).astype(BF16) for j in js]
    a_rk = [jnp.where(incl, prod[j][n:, n:], 0.0).astype(BF16) for j in js]

    tick()
    pq = [jnp.concatenate([a_s[j].astype(F32), _dot(a_ak[j], v_s[j])], axis=1) for j in js]
    nk_b = [jnp.where(strict, prod[j][:n, :n], 0.0).astype(BF16) for j in js]
    span = 1
    while True:
        tick()
        pq = [pq[j] + _dot(nk_b[j], pq[j].astype(BF16)) for j in js]
        span *= 2
        if span >= CHUNK:
            break
        tick()
        nk_b = [_dot(nk_b[j], nk_b[j]).astype(BF16) for j in js]
    pq_b = [pq[j].astype(BF16) for j in js]

    zero = jnp.zeros((n, LANES), BF16)
    rhs = [jnp.concatenate([pq_b[j], jnp.concatenate([zero, v_s[j]], axis=1)], axis=0) for j in js]
    mn = [_dot_tn(jnp.concatenate([bh_s[j], kh_s[j]], axis=0), rhs[j]) for j in js]
    yy = [_dot(jnp.concatenate([a_rb[j], a_rk[j]], axis=1), rhs[j]) for j in js]
    out = []
    for j in js:
        m_mat = mn[j][:, :LANES] + jnp.where(eye, jnp.broadcast_to(gam_ref[j], (LANES, LANES)), 0.0)
        n_mat = mn[j][:, LANES:]
        y_c = r_s[j].astype(F32) + yy[j][:, :LANES]
        y_n = yy[j][:, LANES:]
        out.append((m_mat, n_mat, y_c, y_n))
    return out


def _scan_kernel(at_ref, rt_ref, bt_ref, kt_ref, bh_ref, kh_ref, v_ref, gam_ref, g_ref, bv_ref, gnw_ref, gnb_ref,
                 y_ref, h_ref, f_ref, *, steps):
    g = pl.program_id(0)

    @pl.when(g == 0)
    def _():
        h_ref[...] = jnp.zeros_like(h_ref)
        f_ref[...] = jnp.zeros_like(f_ref)

    starts_pair = lax.rem(jnp.maximum(g - 1, 0), steps) == 0
    chain = {"h": jnp.where(starts_pair, 0.0, h_ref[...]), "ys": []}

    def tick():
        j = len(chain["ys"])
        if j < SCAN_CHUNKS:
            h = chain["h"].astype(BF16)
            both = _dot(jnp.concatenate([f_ref[j, 2], f_ref[j, 0]], axis=0).astype(BF16), h)
            y_st = both[:2 * CHUNK] + f_ref[j, 3]
            chain["h"] = both[2 * CHUNK:] + f_ref[j, 1]
            chain["ys"].append(y_st[:CHUNK] + y_st[CHUNK:])

    n = 2 * CHUNK
    own = (lax.broadcasted_iota(jnp.int32, (n, LANES), 1) // HEAD_DIM
           == lax.broadcasted_iota(jnp.int32, (n, LANES), 0) // CHUNK)
    ti = lax.broadcasted_iota(jnp.int32, (n, n), 0)
    si = lax.broadcasted_iota(jnp.int32, (n, n), 1)
    same = (ti // CHUNK) == (si // CHUNK)
    strict = same & (si < ti)
    incl = same & (si <= ti)
    eye = ti == si

    factors = _chunk_factors(SCAN_CHUNKS, own, strict, incl, eye, at_ref, rt_ref, bt_ref, kt_ref, bh_ref, kh_ref,
                             v_ref, gam_ref, tick)
    while len(chain["ys"]) < SCAN_CHUNKS:
        tick()
    h_ref[...] = chain["h"]
    y = jnp.concatenate(chain["ys"], axis=0)

    low = lax.broadcasted_iota(jnp.int32, y.shape, 1) < HEAD_DIM

    def head_mean(x):
        s_low = jnp.sum(jnp.where(low, x, 0.0), axis=-1, keepdims=True)
        s_all = jnp.sum(x, axis=-1, keepdims=True)
        return jnp.where(low, s_low, s_all - s_low) * (1.0 / HEAD_DIM)

    yc = y - head_mean(y)
    var = head_mean(yc * yc)
    yn = yc * lax.rsqrt(var + RWKV_GN_EPS) * gnw_ref[...] + gnb_ref[...]
    y_ref[...] = ((yn + bv_ref[...]) * g_ref[...]).astype(BF16)

    for j, mats in enumerate(factors):
        for i, mat in enumerate(mats):
            f_ref[j, i] = mat


def _scan(at, rt, bt, kt, bh, kh, v, gam, g, bv, gn_w, gn_b):
    t = at.shape[0]
    rows = SCAN_CHUNKS * CHUNK
    steps = t // rows
    total = (RWKV_WIDTH // LANES) * steps

    def current(i):
        i = jnp.minimum(i, total - 1)
        return lax.rem(i, steps), lax.div(i, steps)

    def previous(i):
        i = jnp.maximum(i - 1, 0)
        return lax.rem(i, steps), lax.div(i, steps)

    tile = pl.BlockSpec((rows, LANES), current)
    lagged = pl.BlockSpec((rows, LANES), previous)
    vec = pl.BlockSpec((1, LANES), lambda i: (0, previous(i)[1]))
    gam_spec = pl.BlockSpec((SCAN_CHUNKS, 1, LANES), lambda i: (current(i)[0], 0, current(i)[1]))
    return pl.pallas_call(
        functools.partial(_scan_kernel, steps=steps),
        grid=(total + 1,),
        in_specs=[tile] * 7 + [gam_spec, lagged, lagged, vec, vec],
        out_specs=lagged,
        out_shape=jax.ShapeDtypeStruct((t, RWKV_WIDTH), BF16),
        scratch_shapes=[pltpu.VMEM((LANES, LANES), F32), pltpu.VMEM((SCAN_CHUNKS, 4, 2 * CHUNK, LANES), F32)],
        compiler_params=_params("arbitrary"),
        name="rwkv_scan",
    )(at, rt, bt, kt, bh, kh, v, gam, g, bv, gn_w, gn_b)


ATTN_BLOCKS = 1


def _attn_kernel(sink_ref, qkv_ref, pos_ref, freq_ref, o_ref, kprev_ref, vprev_ref):
    step = pl.program_id(0)
    rows = ATTN_BLOCKS * WINDOW

    @pl.when(step == 0)
    def _():
        kprev_ref[...] = jnp.zeros_like(kprev_ref)
        vprev_ref[...] = jnp.zeros_like(vprev_ref)

    quarter = HEAD_DIM // 2
    groups = LANES // quarter
    prow = rows // groups
    pos = pos_ref[...].astype(F32)
    plane = lax.broadcasted_iota(jnp.int32, (prow, LANES), 1) // quarter
    pos_packed = pos[(groups - 1) * prow:, :]
    for b in range(groups - 2, -1, -1):
        pos_packed = jnp.where(plane == b, pos[b * prow:(b + 1) * prow, :], pos_packed)
    ang = pos_packed * freq_ref[...]

    def unpack(packed):
        rot = [packed] + [pltpu.roll(packed, quarter * j, 1) for j in range(1, groups)]
        blocks = []
        for b in range(groups):
            out = rot[(groups - 1 - b) % groups]
            for k in range(groups - 2, -1, -1):
                out = jnp.where(plane == k, rot[(k - b) % groups], out)
            blocks.append(out)
        return jnp.concatenate(blocks, axis=0)

    cos = unpack(jnp.cos(ang))
    lane = lax.broadcasted_iota(jnp.int32, (rows, LANES), 1)
    first_half = lane % HEAD_DIM < HEAD_DIM // 2
    sin = jnp.where(first_half, -1.0, 1.0) * unpack(jnp.sin(ang))

    def rope(x):
        swapped = jnp.where(first_half, pltpu.roll(x, LANES - HEAD_DIM // 2, 1), pltpu.roll(x, HEAD_DIM // 2, 1))
        return x * cos + swapped * sin

    scale = HEAD_DIM ** -0.5
    qi = lax.broadcasted_iota(jnp.int32, (WINDOW, 2 * WINDOW), 0) + WINDOW
    ki = lax.broadcasted_iota(jnp.int32, (WINDOW, 2 * WINDOW), 1)
    rel = qi - ki
    local = (rel >= 0) & (rel < WINDOW)
    masks = [local & ((step > 0) | (ki >= WINDOW))] + [local] * (ATTN_BLOCKS - 1)

    k_cur = jnp.concatenate([rope(qkv_ref[:, ATTN_WIDTH + j * LANES:ATTN_WIDTH + (j + 1) * LANES].astype(F32))
                             for j in range(ATTN_KV_WIDTH // LANES)], axis=1)
    v_cur = qkv_ref[:, ATTN_WIDTH + ATTN_KV_WIDTH:].astype(F32)
    k_all = jnp.concatenate([kprev_ref[...], k_cur], axis=0)
    v_all = jnp.concatenate([vprev_ref[...], v_cur], axis=0)
    k_heads = [k_all[:, g * HEAD_DIM:(g + 1) * HEAD_DIM].astype(BF16) for g in range(ATTN_KV_HEADS)]
    ones = jnp.ones((v_all.shape[0], HEAD_DIM), BF16)
    v_ones = [jnp.concatenate([v_all[:, g * HEAD_DIM:(g + 1) * HEAD_DIM].astype(BF16), ones], axis=1)
              for g in range(ATTN_KV_HEADS)]

    heads = range(ATTN_WIDTH // HEAD_DIM)
    per_group = LANES // HEAD_DIM
    units = [(h, u) for u in range(ATTN_BLOCKS) for h in heads]
    q_groups = [rope(qkv_ref[:, j * LANES:(j + 1) * LANES].astype(F32)) * scale for j in range(ATTN_WIDTH // LANES)]

    def q_of(h, u):
        cols = slice((h % per_group) * HEAD_DIM, (h % per_group + 1) * HEAD_DIM)
        return q_groups[h // per_group][u * WINDOW:(u + 1) * WINDOW, cols].astype(BF16)

    def band(xs, h, u):
        return xs[h // ATTN_GROUP][u * WINDOW:(u + 2) * WINDOW]

    s = {hu: jnp.where(masks[hu[1]], _dot_nt(q_of(*hu), band(k_heads, *hu)), -jnp.inf) for hu in units}
    pexp, shift = {}, {}
    for h, u in units:
        sink = sink_ref[h]
        m = jnp.maximum(jnp.max(s[h, u], axis=-1, keepdims=True), sink)
        pexp[h, u] = jnp.exp(s[h, u] - m).astype(BF16)
        shift[h, u] = jnp.exp(sink - m)
    ov = {hu: _dot(pexp[hu], band(v_ones, *hu)) for hu in units}
    o = {hu: ov[hu][:, :HEAD_DIM] * (1.0 / (ov[hu][:, HEAD_DIM:] + shift[hu])) for hu in units}
    for u in range(ATTN_BLOCKS):
        for j in range(ATTN_WIDTH // LANES):
            o_ref[u * WINDOW:(u + 1) * WINDOW, j * LANES:(j + 1) * LANES] = jnp.concatenate(
                [o[h, u] for h in range(j * per_group, (j + 1) * per_group)], axis=1).astype(BF16)

    kprev_ref[...] = k_cur[rows - WINDOW:]
    vprev_ref[...] = v_cur[rows - WINDOW:]


def _attn(qkv, pos_col, freq, sinks):
    t = qkv.shape[0]
    rows = ATTN_BLOCKS * WINDOW
    return pl.pallas_call(
        _attn_kernel,
        grid_spec=pltpu.PrefetchScalarGridSpec(
            num_scalar_prefetch=1,
            grid=(t // rows,),
            in_specs=[
                pl.BlockSpec((rows, ATTN_COLS), lambda i, s: (i, 0)),
                pl.BlockSpec((rows, 1), lambda i, s: (i, 0)),
                pl.BlockSpec((1, LANES), lambda i, s: (0, 0)),
            ],
            out_specs=pl.BlockSpec((rows, ATTN_WIDTH), lambda i, s: (i, 0)),
            scratch_shapes=[pltpu.VMEM((WINDOW, ATTN_KV_WIDTH), F32), pltpu.VMEM((WINDOW, ATTN_KV_WIDTH), F32)],
        ),
        out_shape=jax.ShapeDtypeStruct((t, ATTN_WIDTH), BF16),
        compiler_params=_params("arbitrary"),
        name="swa_attn",
    )(sinks, qkv, pos_col, freq)


def _route_stages(lg, run, out):
    tm = lg.shape[0]
    lane = lax.broadcasted_iota(jnp.int32, lg.shape, 1)
    lane_f = lane.astype(F32)
    ninf = -jnp.inf

    def top(vals):
        best = jnp.max(vals, axis=-1, keepdims=True)
        idx = jnp.min(jnp.where(vals == best, lane_f, float(LANES)), axis=-1, keepdims=True)
        return best, idx

    gl = jnp.where(lane < N_GROUPS, lg, ninf)
    gmax, gidx = top(gl)
    pg_top = 1.0 / jnp.sum(jnp.exp(gl - gmax), axis=-1, keepdims=True)
    yield
    first = N_GROUPS + EXPERTS_PER_GROUP * gidx
    el = jnp.where((lane_f >= first) & (lane_f < first + EXPERTS_PER_GROUP), lg, ninf)
    v0, i0 = top(el)
    yield
    v1, i1 = top(jnp.where(lane_f == i0, ninf, el))
    ex = jnp.exp(v1 - v0)
    g0 = pg_top / (1.0 + ex)
    g1 = pg_top * ex / (1.0 + ex)
    e0 = i0 - N_GROUPS
    e1 = i1 - N_GROUPS
    yield

    hot0 = (lane_f == e0).astype(BF16)
    hot1 = (lane_f == e1).astype(BF16)
    rr = lax.broadcasted_iota(jnp.int32, (tm, tm), 0)
    cc = lax.broadcasted_iota(jnp.int32, (tm, tm), 1)
    before = (cc < rr).astype(BF16)
    cum = _dot(before, jnp.concatenate([hot0, hot1], axis=1))
    h0 = hot0.astype(F32)
    h1 = hot1.astype(F32)
    tot0 = jnp.sum(h0, axis=0, keepdims=True)
    tot1 = jnp.sum(h1, axis=0, keepdims=True)
    rank0 = jnp.sum(h0 * (run + cum[:, :LANES]), axis=-1, keepdims=True)
    rank1 = jnp.sum(h1 * (run + tot0 + cum[:, LANES:]), axis=-1, keepdims=True)
    ints = jnp.where(lane == 0, e0, jnp.where(lane == 1, e1, jnp.where(lane == 2, rank0, jnp.where(lane == 3, rank1, 0.0))))
    out["ints"] = ints.astype(jnp.int32)
    out["gates"] = jnp.where(lane == 0, g0, jnp.where(lane == 1, g1, 0.0))
    out["run"] = run + tot0 + tot1
    yield


OUT_PROJ_SPLIT = 4


def _out_proj_kernel(x_ref, ya_ref, yb_ref, wo_ref, g_ref, wr_ref, br_ref,
                     x1_ref, h2_ref, ri_ref, rg_ref, cnt_ref, run_ref, lg_ref):
    s = pl.program_id(0)

    @pl.when(s == 0)
    def _():
        run_ref[...] = jnp.zeros_like(run_ref)
        lg_ref[...] = jnp.zeros_like(lg_ref)

    routed = {}
    stages = _route_stages(lg_ref[...], run_ref[...], routed)
    ya = ya_ref[...]
    yb = yb_ref[...]
    width = D_MODEL // OUT_PROJ_SPLIT
    ssq = 0.0
    for c in range(OUT_PROJ_SPLIT):
        cols = slice(c * width, (c + 1) * width)
        x1 = x_ref[:, cols] + _dot(ya, wo_ref[:RWKV_WIDTH, cols]) + _dot(yb, wo_ref[RWKV_WIDTH:, cols])
        x1_ref[:, cols] = x1
        ssq = ssq + jnp.sum(x1 * x1, axis=-1, keepdims=True)
        next(stages)

    ri_ref[...] = routed["ints"]
    rg_ref[...] = routed["gates"]
    run = jnp.where(s > 0, routed["run"], run_ref[...])
    run_ref[...] = run
    cnt_ref[...] = run.astype(jnp.int32)

    h2 = x1_ref[...] * lax.rsqrt(ssq * (1.0 / D_MODEL) + RMS_EPS) * g_ref[...]
    h2_ref[...] = h2
    lg_ref[...] = _dot_x3(h2, wr_ref[...]) + br_ref[...]


def _out_proj(x, y_rwkv, y_attn, wo, ln2, wr, br, tm):
    t, d = x.shape
    steps = t // tm
    row = lambda i: (jnp.minimum(i, steps - 1), 0)
    lagged = lambda i: (jnp.maximum(i - 1, 0), 0)
    fixed = lambda i: (0, 0)
    return pl.pallas_call(
        _out_proj_kernel,
        grid=(steps + 1,),
        in_specs=[
            pl.BlockSpec((tm, d), row),
            pl.BlockSpec((tm, RWKV_WIDTH), row),
            pl.BlockSpec((tm, ATTN_WIDTH), row),
            pl.BlockSpec((d, d), fixed),
            pl.BlockSpec((1, d), fixed),
            pl.BlockSpec((d, LANES), fixed),
            pl.BlockSpec((1, LANES), fixed),
        ],
        out_specs=[pl.BlockSpec((tm, d), row), pl.BlockSpec((tm, d), row), pl.BlockSpec((tm, LANES), lagged),
                   pl.BlockSpec((tm, LANES), lagged), pl.BlockSpec((1, LANES), fixed)],
        out_shape=[jax.ShapeDtypeStruct((t, d), F32), jax.ShapeDtypeStruct((t, d), F32),
                   jax.ShapeDtypeStruct((t, LANES), jnp.int32), jax.ShapeDtypeStruct((t, LANES), F32),
                   jax.ShapeDtypeStruct((1, LANES), jnp.int32)],
        scratch_shapes=[pltpu.VMEM((1, LANES), F32), pltpu.VMEM((tm, LANES), F32)],
        compiler_params=_params("arbitrary"),
        name="out_proj_router",
    )(x, y_rwkv, y_attn, wo, ln2, wr, br)


def _for_rows(n, fn):
    full = lax.shift_right_logical(n, 3)

    def group(i, carry):
        for u in range(SUBLANES):
            fn(i, u)
        return carry

    def single(r, carry):
        fn(full, r)
        return carry

    lax.fori_loop(0, full, group, 0)
    lax.fori_loop(0, n - full * SUBLANES, single, 0)


def _moe_kernel(exp_ref, nvalid_ref, first_ref, wslot_ref, next_ref, start_ref, eid_ref, rank_ref,
                h_hbm, wgu_hbm, wdn_hbm, y_hbm, xbuf, obuf, wgu_buf, wdn_buf, tok_ref, dst_ref, gsem, ssem, wsem):
    b = pl.program_id(0)
    last = pl.num_programs(0) - 1
    cur = lax.rem(b, 2)
    nv = nvalid_ref[b]
    ws = wslot_ref[b]
    n_tok = h_hbm.shape[0]

    def weight_copies(e, buf):
        return (pltpu.make_async_copy(wgu_hbm.at[e], wgu_buf.at[buf], wsem.at[buf]),
                pltpu.make_async_copy(wdn_hbm.at[e], wdn_buf.at[buf], wsem.at[buf]))

    def gather_copy(blk, buf, i, u):
        tok = tok_ref[blk * MOE_BLOCK + i * SUBLANES + u]
        return pltpu.make_async_copy(h_hbm.at[pl.ds(tok, 1), :], xbuf.at[buf, i, pl.ds(u, 1), :], gsem.at[buf])

    def scatter_copy(blk, buf, i, u):
        dst = dst_ref[blk * MOE_BLOCK + i * SUBLANES + u]
        return pltpu.make_async_copy(obuf.at[buf, i, pl.ds(u, 1), :], y_hbm.at[pl.ds(dst, 1), :], ssem.at[buf])

    def gather_start(blk, buf):
        _for_rows(nvalid_ref[blk], lambda i, u: gather_copy(blk, buf, i, u).start())

    def gather_wait(blk, buf):
        _for_rows(nvalid_ref[blk], lambda i, u: gather_copy(blk, buf, i, u).wait())

    def scatter_start(blk, buf):
        _for_rows(nvalid_ref[blk], lambda i, u: scatter_copy(blk, buf, i, u).start())

    def scatter_wait(blk, buf):
        _for_rows(nvalid_ref[blk], lambda i, u: scatter_copy(blk, buf, i, u).wait())

    @pl.when(b == 0)
    def _():
        for c in weight_copies(exp_ref[0], 0):
            c.start()

    @pl.when((first_ref[b] == 1) & (next_ref[b] >= 0))
    def _():
        for c in weight_copies(next_ref[b], 1 - ws):
            c.start()

    @pl.when(b == 0)
    def _():
        xbuf[...] = jnp.zeros_like(xbuf)

        def place(i, carry):
            for u in range(SUBLANES):
                a = i * SUBLANES + u
                row = start_ref[eid_ref[a]] + rank_ref[a]
                tok_ref[row] = i * (SUBLANES // TOP_K_INNER) + u // TOP_K_INNER
                dst_ref[row] = (u % TOP_K_INNER) * n_tok + i * (SUBLANES // TOP_K_INNER) + u // TOP_K_INNER
            return carry

        lax.fori_loop(0, eid_ref.shape[0] // SUBLANES, place, 0)
        gather_start(0, 0)

    @pl.when(b < last)
    def _():
        gather_start(b + 1, 1 - cur)

    @pl.when(b >= 2)
    def _():
        scatter_wait(b - 2, cur)

    def experts(rows):
        tiles = rows // SUBLANES
        x = xbuf[cur, :tiles].reshape(rows, D_MODEL).astype(BF16)
        out = obuf.at[cur, :tiles]
        step = D_EXPERT // EXPERT_SPLIT
        for j in range(EXPERT_SPLIT):
            hg = _dot(x, wgu_buf[ws, :, j * step:(j + 1) * step].astype(BF16))
            hu = _dot(x, wgu_buf[ws, :, D_EXPERT + j * step:D_EXPERT + (j + 1) * step].astype(BF16))
            act = (hg * jax.nn.sigmoid(hg) * hu).astype(BF16)
            part = _dot(act, wdn_buf[ws, j * step:(j + 1) * step, :].astype(BF16))
            part = part.reshape(tiles, SUBLANES, D_MODEL)
            if j == 0:
                out[...] = part
            else:
                out[...] += part

    @pl.when(nv > 0)
    def _():
        gather_wait(b, cur)

        @pl.when(first_ref[b] == 1)
        def _():
            for c in weight_copies(exp_ref[b], ws):
                c.wait()

        @pl.when(nv > MOE_BLOCK // 2)
        def _():
            experts(MOE_BLOCK)

        @pl.when((nv <= MOE_BLOCK // 2) & (nv > MOE_BLOCK // 4))
        def _():
            experts(MOE_BLOCK // 2)

        @pl.when(nv <= MOE_BLOCK // 4)
        def _():
            experts(MOE_BLOCK // 4)

        scatter_start(b, cur)

    @pl.when(b == last)
    def _():
        @pl.when(b >= 1)
        def _():
            scatter_wait(b - 1, 1 - cur)

        scatter_wait(b, cur)


def _moe(h2, w_gu, w_dn, tables, pad_start, eid, rank):
    t, d = h2.shape
    n_blocks = tables[0].shape[0]
    hbm = pl.BlockSpec(memory_space=pl.ANY)
    return pl.pallas_call(
        _moe_kernel,
        grid_spec=pltpu.PrefetchScalarGridSpec(
            num_scalar_prefetch=len(tables) + 3,
            grid=(n_blocks,),
            in_specs=[hbm, hbm, hbm],
            out_specs=hbm,
            scratch_shapes=[
                pltpu.VMEM((2, MOE_BLOCK // SUBLANES, SUBLANES, d), F32),
                pltpu.VMEM((2, MOE_BLOCK // SUBLANES, SUBLANES, d), F32),
                pltpu.VMEM((2, d, 2 * D_EXPERT), F32),
                pltpu.VMEM((2, D_EXPERT, d), F32),
                pltpu.SMEM((n_blocks * MOE_BLOCK,), jnp.int32),
                pltpu.SMEM((n_blocks * MOE_BLOCK,), jnp.int32),
                pltpu.SemaphoreType.DMA((2,)),
                pltpu.SemaphoreType.DMA((2,)),
                pltpu.SemaphoreType.DMA((2,)),
            ],
        ),
        out_shape=jax.ShapeDtypeStruct((TOP_K_INNER * t, d), F32),
        compiler_params=_params("arbitrary"),
        name="moe_experts",
    )(*tables, pad_start, eid, rank, h2, w_gu, w_dn)


def _block_tables(counts, m_assign):
    padded = ((counts + MOE_BLOCK - 1) // MOE_BLOCK) * MOE_BLOCK
    pad_end = jnp.cumsum(padded)
    pad_start = pad_end - padded
    n_blocks = -(-m_assign // MOE_BLOCK) + N_EXPERTS
    blk_start = jnp.arange(n_blocks, dtype=jnp.int32) * MOE_BLOCK
    blk_exp = jnp.minimum(jnp.sum(pad_end[None, :] <= blk_start[:, None], axis=1), N_EXPERTS - 1).astype(jnp.int32)
    nvalid = jnp.clip(counts[blk_exp] - (blk_start - pad_start[blk_exp]), 0, MOE_BLOCK)
    nvalid = jnp.where(blk_start < pad_end[-1], nvalid, 0).astype(jnp.int32)
    prev_exp = jnp.concatenate([jnp.full((1,), -1, jnp.int32), blk_exp[:-1]])
    first = ((nvalid > 0) & (blk_exp != prev_exp)).astype(jnp.int32)
    wslot = jnp.maximum(jnp.cumsum(first) - 1, 0) % 2
    ids = jnp.arange(N_EXPERTS, dtype=jnp.int32)
    later = jnp.where((ids[None, :] > ids[:, None]) & (counts[None, :] > 0), ids[None, :], N_EXPERTS)
    next_of = jnp.min(later, axis=1)
    next_exp = jnp.where(next_of == N_EXPERTS, -1, next_of)[blk_exp]
    tables = tuple(a.astype(jnp.int32) for a in (blk_exp, nvalid, first, wslot, next_exp))
    return tables, pad_start.astype(jnp.int32)


def _final_kernel(x_ref, ya_ref, yb_ref, rg_ref, g_ref, o_ref):
    x = x_ref[...] + rg_ref[:, 0:1] * ya_ref[...] + rg_ref[:, 1:2] * yb_ref[...]
    o_ref[...] = _rms(x, g_ref[...])


def _final(x1, y2, gates, gain, tm):
    t, d = x1.shape
    nt = t // tm
    return pl.pallas_call(
        _final_kernel,
        grid=(nt,),
        in_specs=[
            pl.BlockSpec((tm, d), lambda i: (i, 0)),
            pl.BlockSpec((tm, d), lambda i: (i, 0)),
            pl.BlockSpec((tm, d), lambda i: (i + nt, 0)),
            pl.BlockSpec((tm, LANES), lambda i: (i, 0)),
            pl.BlockSpec((1, d), lambda i: (0, 0)),
        ],
        out_specs=pl.BlockSpec((tm, d), lambda i: (i, 0)),
        out_shape=jax.ShapeDtypeStruct((t, d), F32),
        compiler_params=_params("parallel"),
        name="final_norm",
    )(x1, y2, y2, gates, gain)


def _pad_cols(a, n):
    return jnp.pad(a, ((0, 0), (0, n - a.shape[1])))


def _rwkv_mix(p_rwkv, mu, w0, wdu, a0, wau, wgu, k_k, k_a, r_k, gn_w, gn_b):
    row = lambda a: a.reshape(1, -1)
    outs = _prep(p_rwkv, _pad_cols(row(mu), RWKV_COLS_PAD), row(w0), wdu, row(a0), wau,
                 jnp.pad(wgu, ((0, GATE_LORA_PAD - GATE_LORA), (0, 0))), row(k_k), row(k_a), row(r_k), tm=256)
    return _scan(*outs, row(gn_w), row(gn_b))


def _mix_and_route(x, y_rwkv, y_attn, w_out, ln2, w_rg, b_rg, w_re, b_re):
    row = lambda a: a.reshape(1, -1)
    wr = _pad_cols(jnp.concatenate([w_rg, w_re], axis=1), LANES)
    br = _pad_cols(row(jnp.concatenate([b_rg, b_re])), LANES)
    return _out_proj(x, y_rwkv, y_attn, w_out.astype(BF16), row(ln2), wr, br, tm=256)


def _experts(h2, route_ints, counts, w_gu, w_dn):
    t = h2.shape[0]
    tables, pad_start = _block_tables(counts[0, :N_EXPERTS], t * TOP_K_INNER)
    eid = route_ints[:, 0:TOP_K_INNER].reshape(-1)
    rank = route_ints[:, TOP_K_INNER:2 * TOP_K_INNER].reshape(-1)
    return _moe(h2, w_gu, w_dn, tables, pad_start, eid, rank)


def _layer(x, positions, ln1, w_in, mu, w0, wdu, a0, wau, wgu, k_k, k_a, r_k, gn_w, gn_b, sinks, w_out, ln2,
           w_rg, b_rg, w_re, b_re, w_gu, w_dn, ln_f):
    t = x.shape[0]
    row = lambda a: a.reshape(1, -1)
    w_bf = w_in.astype(BF16)
    p_rwkv = _in_proj(x, row(ln1), w_bf, tm=min(1024, t), tn=RWKV_COLS_PAD // 3, n=RWKV_COLS_PAD, out_dtype=F32)
    qkv = _in_proj(x, row(ln1), w_bf[:, RWKV_COLS:], tm=min(1024, t), tn=ATTN_COLS, n=ATTN_COLS, out_dtype=BF16)

    y_rwkv = _rwkv_mix(p_rwkv, mu, w0, wdu, a0, wau, wgu, k_k, k_a, r_k, gn_w, gn_b)

    inv_freq = ROPE_THETA ** (-jnp.arange(0, HEAD_DIM, 2, dtype=F32) / HEAD_DIM)
    freq = jnp.tile(inv_freq, 2 * LANES // HEAD_DIM).reshape(1, LANES)
    y_attn = _attn(qkv, positions.reshape(t, 1), freq, sinks)

    x1, h2, route_ints, gates, counts = _mix_and_route(x, y_rwkv, y_attn, w_out, ln2, w_rg, b_rg, w_re, b_re)
    y2 = _experts(h2, route_ints, counts, w_gu, w_dn)
    return _final(x1, y2, gates, row(ln_f), tm=min(512, t))


def kernel(x, positions, ln1, w_in, mu_shift, w_decay0, w_decay_up, a0, w_a_up, w_g_up, k_k, k_a, r_k, gn_w, gn_b,
           sinks, w_out, ln2, w_router_group, b_router_group, w_router_expert, b_router_expert, w_expert_gu,
           w_expert_down, ln_f):
    assert ln1.shape[0] == 1, "one trunk layer"
    outs = [
        _layer(x[i], positions[i], ln1[0], w_in[0], mu_shift[0], w_decay0[0], w_decay_up[0], a0[0], w_a_up[0],
               w_g_up[0], k_k[0], k_a[0], r_k[0], gn_w[0], gn_b[0], sinks[0], w_out[0], ln2[0], w_router_group[0],
               b_router_group[0], w_router_expert[0], b_router_expert[0], w_expert_gu[0], w_expert_down[0], ln_f)
        for i in range(x.shape[0])
    ]
    return jnp.stack(outs, axis=0)
```

```python
import functools
import math

import jax
import jax.numpy as jnp
from jax import lax
from jax.experimental import pallas as pl
from jax.experimental.pallas import tpu as pltpu

D_MODEL = 2048
HEAD_DIM = 64
RWKV_WIDTH = 1024
ATTN_WIDTH = 1024
ATTN_KV_HEADS = 4
ATTN_GROUP = 4
ATTN_KV_WIDTH = 256
WINDOW = 128
ROPE_THETA = 10000.0
DECAY_LORA = 64
AAA_LORA = 64
GATE_LORA = 160
GATE_LORA_PAD = 256
RWKV_COLS = 3 * RWKV_WIDTH + DECAY_LORA + AAA_LORA + GATE_LORA
RWKV_COLS_PAD = 3 * RWKV_WIDTH + DECAY_LORA + AAA_LORA + GATE_LORA_PAD
ATTN_COLS = ATTN_WIDTH + 2 * ATTN_KV_WIDTH
N_GROUPS = 8
EXPERTS_PER_GROUP = 8
N_EXPERTS = 64
TOP_K_INNER = 2
D_EXPERT = 768
MOE_BLOCK = 256
RMS_EPS = 1e-6
RWKV_GN_EPS = 64e-5

LANES = 128
SUBLANES = 8
CHUNK = 64
SCAN_CHUNKS = 8
EXPERT_SPLIT = 3
VMEM_LIMIT = 56 * 1024 * 1024

F32 = jnp.float32
BF16 = jnp.bfloat16


def _dot(a, b):
    return jnp.dot(a, b, preferred_element_type=F32)


def _split(x, parts):
    out = []
    for _ in range(parts - 1):
        hi = x.astype(BF16)
        out.append(hi)
        x = x - hi.astype(F32)
    out.append(x.astype(BF16))
    return out


def _dot_x3(a, b):
    ah, al = _split(a, 2)
    bh, bl = _split(b, 2)
    return _dot(ah, bh) + (_dot(ah, bl) + _dot(al, bh))


def _dot_x3_pre(a, b_hi, b_lo):
    ah, al = _split(a, 2)
    return _dot(ah, b_hi) + (_dot(ah, b_lo) + _dot(al, b_hi))


def _dot_sel(a, sel):
    ah, al = _split(a, 2)
    return _dot(ah, sel) + _dot(al, sel)


def _dot_sel_lhs(sel, b):
    bh, bl = _split(b, 2)
    return _dot(sel, bh) + _dot(sel, bl)


def _dot_nt(a, b):
    return lax.dot_general(a, b, (((1,), (1,)), ((), ())), preferred_element_type=F32)


def _dot_tn(a, b):
    return lax.dot_general(a, b, (((0,), (0,)), ((), ())), preferred_element_type=F32)


def _params(*sem):
    return pltpu.CompilerParams(dimension_semantics=sem, vmem_limit_bytes=VMEM_LIMIT)


def _head_sum_matrix():
    r = lax.broadcasted_iota(jnp.int32, (LANES, LANES), 0) // HEAD_DIM
    c = lax.broadcasted_iota(jnp.int32, (LANES, LANES), 1) // HEAD_DIM
    return (r == c).astype(BF16)


def _head_sums(x, bd):
    xb = x.astype(BF16)
    parts = [_dot(xb[:, j * LANES:(j + 1) * LANES], bd) for j in range(x.shape[1] // LANES)]
    return jnp.concatenate(parts, axis=1)


def _rms(x, gain):
    return x * lax.rsqrt(jnp.mean(x * x, axis=-1, keepdims=True) + RMS_EPS) * gain


IN_PROJ_ROW_CHUNKS = 4


def _in_proj_kernel(x_ref, g_ref, w_ref, o_ref, h_ref):
    j = pl.program_id(1)
    rows = x_ref.shape[0] // IN_PROJ_ROW_CHUNKS

    @pl.when(j == 0)
    def _():
        for c in range(IN_PROJ_ROW_CHUNKS):
            sl = pl.ds(c * rows, rows)
            h = _rms(x_ref[sl, :], g_ref[...]).astype(BF16)
            h_ref[sl, :] = h
            o_ref[sl, :] = _dot(h, w_ref[...]).astype(o_ref.dtype)

    @pl.when(j > 0)
    def _():
        o_ref[...] = _dot(h_ref[...], w_ref[...]).astype(o_ref.dtype)


def _in_proj(x, gain, w, tm, tn, n, out_dtype):
    t, d = x.shape
    return pl.pallas_call(
        _in_proj_kernel,
        grid=(t // tm, n // tn),
        in_specs=[
            pl.BlockSpec((tm, d), lambda i, j: (i, 0)),
            pl.BlockSpec((1, d), lambda i, j: (0, 0)),
            pl.BlockSpec((d, tn), lambda i, j: (0, j)),
        ],
        out_specs=pl.BlockSpec((tm, tn), lambda i, j: (i, j)),
        out_shape=jax.ShapeDtypeStruct((t, n), out_dtype),
        scratch_shapes=[pltpu.VMEM((tm, d), BF16)],
        compiler_params=_params("parallel", "arbitrary"),
        name="in_proj",
    )(x, gain, w)


def _prep_kernel(p_ref, prev_ref, mu_ref, w0_ref, wdu_ref, wdl_ref, a0_ref, wau_ref, wal_ref, wgu_ref, wgl_ref,
                 kk_ref, ka_ref, rk_ref,
                 at_ref, rt_ref, bt_ref, kt_ref, bh_ref, kh_ref, v_ref, gam_ref, g_ref, bv_ref):
    i = pl.program_id(0)
    tm = p_ref.shape[0]
    w = RWKV_WIDTH
    p = p_ref[...]
    prev_row = jnp.where(i == 0, 0.0, prev_ref[7:8, :])
    rolled = pltpu.roll(p, 1, 0)
    row = lax.broadcasted_iota(jnp.int32, (SUBLANES, p.shape[1]), 0)
    shifted = jnp.concatenate([jnp.where(row == 0, prev_row, rolled[:SUBLANES]), rolled[SUBLANES:]], axis=0)
    ps = p + (shifted - p) * mu_ref[...]

    r = ps[:, 0:w]
    k = ps[:, w:2 * w]
    v = ps[:, 2 * w:3 * w]
    wd = ps[:, 3 * w:3 * w + DECAY_LORA]
    ad = ps[:, 3 * w + DECAY_LORA:3 * w + DECAY_LORA + AAA_LORA]
    gd = ps[:, 3 * w + DECAY_LORA + AAA_LORA:]

    z = w0_ref[...] + _dot_x3_pre(jnp.tanh(wd), wdu_ref[...], wdl_ref[...])
    logw = -math.exp(-0.5) * jax.nn.sigmoid(z)
    alr = jax.nn.sigmoid(a0_ref[...] + _dot_x3_pre(ad, wau_ref[...], wal_ref[...]))
    g_ref[...] = _dot_x3_pre(jax.nn.sigmoid(gd), wgu_ref[...], wgl_ref[...])

    bd = _head_sum_matrix()
    kk = k * kk_ref[...]
    kk = kk * lax.rsqrt(jnp.maximum(_head_sums(kk * kk, bd), 1e-24))
    kmod = k * (1.0 + (alr - 1.0) * ka_ref[...])
    b = kk * alr
    bv_ref[...] = _head_sums(r * kmod * rk_ref[...], bd) * v
    v_ref[...] = v.astype(BF16)

    rr = lax.broadcasted_iota(jnp.int32, (tm, tm), 0)
    cc = lax.broadcasted_iota(jnp.int32, (tm, tm), 1)
    tri = ((cc <= rr) & (cc // CHUNK == rr // CHUNK)).astype(BF16)
    cum = _dot_sel_lhs(tri, logw)
    tot_rows = []
    for c in range(tm // CHUNK):
        last = cum[c * CHUNK + CHUNK - 1:c * CHUNK + CHUNK, :]
        gam_ref[c] = jnp.exp(last)
        tot_rows.append(jnp.broadcast_to(last, (CHUNK, w)))
    tot = jnp.concatenate(tot_rows, axis=0)

    e_neg = jnp.exp(-cum)
    e_rem = jnp.exp(tot - cum)
    at_ref[...] = (-kk * jnp.exp(cum - logw)).astype(BF16)
    rt_ref[...] = (r * jnp.exp(cum)).astype(BF16)
    bt_ref[...] = (b * e_neg).astype(BF16)
    kt_ref[...] = (kmod * e_neg).astype(BF16)
    bh_ref[...] = (b * e_rem).astype(BF16)
    kh_ref[...] = (kmod * e_rem).astype(BF16)


def _hi_lo(w):
    hi = w.astype(BF16)
    return hi, (w - hi.astype(F32)).astype(BF16)


def _prep(p_rwkv, mu, w0, wdu, a0, wau, wgu, k_k, k_a, r_k, tm):
    t = p_rwkv.shape[0]
    w = RWKV_WIDTH
    cp = RWKV_COLS_PAD
    nc = tm // CHUNK
    row = lambda i: (i, 0)
    fixed = lambda i: (0, 0)
    vec = pl.BlockSpec((1, w), fixed)
    big_bf = jax.ShapeDtypeStruct((t, w), BF16)
    big_f32 = jax.ShapeDtypeStruct((t, w), F32)
    out_tile = pl.BlockSpec((tm, w), row)
    return pl.pallas_call(
        _prep_kernel,
        grid=(t // tm,),
        in_specs=[
            pl.BlockSpec((tm, cp), row),
            pl.BlockSpec((8, cp), lambda i: (jnp.maximum(i * (tm // 8) - 1, 0), 0)),
            pl.BlockSpec((1, cp), fixed),
            vec, pl.BlockSpec((DECAY_LORA, w), fixed), pl.BlockSpec((DECAY_LORA, w), fixed),
            vec, pl.BlockSpec((AAA_LORA, w), fixed), pl.BlockSpec((AAA_LORA, w), fixed),
            pl.BlockSpec((GATE_LORA_PAD, w), fixed), pl.BlockSpec((GATE_LORA_PAD, w), fixed),
            vec, vec, vec,
        ],
        out_specs=[out_tile] * 7 + [pl.BlockSpec((nc, 1, w), lambda i: (i, 0, 0)), out_tile, out_tile],
        out_shape=[big_bf] * 7 + [jax.ShapeDtypeStruct((t // CHUNK, 1, w), F32), big_f32, big_f32],
        compiler_params=_params("parallel"),
        name="rwkv_prep",
    )(p_rwkv, p_rwkv, mu, w0, *_hi_lo(wdu), a0, *_hi_lo(wau), *_hi_lo(wgu), k_k, k_a, r_k)


def _chunk_factors(nchunks, own, strict, incl, eye, at_ref, rt_ref, bt_ref, kt_ref, bh_ref, kh_ref, v_ref, gam_ref,
                   tick):
    js = range(nchunks)
    n = 2 * CHUNK

    def stack(ref):
        xs = [ref[pl.ds(j * CHUNK, CHUNK), :] for j in js]
        return [jnp.where(own, jnp.concatenate([x, x], axis=0), jnp.zeros((), x.dtype)) for x in xs]

    a_s, r_s, b_s, k_s = stack(at_ref), stack(rt_ref), stack(bt_ref), stack(kt_ref)
    bh_s, kh_s, v_s = stack(bh_ref), stack(kh_ref), stack(v_ref)

    prod = [_dot_nt(jnp.concatenate([a_s[j], r_s[j]], axis=0), jnp.concatenate([b_s[j], k_s[j]], axis=0)) for j in js]
    a_ak = [jnp.where(strict, prod[j][:n, n:], 0.0).astype(BF16) for j in js]
    a_rb = [jnp.where(incl, prod[j][n:, :n], 0.0).astype(BF16) for j in js]
    a_rk = [jnp.where(incl, prod[j][n:, n:], 0.0).astype(BF16) for j in js]

    tick()
    pq = [jnp.concatenate([a_s[j].astype(F32), _dot(a_ak[j], v_s[j])], axis=1) for j in js]
    nk_b = [jnp.where(strict, prod[j][:n, :n], 0.0).astype(BF16) for j in js]
    span = 1
    while True:
        tick()
        pq = [pq[j] + _dot(nk_b[j], pq[j].astype(BF16)) for j in js]
        span *= 2
        if span >= CHUNK:
            break
        tick()
        nk_b = [_dot(nk_b[j], nk_b[j]).astype(BF16) for j in js]
    pq_b = [pq[j].astype(BF16) for j in js]

    zero = jnp.zeros((n, LANES), BF16)
    rhs = [jnp.concatenate([pq_b[j], jnp.concatenate([zero, v_s[j]], axis=1)], axis=0) for j in js]
    mn = [_dot_tn(jnp.concatenate([bh_s[j], kh_s[j]], axis=0), rhs[j]) for j in js]
    yy = [_dot(jnp.concatenate([a_rb[j], a_rk[j]], axis=1), rhs[j]) for j in js]
    out = []
    for j in js:
        m_mat = mn[j][:, :LANES] + jnp.where(eye, jnp.broadcast_to(gam_ref[j], (LANES, LANES)), 0.0)
        n_mat = mn[j][:, LANES:]
        y_c = r_s[j].astype(F32) + yy[j][:, :LANES]
        y_n = yy[j][:, LANES:]
        out.append((m_mat, n_mat, y_c, y_n))
    return out


def _scan_kernel(at_ref, rt_ref, bt_ref, kt_ref, bh_ref, kh_ref, v_ref, gam_ref, g_ref, bv_ref, gnw_ref, gnb_ref,
                 y_ref, h_ref, f_ref, *, steps):
    g = pl.program_id(0)

    @pl.when(g == 0)
    def _():
        h_ref[...] = jnp.zeros_like(h_ref)
        f_ref[...] = jnp.zeros_like(f_ref)

    starts_pair = lax.rem(jnp.maximum(g - 1, 0), steps) == 0
    chain = {"h": jnp.where(starts_pair, 0.0, h_ref[...]), "ys": []}

    def tick():
        j = len(chain["ys"])
        if j < SCAN_CHUNKS:
            h = chain["h"].astype(BF16)
            both = _dot(jnp.concatenate([f_ref[j, 2], f_ref[j, 0]], axis=0).astype(BF16), h)
            y_st = both[:2 * CHUNK] + f_ref[j, 3]
            chain["h"] = both[2 * CHUNK:] + f_ref[j, 1]
            chain["ys"].append(y_st[:CHUNK] + y_st[CHUNK:])

    n = 2 * CHUNK
    own = (lax.broadcasted_iota(jnp.int32, (n, LANES), 1) // HEAD_DIM
           == lax.broadcasted_iota(jnp.int32, (n, LANES), 0) // CHUNK)
    ti = lax.broadcasted_iota(jnp.int32, (n, n), 0)
    si = lax.broadcasted_iota(jnp.int32, (n, n), 1)
    same = (ti // CHUNK) == (si // CHUNK)
    strict = same & (si < ti)
    incl = same & (si <= ti)
    eye = ti == si

    factors = _chunk_factors(SCAN_CHUNKS, own, strict, incl, eye, at_ref, rt_ref, bt_ref, kt_ref, bh_ref, kh_ref,
                             v_ref, gam_ref, tick)
    while len(chain["ys"]) < SCAN_CHUNKS:
        tick()
    h_ref[...] = chain["h"]
    y = jnp.concatenate(chain["ys"], axis=0)

    low = lax.broadcasted_iota(jnp.int32, y.shape, 1) < HEAD_DIM

    def head_mean(x):
        s_low = jnp.sum(jnp.where(low, x, 0.0), axis=-1, keepdims=True)
        s_all = jnp.sum(x, axis=-1, keepdims=True)
        return jnp.where(low, s_low, s_all - s_low) * (1.0 / HEAD_DIM)

    yc = y - head_mean(y)
    var = head_mean(yc * yc)
    yn = yc * lax.rsqrt(var + RWKV_GN_EPS) * gnw_ref[...] + gnb_ref[...]
    y_ref[...] = ((yn + bv_ref[...]) * g_ref[...]).astype(BF16)

    for j, mats in enumerate(factors):
        for i, mat in enumerate(mats):
            f_ref[j, i] = mat


def _scan(at, rt, bt, kt, bh, kh, v, gam, g, bv, gn_w, gn_b):
    t = at.shape[0]
    rows = SCAN_CHUNKS * CHUNK
    steps = t // rows
    total = (RWKV_WIDTH // LANES) * steps

    def current(i):
        i = jnp.minimum(i, total - 1)
        return lax.rem(i, steps), lax.div(i, steps)

    def previous(i):
        i = jnp.maximum(i - 1, 0)
        return lax.rem(i, steps), lax.div(i, steps)

    tile = pl.BlockSpec((rows, LANES), current)
    lagged = pl.BlockSpec((rows, LANES), previous)
    vec = pl.BlockSpec((1, LANES), lambda i: (0, previous(i)[1]))
    gam_spec = pl.BlockSpec((SCAN_CHUNKS, 1, LANES), lambda i: (current(i)[0], 0, current(i)[1]))
    return pl.pallas_call(
        functools.partial(_scan_kernel, steps=steps),
        grid=(total + 1,),
        in_specs=[tile] * 7 + [gam_spec, lagged, lagged, vec, vec],
        out_specs=lagged,
        out_shape=jax.ShapeDtypeStruct((t, RWKV_WIDTH), BF16),
        scratch_shapes=[pltpu.VMEM((LANES, LANES), F32), pltpu.VMEM((SCAN_CHUNKS, 4, 2 * CHUNK, LANES), F32)],
        compiler_params=_params("arbitrary"),
        name="rwkv_scan",
    )(at, rt, bt, kt, bh, kh, v, gam, g, bv, gn_w, gn_b)


ATTN_BLOCKS = 1


def _attn_kernel(sink_ref, qkv_ref, pos_ref, freq_ref, o_ref, kprev_ref, vprev_ref):
    step = pl.program_id(0)
    rows = ATTN_BLOCKS * WINDOW

    @pl.when(step == 0)
    def _():
        kprev_ref[...] = jnp.zeros_like(kprev_ref)
        vprev_ref[...] = jnp.zeros_like(vprev_ref)

    quarter = HEAD_DIM // 2
    groups = LANES // quarter
    prow = rows // groups
    pos = pos_ref[...].astype(F32)
    plane = lax.broadcasted_iota(jnp.int32, (prow, LANES), 1) // quarter
    pos_packed = pos[(groups - 1) * prow:, :]
    for b in range(groups - 2, -1, -1):
        pos_packed = jnp.where(plane == b, pos[b * prow:(b + 1) * prow, :], pos_packed)
    ang = pos_packed * freq_ref[...]

    def unpack(packed):
        rot = [packed] + [pltpu.roll(packed, quarter * j, 1) for j in range(1, groups)]
        blocks = []
        for b in range(groups):
            out = rot[(groups - 1 - b) % groups]
            for k in range(groups - 2, -1, -1):
                out = jnp.where(plane == k, rot[(k - b) % groups], out)
            blocks.append(out)
        return jnp.concatenate(blocks, axis=0)

    cos = unpack(jnp.cos(ang))
    lane = lax.broadcasted_iota(jnp.int32, (rows, LANES), 1)
    first_half = lane % HEAD_DIM < HEAD_DIM // 2
    sin = jnp.where(first_half, -1.0, 1.0) * unpack(jnp.sin(ang))

    def rope(x):
        swapped = jnp.where(first_half, pltpu.roll(x, LANES - HEAD_DIM // 2, 1), pltpu.roll(x, HEAD_DIM // 2, 1))
        return x * cos + swapped * sin

    scale = HEAD_DIM ** -0.5
    qi = lax.broadcasted_iota(jnp.int32, (WINDOW, 2 * WINDOW), 0) + WINDOW
    ki = lax.broadcasted_iota(jnp.int32, (WINDOW, 2 * WINDOW), 1)
    rel = qi - ki
    local = (rel >= 0) & (rel < WINDOW)
    masks = [local & ((step > 0) | (ki >= WINDOW))] + [local] * (ATTN_BLOCKS - 1)

    k_cur = jnp.concatenate([rope(qkv_ref[:, ATTN_WIDTH + j * LANES:ATTN_WIDTH + (j + 1) * LANES].astype(F32))
                             for j in range(ATTN_KV_WIDTH // LANES)], axis=1)
    v_cur = qkv_ref[:, ATTN_WIDTH + ATTN_KV_WIDTH:].astype(F32)
    k_all = jnp.concatenate([kprev_ref[...], k_cur], axis=0)
    v_all = jnp.concatenate([vprev_ref[...], v_cur], axis=0)
    k_heads = [k_all[:, g * HEAD_DIM:(g + 1) * HEAD_DIM].astype(BF16) for g in range(ATTN_KV_HEADS)]
    ones = jnp.ones((v_all.shape[0], HEAD_DIM), BF16)
    v_ones = [jnp.concatenate([v_all[:, g * HEAD_DIM:(g + 1) * HEAD_DIM].astype(BF16), ones], axis=1)
              for g in range(ATTN_KV_HEADS)]

    heads = range(ATTN_WIDTH // HEAD_DIM)
    per_group = LANES // HEAD_DIM
    units = [(h, u) for u in range(ATTN_BLOCKS) for h in heads]
    q_groups = [rope(qkv_ref[:, j * LANES:(j + 1) * LANES].astype(F32)) * scale for j in range(ATTN_WIDTH // LANES)]

    def q_of(h, u):
        cols = slice((h % per_group) * HEAD_DIM, (h % per_group + 1) * HEAD_DIM)
        return q_groups[h // per_group][u * WINDOW:(u + 1) * WINDOW, cols].astype(BF16)

    def band(xs, h, u):
        return xs[h // ATTN_GROUP][u * WINDOW:(u + 2) * WINDOW]

    s = {hu: jnp.where(masks[hu[1]], _dot_nt(q_of(*hu), band(k_heads, *hu)), -jnp.inf) for hu in units}
    pexp, shift = {}, {}
    for h, u in units:
        sink = sink_ref[h]
        m = jnp.maximum(jnp.max(s[h, u], axis=-1, keepdims=True), sink)
        pexp[h, u] = jnp.exp(s[h, u] - m).astype(BF16)
        shift[h, u] = jnp.exp(sink - m)
    ov = {hu: _dot(pexp[hu], band(v_ones, *hu)) for hu in units}
    o = {hu: ov[hu][:, :HEAD_DIM] * (1.0 / (ov[hu][:, HEAD_DIM:] + shift[hu])) for hu in units}
    for u in range(ATTN_BLOCKS):
        for j in range(ATTN_WIDTH // LANES):
            o_ref[u * WINDOW:(u + 1) * WINDOW, j * LANES:(j + 1) * LANES] = jnp.concatenate(
                [o[h, u] for h in range(j * per_group, (j + 1) * per_group)], axis=1).astype(BF16)

    kprev_ref[...] = k_cur[rows - WINDOW:]
    vprev_ref[...] = v_cur[rows - WINDOW:]


def _attn(qkv, pos_col, freq, sinks):
    t = qkv.shape[0]
    rows = ATTN_BLOCKS * WINDOW
    return pl.pallas_call(
        _attn_kernel,
        grid_spec=pltpu.PrefetchScalarGridSpec(
            num_scalar_prefetch=1,
            grid=(t // rows,),
            in_specs=[
                pl.BlockSpec((rows, ATTN_COLS), lambda i, s: (i, 0)),
                pl.BlockSpec((rows, 1), lambda i, s: (i, 0)),
                pl.BlockSpec((1, LANES), lambda i, s: (0, 0)),
            ],
            out_specs=pl.BlockSpec((rows, ATTN_WIDTH), lambda i, s: (i, 0)),
            scratch_shapes=[pltpu.VMEM((WINDOW, ATTN_KV_WIDTH), F32), pltpu.VMEM((WINDOW, ATTN_KV_WIDTH), F32)],
        ),
        out_shape=jax.ShapeDtypeStruct((t, ATTN_WIDTH), BF16),
        compiler_params=_params("arbitrary"),
        name="swa_attn",
    )(sinks, qkv, pos_col, freq)


def _route_stages(lg, run, out):
    tm = lg.shape[0]
    lane = lax.broadcasted_iota(jnp.int32, lg.shape, 1)
    lane_f = lane.astype(F32)
    ninf = -jnp.inf

    def top(vals):
        best = jnp.max(vals, axis=-1, keepdims=True)
        idx = jnp.min(jnp.where(vals == best, lane_f, float(LANES)), axis=-1, keepdims=True)
        return best, idx

    gl = jnp.where(lane < N_GROUPS, lg, ninf)
    gmax, gidx = top(gl)
    pg_top = 1.0 / jnp.sum(jnp.exp(gl - gmax), axis=-1, keepdims=True)
    yield
    first = N_GROUPS + EXPERTS_PER_GROUP * gidx
    el = jnp.where((lane_f >= first) & (lane_f < first + EXPERTS_PER_GROUP), lg, ninf)
    v0, i0 = top(el)
    yield
    v1, i1 = top(jnp.where(lane_f == i0, ninf, el))
    ex = jnp.exp(v1 - v0)
    g0 = pg_top / (1.0 + ex)
    g1 = pg_top * ex / (1.0 + ex)
    e0 = i0 - N_GROUPS
    e1 = i1 - N_GROUPS
    yield

    hot0 = (lane_f == e0).astype(BF16)
    hot1 = (lane_f == e1).astype(BF16)
    rr = lax.broadcasted_iota(jnp.int32, (tm, tm), 0)
    cc = lax.broadcasted_iota(jnp.int32, (tm, tm), 1)
    before = (cc < rr).astype(BF16)
    cum = _dot(before, jnp.concatenate([hot0, hot1], axis=1))
    h0 = hot0.astype(F32)
    h1 = hot1.astype(F32)
    tot0 = jnp.sum(h0, axis=0, keepdims=True)
    tot1 = jnp.sum(h1, axis=0, keepdims=True)
    rank0 = jnp.sum(h0 * (run + cum[:, :LANES]), axis=-1, keepdims=True)
    rank1 = jnp.sum(h1 * (run + tot0 + cum[:, LANES:]), axis=-1, keepdims=True)
    ints = jnp.where(lane == 0, e0, jnp.where(lane == 1, e1, jnp.where(lane == 2, rank0, jnp.where(lane == 3, rank1, 0.0))))
    out["ints"] = ints.astype(jnp.int32)
    out["gates"] = jnp.where(lane == 0, g0, jnp.where(lane == 1, g1, 0.0))
    out["run"] = run + tot0 + tot1
    yield


OUT_PROJ_SPLIT = 4


def _out_proj_kernel(x_ref, ya_ref, yb_ref, wo_ref, g_ref, wr_ref, br_ref,
                     x1_ref, h2_ref, ri_ref, rg_ref, cnt_ref, run_ref, lg_ref):
    s = pl.program_id(0)

    @pl.when(s == 0)
    def _():
        run_ref[...] = jnp.zeros_like(run_ref)
        lg_ref[...] = jnp.zeros_like(lg_ref)

    routed = {}
    stages = _route_stages(lg_ref[...], run_ref[...], routed)
    ya = ya_ref[...]
    yb = yb_ref[...]
    width = D_MODEL // OUT_PROJ_SPLIT
    ssq = 0.0
    for c in range(OUT_PROJ_SPLIT):
        cols = slice(c * width, (c + 1) * width)
        x1 = x_ref[:, cols] + _dot(ya, wo_ref[:RWKV_WIDTH, cols]) + _dot(yb, wo_ref[RWKV_WIDTH:, cols])
        x1_ref[:, cols] = x1
        ssq = ssq + jnp.sum(x1 * x1, axis=-1, keepdims=True)
        next(stages)

    ri_ref[...] = routed["ints"]
    rg_ref[...] = routed["gates"]
    run = jnp.where(s > 0, routed["run"], run_ref[...])
    run_ref[...] = run
    cnt_ref[...] = run.astype(jnp.int32)

    h2 = x1_ref[...] * lax.rsqrt(ssq * (1.0 / D_MODEL) + RMS_EPS) * g_ref[...]
    h2_ref[...] = h2
    lg_ref[...] = _dot_x3(h2, wr_ref[...]) + br_ref[...]


def _out_proj(x, y_rwkv, y_attn, wo, ln2, wr, br, tm):
    t, d = x.shape
    steps = t // tm
    row = lambda i: (jnp.minimum(i, steps - 1), 0)
    lagged = lambda i: (jnp.maximum(i - 1, 0), 0)
    fixed = lambda i: (0, 0)
    return pl.pallas_call(
        _out_proj_kernel,
        grid=(steps + 1,),
        in_specs=[
            pl.BlockSpec((tm, d), row),
            pl.BlockSpec((tm, RWKV_WIDTH), row),
            pl.BlockSpec((tm, ATTN_WIDTH), row),
            pl.BlockSpec((d, d), fixed),
            pl.BlockSpec((1, d), fixed),
            pl.BlockSpec((d, LANES), fixed),
            pl.BlockSpec((1, LANES), fixed),
        ],
        out_specs=[pl.BlockSpec((tm, d), row), pl.BlockSpec((tm, d), row), pl.BlockSpec((tm, LANES), lagged),
                   pl.BlockSpec((tm, LANES), lagged), pl.BlockSpec((1, LANES), fixed)],
        out_shape=[jax.ShapeDtypeStruct((t, d), F32), jax.ShapeDtypeStruct((t, d), F32),
                   jax.ShapeDtypeStruct((t, LANES), jnp.int32), jax.ShapeDtypeStruct((t, LANES), F32),
                   jax.ShapeDtypeStruct((1, LANES), jnp.int32)],
        scratch_shapes=[pltpu.VMEM((1, LANES), F32), pltpu.VMEM((tm, LANES), F32)],
        compiler_params=_params("arbitrary"),
        name="out_proj_router",
    )(x, y_rwkv, y_attn, wo, ln2, wr, br)


def _for_rows(n, fn):
    full = lax.shift_right_logical(n, 3)

    def group(i, carry):
        for u in range(SUBLANES):
            fn(i, u)
        return carry

    def single(r, carry):
        fn(full, r)
        return carry

    lax.fori_loop(0, full, group, 0)
    lax.fori_loop(0, n - full * SUBLANES, single, 0)


def _moe_kernel(exp_ref, nvalid_ref, first_ref, wslot_ref, next_ref, start_ref, eid_ref, rank_ref,
                h_hbm, wgu_hbm, wdn_hbm, y_hbm, xbuf, obuf, wgu_buf, wdn_buf, tok_ref, dst_ref, gsem, ssem, wsem):
    b = pl.program_id(0)
    last = pl.num_programs(0) - 1
    cur = lax.rem(b, 2)
    nv = nvalid_ref[b]
    ws = wslot_ref[b]
    n_tok = h_hbm.shape[0]

    def weight_copies(e, buf):
        return (pltpu.make_async_copy(wgu_hbm.at[e], wgu_buf.at[buf], wsem.at[buf]),
                pltpu.make_async_copy(wdn_hbm.at[e], wdn_buf.at[buf], wsem.at[buf]))

    def gather_copy(blk, buf, i, u):
        tok = tok_ref[blk * MOE_BLOCK + i * SUBLANES + u]
        return pltpu.make_async_copy(h_hbm.at[pl.ds(tok, 1), :], xbuf.at[buf, i, pl.ds(u, 1), :], gsem.at[buf])

    def scatter_copy(blk, buf, i, u):
        dst = dst_ref[blk * MOE_BLOCK + i * SUBLANES + u]
        return pltpu.make_async_copy(obuf.at[buf, i, pl.ds(u, 1), :], y_hbm.at[pl.ds(dst, 1), :], ssem.at[buf])

    def gather_start(blk, buf):
        _for_rows(nvalid_ref[blk], lambda i, u: gather_copy(blk, buf, i, u).start())

    def gather_wait(blk, buf):
        _for_rows(nvalid_ref[blk], lambda i, u: gather_copy(blk, buf, i, u).wait())

    def scatter_start(blk, buf):
        _for_rows(nvalid_ref[blk],
                  lambda i, u: scatter_copy(blk, buf, i, u).start(priority=u % 2 if isinstance(u, int) else 0))

    def scatter_wait(blk, buf):
        _for_rows(nvalid_ref[blk], lambda i, u: scatter_copy(blk, buf, i, u).wait())

    @pl.when(b == 0)
    def _():
        for c in weight_copies(exp_ref[0], 0):
            c.start(priority=1)

    @pl.when((first_ref[b] == 1) & (next_ref[b] >= 0))
    def _():
        for c in weight_copies(next_ref[b], 1 - ws):
            c.start(priority=1)

    @pl.when(b == 0)
    def _():
        xbuf[...] = jnp.zeros_like(xbuf)

        def place(i, carry):
            for u in range(SUBLANES):
                a = i * SUBLANES + u
                row = start_ref[eid_ref[a]] + rank_ref[a]
                tok_ref[row] = i * (SUBLANES // TOP_K_INNER) + u // TOP_K_INNER
                dst_ref[row] = (u % TOP_K_INNER) * n_tok + i * (SUBLANES // TOP_K_INNER) + u // TOP_K_INNER
            return carry

        lax.fori_loop(0, eid_ref.shape[0] // SUBLANES, place, 0)
        gather_start(0, 0)

    @pl.when(b < last)
    def _():
        gather_start(b + 1, 1 - cur)

    @pl.when(b >= 2)
    def _():
        scatter_wait(b - 2, cur)

    def experts(rows):
        tiles = rows // SUBLANES
        x = xbuf[cur, :tiles].reshape(rows, D_MODEL).astype(BF16)
        out = obuf.at[cur, :tiles]
        step = D_EXPERT // EXPERT_SPLIT
        for j in range(EXPERT_SPLIT):
            hg = _dot(x, wgu_buf[ws, :, j * step:(j + 1) * step].astype(BF16))
            hu = _dot(x, wgu_buf[ws, :, D_EXPERT + j * step:D_EXPERT + (j + 1) * step].astype(BF16))
            act = (hg * jax.nn.sigmoid(hg) * hu).astype(BF16)
            part = _dot(act, wdn_buf[ws, j * step:(j + 1) * step, :].astype(BF16))
            part = part.reshape(tiles, SUBLANES, D_MODEL)
            if j == 0:
                out[...] = part
            else:
                out[...] += part

    @pl.when(nv > 0)
    def _():
        gather_wait(b, cur)

        @pl.when(first_ref[b] == 1)
        def _():
            for c in weight_copies(exp_ref[b], ws):
                c.wait()

        @pl.when(nv > MOE_BLOCK // 2)
        def _():
            experts(MOE_BLOCK)

        @pl.when((nv <= MOE_BLOCK // 2) & (nv > MOE_BLOCK // 4))
        def _():
            experts(MOE_BLOCK // 2)

        @pl.when(nv <= MOE_BLOCK // 4)
        def _():
            experts(MOE_BLOCK // 4)

        scatter_start(b, cur)

    @pl.when(b == last)
    def _():
        @pl.when(b >= 1)
        def _():
            scatter_wait(b - 1, 1 - cur)

        scatter_wait(b, cur)


def _moe(h2, w_gu, w_dn, tables, pad_start, eid, rank):
    t, d = h2.shape
    n_blocks = tables[0].shape[0]
    hbm = pl.BlockSpec(memory_space=pl.ANY)
    return pl.pallas_call(
        _moe_kernel,
        grid_spec=pltpu.PrefetchScalarGridSpec(
            num_scalar_prefetch=len(tables) + 3,
            grid=(n_blocks,),
            in_specs=[hbm, hbm, hbm],
            out_specs=hbm,
            scratch_shapes=[
                pltpu.VMEM((2, MOE_BLOCK // SUBLANES, SUBLANES, d), F32),
                pltpu.VMEM((2, MOE_BLOCK // SUBLANES, SUBLANES, d), F32),
                pltpu.VMEM((2, d, 2 * D_EXPERT), F32),
                pltpu.VMEM((2, D_EXPERT, d), F32),
                pltpu.SMEM((n_blocks * MOE_BLOCK,), jnp.int32),
                pltpu.SMEM((n_blocks * MOE_BLOCK,), jnp.int32),
                pltpu.SemaphoreType.DMA((2,)),
                pltpu.SemaphoreType.DMA((2,)),
                pltpu.SemaphoreType.DMA((2,)),
            ],
        ),
        out_shape=jax.ShapeDtypeStruct((TOP_K_INNER * t, d), F32),
        compiler_params=_params("arbitrary"),
        name="moe_experts",
    )(*tables, pad_start, eid, rank, h2, w_gu, w_dn)


def _block_tables(counts, m_assign):
    padded = ((counts + MOE_BLOCK - 1) // MOE_BLOCK) * MOE_BLOCK
    pad_end = jnp.cumsum(padded)
    pad_start = pad_end - padded
    n_blocks = -(-m_assign // MOE_BLOCK) + N_EXPERTS
    blk_start = jnp.arange(n_blocks, dtype=jnp.int32) * MOE_BLOCK
    blk_exp = jnp.minimum(jnp.sum(pad_end[None, :] <= blk_start[:, None], axis=1), N_EXPERTS - 1).astype(jnp.int32)
    nvalid = jnp.clip(counts[blk_exp] - (blk_start - pad_start[blk_exp]), 0, MOE_BLOCK)
    nvalid = jnp.where(blk_start < pad_end[-1], nvalid, 0).astype(jnp.int32)
    prev_exp = jnp.concatenate([jnp.full((1,), -1, jnp.int32), blk_exp[:-1]])
    first = ((nvalid > 0) & (blk_exp != prev_exp)).astype(jnp.int32)
    wslot = jnp.maximum(jnp.cumsum(first) - 1, 0) % 2
    ids = jnp.arange(N_EXPERTS, dtype=jnp.int32)
    later = jnp.where((ids[None, :] > ids[:, None]) & (counts[None, :] > 0), ids[None, :], N_EXPERTS)
    next_of = jnp.min(later, axis=1)
    next_exp = jnp.where(next_of == N_EXPERTS, -1, next_of)[blk_exp]
    tables = tuple(a.astype(jnp.int32) for a in (blk_exp, nvalid, first, wslot, next_exp))
    return tables, pad_start.astype(jnp.int32)


def _final_kernel(x_ref, ya_ref, yb_ref, rg_ref, g_ref, o_ref):
    x = x_ref[...] + rg_ref[:, 0:1] * ya_ref[...] + rg_ref[:, 1:2] * yb_ref[...]
    o_ref[...] = _rms(x, g_ref[...])


def _final(x1, y2, gates, gain, tm):
    t, d = x1.shape
    nt = t // tm
    return pl.pallas_call(
        _final_kernel,
        grid=(nt,),
        in_specs=[
            pl.BlockSpec((tm, d), lambda i: (i, 0)),
            pl.BlockSpec((tm, d), lambda i: (i, 0)),
            pl.BlockSpec((tm, d), lambda i: (i + nt, 0)),
            pl.BlockSpec((tm, LANES), lambda i: (i, 0)),
            pl.BlockSpec((1, d), lambda i: (0, 0)),
        ],
        out_specs=pl.BlockSpec((tm, d), lambda i: (i, 0)),
        out_shape=jax.ShapeDtypeStruct((t, d), F32),
        compiler_params=_params("parallel"),
        name="final_norm",
    )(x1, y2, y2, gates, gain)


def _pad_cols(a, n):
    return jnp.pad(a, ((0, 0), (0, n - a.shape[1])))


def _rwkv_mix(p_rwkv, mu, w0, wdu, a0, wau, wgu, k_k, k_a, r_k, gn_w, gn_b):
    row = lambda a: a.reshape(1, -1)
    outs = _prep(p_rwkv, _pad_cols(row(mu), RWKV_COLS_PAD), row(w0), wdu, row(a0), wau,
                 jnp.pad(wgu, ((0, GATE_LORA_PAD - GATE_LORA), (0, 0))), row(k_k), row(k_a), row(r_k), tm=256)
    return _scan(*outs, row(gn_w), row(gn_b))


def _mix_and_route(x, y_rwkv, y_attn, w_out, ln2, w_rg, b_rg, w_re, b_re):
    row = lambda a: a.reshape(1, -1)
    wr = _pad_cols(jnp.concatenate([w_rg, w_re], axis=1), LANES)
    br = _pad_cols(row(jnp.concatenate([b_rg, b_re])), LANES)
    return _out_proj(x, y_rwkv, y_attn, w_out.astype(BF16), row(ln2), wr, br, tm=256)


def _experts(h2, route_ints, counts, w_gu, w_dn):
    t = h2.shape[0]
    tables, pad_start = _block_tables(counts[0, :N_EXPERTS], t * TOP_K_INNER)
    eid = route_ints[:, 0:TOP_K_INNER].reshape(-1)
    rank = route_ints[:, TOP_K_INNER:2 * TOP_K_INNER].reshape(-1)
    return _moe(h2, w_gu, w_dn, tables, pad_start, eid, rank)


def _layer(x, positions, ln1, w_in, mu, w0, wdu, a0, wau, wgu, k_k, k_a, r_k, gn_w, gn_b, sinks, w_out, ln2,
           w_rg, b_rg, w_re, b_re, w_gu, w_dn, ln_f):
    t = x.shape[0]
    row = lambda a: a.reshape(1, -1)
    w_bf = w_in.astype(BF16)
    p_rwkv = _in_proj(x, row(ln1), w_bf, tm=min(1024, t), tn=RWKV_COLS_PAD // 3, n=RWKV_COLS_PAD, out_dtype=F32)
    qkv = _in_proj(x, row(ln1), w_bf[:, RWKV_COLS:], tm=min(1024, t), tn=ATTN_COLS, n=ATTN_COLS, out_dtype=BF16)

    y_rwkv = _rwkv_mix(p_rwkv, mu, w0, wdu, a0, wau, wgu, k_k, k_a, r_k, gn_w, gn_b)

    inv_freq = ROPE_THETA ** (-jnp.arange(0, HEAD_DIM, 2, dtype=F32) / HEAD_DIM)
    freq = jnp.tile(inv_freq, 2 * LANES // HEAD_DIM).reshape(1, LANES)
    y_attn = _attn(qkv, positions.reshape(t, 1), freq, sinks)

    x1, h2, route_ints, gates, counts = _mix_and_route(x, y_rwkv, y_attn, w_out, ln2, w_rg, b_rg, w_re, b_re)
    y2 = _experts(h2, route_ints, counts, w_gu, w_dn)
    return _final(x1, y2, gates, row(ln_f), tm=min(512, t))


def kernel(x, positions, ln1, w_in, mu_shift, w_decay0, w_decay_up, a0, w_a_up, w_g_up, k_k, k_a, r_k, gn_w, gn_b,
           sinks, w_out, ln2, w_router_group, b_router_group, w_router_expert, b_router_expert, w_expert_gu,
           w_expert_down, ln_f):
    assert ln1.shape[0] == 1, "one trunk layer"
    outs = [
        _layer(x[i], positions[i], ln1[0], w_in[0], mu_shift[0], w_decay0[0], w_decay_up[0], a0[0], w_a_up[0],
               w_g_up[0], k_k[0], k_a[0], r_k[0], gn_w[0], gn_b[0], sinks[0], w_out[0], ln2[0], w_router_group[0],
               b_router_group[0], w_router_expert[0], b_router_expert[0], w_expert_gu[0], w_expert_down[0], ln_f)
        for i in range(x.shape[0])
    ]
    return jnp.stack(outs, axis=0)
```
